```python
import math
import jax
import jax.numpy as jnp
from jax import lax
import numpy as np

D_MODEL = 1024
BATCH = 4
SEQ = 8192
DEPTH = 2

GRID_W = 64
CTX_LEN = 256
N_MIXERS = 2
N_HY_LAYERS = (DEPTH + N_MIXERS - 1) // N_MIXERS
N_ML_LAYERS = DEPTH // N_MIXERS
EPS = 1e-6
SHORT_W = 3

HY_EMB_BANDS = 16
HY_EMB_DIM = 1 + 2 * HY_EMB_BANDS
HY_FILTER_WIDTH = 64
HY_MAX_DECAY = math.log(1e-2) / 0.3
HY_MIN_DECAY = math.log(1e-2) / 1.5

ML_INNER = 2 * D_MODEL
ML_HEADS = 4
ML_HEAD_DIM = ML_INNER // ML_HEADS
ML_QKV_BLOCK = 4
ML_N_BLOCKS = ML_INNER // ML_QKV_BLOCK
ML_CHUNK = 128
ML_NORM_EPS = 1e-5

FFN_HIDDEN = 2816

kernel_name = 'hyena_mlstm_prefix_dit'


def _f32(a):
    return a.astype(jnp.float32)


def rmsnorm(x, g):
    x32 = _f32(x)
    y = x32 * lax.rsqrt(jnp.mean(x32 * x32, axis=-1, keepdims=True) + EPS)
    return (y * _f32(g)).astype(x.dtype)


def modulate(h, shift, scale):
    return h * (1.0 + scale) + shift


def adaln_params(cond, w, b):
    mod = jax.nn.silu(cond) @ w + b
    return jnp.split(mod[..., None, :], 6, axis=-1)


def short_conv(x, w, b):
    xp = jnp.pad(x, ((0, 0), (1, 1), (0, 0)))
    return w[0] * xp[:, :-2] + w[1] * xp[:, 1:-1] + w[2] * xp[:, 2:] + b


def grid_dwconv(x, w, b, rows, cols):
    B, L, C = x.shape
    img = x.reshape(B, rows, cols, C)
    y = lax.conv_general_dilated(img, w.astype(x.dtype)[:, :, None, :], window_strides=(1, 1),
                                 padding='SAME', dimension_numbers=('NHWC', 'HWIO', 'NHWC'),
                                 feature_group_count=C)
    return y.reshape(B, L, C) + b


def conv_ffn(u, w_up, conv_w, conv_b, w_down, rows, cols):
    a, g = jnp.split(u @ w_up, 2, axis=-1)
    g = grid_dwconv(g, conv_w, conv_b, rows, cols)
    return (jax.nn.silu(g) * a) @ w_down


def hyena_filters(L, f_w1, f_b1, f_w2, f_b2, f_w3, f_b3, f_wout, freq):
    t = jnp.linspace(0.0, 1.0, L, dtype=jnp.float32)[:, None]
    w = 2.0 * math.pi * jnp.arange(L, dtype=jnp.float32)[:, None] / L
    bands = jnp.linspace(1e-4, HY_EMB_BANDS - 1, HY_EMB_BANDS, dtype=jnp.float32)[None, :]
    pos = jnp.concatenate([t, jnp.cos(bands * w), -jnp.sin(bands * w)], axis=-1)
    fr = _f32(freq)
    h = jnp.sin(fr * (pos @ _f32(f_w1) + _f32(f_b1)))
    h = jnp.sin(fr * (h @ _f32(f_w2) + _f32(f_b2)))
    h = jnp.sin(fr * (h @ _f32(f_w3) + _f32(f_b3)))
    h = (h @ _f32(f_wout)).reshape(L, 2, D_MODEL)
    deltas = jnp.linspace(HY_MIN_DECAY, HY_MAX_DECAY, D_MODEL, dtype=jnp.float32)
    h = h * jnp.exp(-t * jnp.abs(deltas))[:, None, :]
    kern = jnp.concatenate([h[:, 0], jnp.zeros((1, D_MODEL), jnp.float32), h[:0:-1, 1]], axis=0)
    return kern / jnp.sum(jnp.abs(kern), axis=0, keepdims=True)


def hyena_mix(u, p):
    (w_in, b_in, sc_w, sc_b, f_w1, f_b1, f_w2, f_b2, f_w3, f_b3, f_wout, freq,
     bias, w_out, b_out) = p
    L = u.shape[1]
    proj = short_conv(u @ w_in + b_in, sc_w, sc_b)
    x0, x1, v = jnp.split(proj, 3, axis=-1)
    kern = hyena_filters(L, f_w1, f_b1, f_w2, f_b2, f_w3, f_b3, f_wout, freq)
    z = _f32(x1 * v)
    y = jnp.fft.irfft(jnp.fft.rfft(z, n=2 * L, axis=1) * jnp.fft.rfft(kern, n=2 * L, axis=0)[None],
                      n=2 * L, axis=1)[:, :L]
    y = (y + z * _f32(bias)).astype(u.dtype)
    return (x0 * y) @ w_out + b_out


def headwise(a, w):
    B, L, _ = a.shape
    out = jnp.einsum('blnc,nce->blne', a.reshape(B, L, ML_N_BLOCKS, ML_QKV_BLOCK), w)
    return out.reshape(B, L, ML_INNER)


def to_heads(a):
    B, L, _ = a.shape
    return _f32(a).reshape(B, L, ML_HEADS, ML_HEAD_DIM).transpose(0, 2, 1, 3)


def mlstm_chunkwise(q, k, v, ig, lf, state):
    B, H, L, dh = q.shape
    nc = L // ML_CHUNK

    def chunks(a):
        return jnp.moveaxis(a.reshape(B, H, nc, ML_CHUNK, *a.shape[3:]), 2, 0)

    lower = jnp.tril(jnp.ones((ML_CHUNK, ML_CHUNK), dtype=bool))

    def step(carry, xs):
        C, n, m = carry
        qc, kc, vc, igc, lfc = xs
        b = jnp.cumsum(lfc, axis=-1)
        a_inter = b + m[..., None]
        d = jnp.where(lower, b[..., :, None] - b[..., None, :] + igc[..., None, :], -jnp.inf)
        m_row = jnp.maximum(a_inter, jnp.max(d, axis=-1))
        w_inter = jnp.exp(a_inter - m_row)
        s = jnp.einsum('bhid,bhjd->bhij', qc, kc) * jnp.exp(d - m_row[..., None])
        num = (w_inter[..., None] * jnp.einsum('bhed,bhid->bhie', C, qc)
               + jnp.einsum('bhij,bhje->bhie', s, vc))
        den = w_inter * jnp.einsum('bhd,bhid->bhi', n, qc) + jnp.sum(s, axis=-1)
        h = num / jnp.maximum(jnp.abs(den), jnp.exp(-m_row))[..., None]
        b_last = b[..., -1]
        e = b_last[..., None] - b + igc
        m_new = jnp.maximum(b_last + m, jnp.max(e, axis=-1))
        g_old = jnp.exp(b_last + m - m_new)
        w_new = jnp.exp(e - m_new[..., None])
        C_new = g_old[..., None, None] * C + jnp.einsum('bhje,bhjd->bhed', vc * w_new[..., None], kc)
        n_new = g_old[..., None] * n + jnp.einsum('bhj,bhjd->bhd', w_new, kc)
        return (C_new, n_new, m_new), h

    state, h = lax.scan(step, state, (chunks(q), chunks(k), chunks(v), chunks(ig), chunks(lf)))
    return jnp.moveaxis(h, 0, 2).reshape(B, H, L, dh), state


def mlstm_final_state(k, v, ig, lf):
    b = jnp.cumsum(lf, axis=-1)
    b_last = b[..., -1]
    e = b_last[..., None] - b + ig
    m = jnp.maximum(b_last, jnp.max(e, axis=-1))
    w = jnp.exp(e - m[..., None])
    return (jnp.einsum('bhje,bhjd->bhed', v * w[..., None], k),
            jnp.einsum('bhj,bhjd->bhd', w, k), m)


def mlstm_prep(u, w_in, conv_w, conv_b, wq, wk, wv, w_gate, b_gate):
    B, L, _ = u.shape
    xm, z = jnp.split(u @ w_in, 2, axis=-1)
    xc = jax.nn.silu(short_conv(xm, conv_w, conv_b))
    q = headwise(xc, wq)
    k = headwise(xc, wk)
    v = headwise(xm, wv)
    gates = (q @ w_gate[:ML_INNER] + k @ w_gate[ML_INNER:2 * ML_INNER]
             + v @ w_gate[2 * ML_INNER:] + b_gate)
    gates = jnp.transpose(_f32(gates).reshape(B, L, 2, 2, ML_HEADS), (2, 3, 0, 4, 1))
    ig = gates[:, 0]
    lf = jax.nn.log_sigmoid(gates[:, 1])
    return to_heads(q), to_heads(k) * (ML_HEAD_DIM ** -0.5), to_heads(v), ig, lf, xc, z


def mlstm_out(h, xc, z, norm_w, skip, w_down):
    B, H, L, dh = h.shape
    mu = jnp.mean(h, axis=-1, keepdims=True)
    var = jnp.mean(jnp.square(h - mu), axis=-1, keepdims=True)
    hn = ((h - mu) * lax.rsqrt(var + ML_NORM_EPS)).transpose(0, 2, 1, 3).reshape(B, L, H * dh)
    hs = (hn * _f32(norm_w)).astype(xc.dtype) + skip * xc
    return (hs * jax.nn.silu(z)) @ w_down


def mlstm_mix(u_lat, u_ctx, p, need_ctx):
    w_in, conv_w, conv_b, wq, wk, wv, w_gate, b_gate, norm_w, skip, w_down = p

    def prep(u):
        return mlstm_prep(u, w_in, conv_w, conv_b, wq, wk, wv, w_gate, b_gate)

    def flip(a):
        return jnp.flip(a, axis=2)

    q, k, v, ig, lf, xc, z = prep(u_lat)
    qc, kc, vc, igc, lfc, xcc, zc = prep(u_ctx)
    if need_ctx:
        B = u_ctx.shape[0]
        zero = (jnp.zeros((B, ML_HEADS, ML_HEAD_DIM, ML_HEAD_DIM), jnp.float32),
                jnp.zeros((B, ML_HEADS, ML_HEAD_DIM), jnp.float32),
                jnp.zeros((B, ML_HEADS), jnp.float32))
        hcf, st_f = mlstm_chunkwise(qc, kc, vc, igc[0], lfc[0], zero)
        hcb, st_b = mlstm_chunkwise(flip(qc), flip(kc), flip(vc), flip(igc[1]), flip(lfc[1]), zero)
        y_ctx = mlstm_out(hcf + flip(hcb), xcc, zc, norm_w, skip, w_down)
    else:
        st_f = mlstm_final_state(kc, vc, igc[0], lfc[0])
        st_b = mlstm_final_state(flip(kc), flip(vc), flip(igc[1]), flip(lfc[1]))
        y_ctx = None
    hf, _ = mlstm_chunkwise(q, k, v, ig[0], lf[0], st_f)
    hb, _ = mlstm_chunkwise(flip(q), flip(k), flip(v), flip(ig[1]), flip(lf[1]), st_b)
    y_lat = mlstm_out(hf + flip(hb), xc, z, norm_w, skip, w_down)
    return y_lat, y_ctx


def setup_inputs(seed: int = 0) -> dict:
    key = jax.random.key(seed)
    keys = iter(jax.random.split(key, 48))

    def nrm(shape, scale):
        return jax.random.normal(next(keys), shape, dtype=jnp.float32) * scale

    D, NH, NM = D_MODEL, N_HY_LAYERS, N_ML_LAYERS
    fb = jnp.linspace(3.0, 6.0, ML_HEADS, dtype=jnp.float32)
    ml_b_gate = jnp.stack([nrm((NM, 2, ML_HEADS), 0.1), fb + nrm((NM, 2, ML_HEADS), 0.1)],
                          axis=2).reshape(NM, 4 * ML_HEADS)
    return {
        'x': nrm((BATCH, SEQ, D), 1.0),
        'c': nrm((BATCH, D), 1.0),
        'ctx': nrm((BATCH, CTX_LEN, D), 1.0),
        'c_ctx': nrm((D,), 1.0),
        'mod_w': nrm((DEPTH, D, 6 * D), 0.5 * D ** -0.5),
        'mod_b': nrm((DEPTH, 6 * D), 0.02),
        'norm_g': 1.0 + nrm((DEPTH, 2, D), 0.02),
        'final_g': 1.0 + nrm((D,), 0.02),
        'hy_w_in': nrm((NH, D, 3 * D), D ** -0.5),
        'hy_b_in': nrm((NH, 3 * D), 0.02),
        'hy_sc_w': nrm((NH, SHORT_W, 3 * D), SHORT_W ** -0.5),
        'hy_sc_b': nrm((NH, 3 * D), 0.02),
        'hy_f_w1': nrm((NH, HY_EMB_DIM, HY_FILTER_WIDTH), HY_EMB_DIM ** -0.5),
        'hy_f_b1': nrm((NH, HY_FILTER_WIDTH), 0.1),
        'hy_f_w2': nrm((NH, HY_FILTER_WIDTH, HY_FILTER_WIDTH), HY_FILTER_WIDTH ** -0.5),
        'hy_f_b2': nrm((NH, HY_FILTER_WIDTH), 0.1),
        'hy_f_w3': nrm((NH, HY_FILTER_WIDTH, HY_FILTER_WIDTH), HY_FILTER_WIDTH ** -0.5),
        'hy_f_b3': nrm((NH, HY_FILTER_WIDTH), 0.1),
        'hy_f_wout': nrm((NH, HY_FILTER_WIDTH, 2 * D), HY_FILTER_WIDTH ** -0.5),
        'hy_freq': 1.0 + nrm((NH, HY_FILTER_WIDTH), 0.1),
        'hy_bias': nrm((NH, D), 0.2),
        'hy_w_out': nrm((NH, D, D), D ** -0.5),
        'hy_b_out': nrm((NH, D), 0.02),
        'ml_w_in': nrm((NM, D, 2 * ML_INNER), D ** -0.5),
        'ml_conv_w': nrm((NM, SHORT_W, ML_INNER), SHORT_W ** -0.5),
        'ml_conv_b': nrm((NM, ML_INNER), 0.02),
        'ml_wq': nrm((NM, ML_N_BLOCKS, ML_QKV_BLOCK, ML_QKV_BLOCK), ML_QKV_BLOCK ** -0.5),
        'ml_wk': nrm((NM, ML_N_BLOCKS, ML_QKV_BLOCK, ML_QKV_BLOCK), ML_QKV_BLOCK ** -0.5),
        'ml_wv': nrm((NM, ML_N_BLOCKS, ML_QKV_BLOCK, ML_QKV_BLOCK), ML_QKV_BLOCK ** -0.5),
        'ml_w_gate': nrm((NM, 3 * ML_INNER, 4 * ML_HEADS), 0.1 * (3 * ML_INNER) ** -0.5),
        'ml_b_gate': ml_b_gate,
        'ml_norm_w': 1.0 + nrm((NM, ML_INNER), 0.02),
        'ml_skip': 1.0 + nrm((NM, ML_INNER), 0.02),
        'ml_w_down': nrm((NM, ML_INNER, D), ML_INNER ** -0.5),
        'ffn_w_up': nrm((DEPTH, D, 2 * FFN_HIDDEN), D ** -0.5),
        'ffn_conv_w': nrm((DEPTH, 3, 3, FFN_HIDDEN), 1.0 / 3.0),
        'ffn_conv_b': nrm((DEPTH, FFN_HIDDEN), 0.02),
        'ffn_w_down': nrm((DEPTH, FFN_HIDDEN, D), FFN_HIDDEN ** -0.5),
    }


def reference(x, c, ctx, c_ctx, mod_w, mod_b, norm_g, final_g,
              hy_w_in, hy_b_in, hy_sc_w, hy_sc_b, hy_f_w1, hy_f_b1, hy_f_w2, hy_f_b2,
              hy_f_w3, hy_f_b3, hy_f_wout, hy_freq, hy_bias, hy_w_out, hy_b_out,
              ml_w_in, ml_conv_w, ml_conv_b, ml_wq, ml_wk, ml_wv, ml_w_gate, ml_b_gate,
              ml_norm_w, ml_skip, ml_w_down,
              ffn_w_up, ffn_conv_w, ffn_conv_b, ffn_w_down):
    rows = x.shape[1] // GRID_W
    ctx_len = ctx.shape[1]
    hy_params = (hy_w_in, hy_b_in, hy_sc_w, hy_sc_b, hy_f_w1, hy_f_b1, hy_f_w2, hy_f_b2,
                 hy_f_w3, hy_f_b3, hy_f_wout, hy_freq, hy_bias, hy_w_out, hy_b_out)
    ml_params = (ml_w_in, ml_conv_w, ml_conv_b, ml_wq, ml_wk, ml_wv, ml_w_gate, ml_b_gate,
                 ml_norm_w, ml_skip, ml_w_down)
    for i in range(DEPTH):
        last = i == DEPTH - 1
        sh1, sc1, g1, sh2, sc2, g2 = adaln_params(c, mod_w[i], mod_b[i])
        csh1, csc1, cg1, csh2, csc2, cg2 = adaln_params(c_ctx, mod_w[i], mod_b[i])
        h_lat = modulate(rmsnorm(x, norm_g[i, 0]), sh1, sc1)
        h_ctx = modulate(rmsnorm(ctx, norm_g[i, 0]), csh1, csc1)
        j = i // N_MIXERS
        if i % N_MIXERS == 0:
            p = tuple(a[j] for a in hy_params)
            y_lat = hyena_mix(h_lat, p)
            y_ctx = None if last else hyena_mix(h_ctx, p)
        else:
            p = tuple(a[j] for a in ml_params)
            y_lat, y_ctx = mlstm_mix(h_lat, h_ctx, p, not last)
        x = x + g1 * y_lat
        ffn_p = (ffn_w_up[i], ffn_conv_w[i], ffn_conv_b[i], ffn_w_down[i])
        x = x + g2 * conv_ffn(modulate(rmsnorm(x, norm_g[i, 1]), sh2, sc2), *ffn_p, rows, GRID_W)
        if not last:
            ctx = ctx + cg1 * y_ctx
            ctx = ctx + cg2 * conv_ffn(modulate(rmsnorm(ctx, norm_g[i, 1]), csh2, csc2),
                                       *ffn_p, 1, ctx_len)
    return rmsnorm(x, final_g)
```

```python
import functools
import math

import jax
import jax.numpy as jnp
from jax import lax
from jax.experimental import pallas as pl
from jax.experimental.pallas import tpu as pltpu

F32 = jnp.float32
BF16 = jnp.bfloat16
HIGHEST = lax.Precision.HIGHEST

EPS = 1e-6
ML_NORM_EPS = 1e-5
GRID_W = 64
ML_HEADS = 4
ML_CHUNK = 128
ML_QKV_BLOCK = 4
HY_EMB_BANDS = 16
HY_MAX_DECAY = math.log(1e-2) / 0.3
HY_MIN_DECAY = math.log(1e-2) / 1.5

V7X_LANES = 128
V7X_SUBLANES = 8
V7X_MXU_DIM = 256
V7X_VMEM_BYTES = 64 * 1024 * 1024
VMEM_LIMIT = V7X_VMEM_BYTES - 8 * 1024 * 1024


def _cparams(*sem):
    return pltpu.CompilerParams(dimension_semantics=("arbitrary",) * len(sem), vmem_limit_bytes=VMEM_LIMIT)


def _dot(a, b, precision=None):
    return jnp.dot(a, b, preferred_element_type=F32, precision=precision)


def _row_tile(n, target):
    t = min(n, target)
    assert n % t == 0, (n, t)
    return t


def _adaln_kernel(c_ref, w_ref, b_ref, o_ref):
    c = c_ref[...]
    s = c * jax.nn.sigmoid(c)
    o_ref[...] = _dot(s, w_ref[...], HIGHEST) + b_ref[...]


def adaln(cond8, w, b):
    r, d = cond8.shape
    n = w.shape[1]
    tn = _row_tile(n, 1536)
    return pl.pallas_call(
        _adaln_kernel,
        grid=(n // tn,),
        in_specs=[pl.BlockSpec((r, d), lambda j: (0, 0)),
                  pl.BlockSpec((d, tn), lambda j: (0, j)),
                  pl.BlockSpec((1, tn), lambda j: (0, j))],
        out_specs=pl.BlockSpec((r, tn), lambda j: (0, j)),
        out_shape=jax.ShapeDtypeStruct((r, n), F32),
        compiler_params=_cparams("arbitrary"),
        name="adaln",
    )(cond8, w, b.reshape(1, n))


def _nmm_kernel(x_ref, g_ref, sh_ref, sc_ref, w_ref, b_ref, o_ref):
    x = x_ref[0]
    y = x * lax.rsqrt(jnp.mean(x * x, axis=-1, keepdims=True) + EPS)
    u = (y * g_ref[...]) * (1.0 + sc_ref[0]) + sh_ref[0]
    acc = _dot(u.astype(BF16), w_ref[...])
    o_ref[0] = (acc + b_ref[...]).astype(o_ref.dtype)


def norm_mod_matmul(x, g, shift, scale, w_bf16, bias, tm):
    B, L, D = x.shape
    n = w_bf16.shape[1]
    tm = _row_tile(L, tm)
    return pl.pallas_call(
        _nmm_kernel,
        grid=(B, L // tm),
        in_specs=[pl.BlockSpec((1, tm, D), lambda b, i: (b, i, 0)),
                  pl.BlockSpec((1, D), lambda b, i: (0, 0)),
                  pl.BlockSpec((1, 1, D), lambda b, i: (b, 0, 0)),
                  pl.BlockSpec((1, 1, D), lambda b, i: (b, 0, 0)),
                  pl.BlockSpec((D, n), lambda b, i: (0, 0)),
                  pl.BlockSpec((1, n), lambda b, i: (0, 0))],
        out_specs=pl.BlockSpec((1, tm, n), lambda b, i: (b, i, 0)),
        out_shape=jax.ShapeDtypeStruct((B, L, n), F32),
        compiler_params=_cparams("parallel", "parallel"),
        name="norm_mod_matmul",
    )(x, g.reshape(1, D), shift, scale, w_bf16, bias.reshape(1, n))


def _rgm_kernel(a_ref, w_ref, b_ref, gate_ref, res_ref, fg_ref, o_ref, *, final_norm):
    acc = _dot(a_ref[0], w_ref[...]) + b_ref[...]
    x = res_ref[0] + gate_ref[0] * acc
    if final_norm:
        x = (x * lax.rsqrt(jnp.mean(x * x, axis=-1, keepdims=True) + EPS)) * fg_ref[...]
    o_ref[0] = x


def res_gate_matmul(a_bf16, w_bf16, bias, gate, res, final_g=None, tm=512):
    B, L, K = a_bf16.shape
    D = w_bf16.shape[1]
    tm = _row_tile(L, tm)
    final_norm = final_g is not None
    fg = final_g if final_norm else jnp.ones((D,), F32)
    return pl.pallas_call(
        functools.partial(_rgm_kernel, final_norm=final_norm),
        grid=(B, L // tm),
        in_specs=[pl.BlockSpec((1, tm, K), lambda b, i: (b, i, 0)),
                  pl.BlockSpec((K, D), lambda b, i: (0, 0)),
                  pl.BlockSpec((1, D), lambda b, i: (0, 0)),
                  pl.BlockSpec((1, 1, D), lambda b, i: (b, 0, 0)),
                  pl.BlockSpec((1, tm, D), lambda b, i: (b, i, 0)),
                  pl.BlockSpec((1, D), lambda b, i: (0, 0))],
        out_specs=pl.BlockSpec((1, tm, D), lambda b, i: (b, i, 0)),
        out_shape=jax.ShapeDtypeStruct((B, L, D), F32),
        compiler_params=_cparams("parallel", "parallel"),
        name="res_gate_matmul",
    )(a_bf16, w_bf16, bias.reshape(1, D), gate, res, fg.reshape(1, D))


def _halo_specs(tm, L, C, cblock=0):
    r = V7X_SUBLANES
    nb = L // r
    prev = pl.BlockSpec((1, r, C), lambda b, i: (b, jnp.maximum(i * (tm // r) - 1, 0), cblock))
    nxt = pl.BlockSpec((1, r, C), lambda b, i: (b, jnp.minimum((i + 1) * (tm // r), nb - 1), cblock))
    return prev, nxt


def _conv3_rows(x, prev_ref, next_ref, w_ref, b_ref):
    tm = x.shape[0]
    i = pl.program_id(1)
    last = pl.num_programs(1) - 1
    prev_row = jnp.where(i > 0, prev_ref[0, V7X_SUBLANES - 1:V7X_SUBLANES, :], 0.0)
    next_row = jnp.where(i < last, next_ref[0, 0:1, :], 0.0)
    row = lax.broadcasted_iota(jnp.int32, x.shape, 0)
    xm1 = jnp.where(row == 0, prev_row, pltpu.roll(x, 1, 0))
    xp1 = jnp.where(row == tm - 1, next_row, pltpu.roll(x, tm - 1, 0))
    return w_ref[0:1, :] * xm1 + w_ref[1:2, :] * x + w_ref[2:3, :] * xp1 + b_ref[...]


def _hy_gate_kernel(p_ref, prev_ref, next_ref, w_ref, b_ref, x0_ref, z_ref):
    D = x0_ref.shape[-1]
    conv = _conv3_rows(p_ref[0], prev_ref, next_ref, w_ref, b_ref)
    x0_ref[0] = conv[:, :D]
    z_ref[0] = conv[:, D:2 * D] * conv[:, 2 * D:]


def hy_gate(proj, sc_w, sc_b, tm=512):
    B, L, C3 = proj.shape
    D = C3 // 3
    tm = _row_tile(L, tm)
    prev, nxt = _halo_specs(tm, L, C3)
    out = jax.ShapeDtypeStruct((B, L, D), F32)
    ospec = pl.BlockSpec((1, tm, D), lambda b, i: (b, i, 0))
    return pl.pallas_call(
        _hy_gate_kernel,
        grid=(B, L // tm),
        in_specs=[pl.BlockSpec((1, tm, C3), lambda b, i: (b, i, 0)), prev, nxt,
                  pl.BlockSpec((3, C3), lambda b, i: (0, 0)),
                  pl.BlockSpec((1, C3), lambda b, i: (0, 0))],
        out_specs=(ospec, ospec),
        out_shape=(out, out),
        compiler_params=_cparams("parallel", "arbitrary"),
        name="hy_gate",
    )(proj, proj, proj, sc_w, sc_b.reshape(1, C3))


def _filter_kernel(w1_ref, b1_ref, w2_ref, b2_ref, w3_ref, b3_ref, wo_ref, fr_ref, band_ref,
                   h_ref, s_ref, *, L, D):
    i = pl.program_id(0)
    tl = h_ref.shape[0]
    rowi = lax.broadcasted_iota(jnp.int32, (tl, 1), 0) + i * tl
    row = rowi.astype(F32)
    t = row / (L - 1.0)
    w = (2.0 * math.pi) * row / L
    col = lax.broadcasted_iota(jnp.int32, (tl, V7X_LANES), 1)
    ang = w * band_ref[...]
    pos = jnp.where(col == 0, t,
                    jnp.where(col <= HY_EMB_BANDS, jnp.cos(ang),
                              jnp.where(col <= 2 * HY_EMB_BANDS, -jnp.sin(ang), 0.0)))
    fr = fr_ref[...]
    h = jnp.sin(fr * (_dot(pos, w1_ref[...], HIGHEST) + b1_ref[...]))
    h = jnp.sin(fr * (_dot(h, w2_ref[...], HIGHEST) + b2_ref[...]))
    h = jnp.sin(fr * (_dot(h, w3_ref[...], HIGHEST) + b3_ref[...]))
    h = _dot(h, wo_ref[...], HIGHEST)
    dcol = lax.broadcasted_iota(jnp.int32, (1, 2 * D), 1)
    chan = jnp.where(dcol >= D, dcol - D, dcol).astype(F32)
    deltas = HY_MIN_DECAY + chan * ((HY_MAX_DECAY - HY_MIN_DECAY) / (D - 1.0))
    h = h * jnp.exp(-t * jnp.abs(deltas))
    h = jnp.where((rowi == 0) & (dcol >= D), 0.0, h)
    h_ref[...] = h
    part = jnp.sum(jnp.abs(h), axis=0, keepdims=True)

    @pl.when(i == 0)
    def _():
        s_ref[...] = part

    @pl.when(i > 0)
    def _():
        s_ref[...] += part


def hyena_filter(L, f_w1, f_b1, f_w2, f_b2, f_w3, f_b3, f_wout, freq):
    W = f_w2.shape[0]
    D2 = f_wout.shape[1]
    P = V7X_LANES
    pad2 = lambda a: jnp.pad(a, ((0, P - a.shape[0]), (0, P - a.shape[1])))
    padv = lambda a: jnp.pad(a.reshape(1, -1), ((0, 0), (0, P - a.shape[0])))
    w1 = pad2(f_w1)
    w2 = pad2(f_w2)
    w3 = pad2(f_w3)
    wo = jnp.pad(f_wout, ((0, P - W), (0, 0)))
    bands = jnp.linspace(1e-4, HY_EMB_BANDS - 1, HY_EMB_BANDS, dtype=F32)
    band_row = jnp.concatenate([jnp.zeros((1,), F32), bands, bands,
                                jnp.zeros((P - 1 - 2 * HY_EMB_BANDS,), F32)]).reshape(1, P)
    tl = _row_tile(L, 512)
    full = lambda shape: pl.BlockSpec(shape, lambda i: (0, 0))
    return pl.pallas_call(
        functools.partial(_filter_kernel, L=L, D=D2 // 2),
        grid=(L // tl,),
        in_specs=[full((P, P)), full((1, P)), full((P, P)), full((1, P)), full((P, P)), full((1, P)),
                  full((P, D2)), full((1, P)), full((1, P))],
        out_specs=(pl.BlockSpec((tl, D2), lambda i: (i, 0)), pl.BlockSpec((1, D2), lambda i: (0, 0))),
        out_shape=(jax.ShapeDtypeStruct((L, D2), F32), jax.ShapeDtypeStruct((1, D2), F32)),
        compiler_params=_cparams("arbitrary"),
        name="hyena_filter",
    )(w1, padv(f_b1), w2, padv(f_b2), w3, padv(f_b3), wo, padv(freq), band_row)


def _cos_sin(p, n):
    ang = (2.0 * math.pi / n) * p.astype(F32)
    return jnp.cos(ang), jnp.sin(ang)


def _dft_tables(N1, N2):
    N = N1 * N2
    ar = lambda n: jnp.arange(n, dtype=jnp.int32)
    c, s = _cos_sin((ar(N1)[:, None] * ar(N1 // 2)[None, :]) % N1, N1)
    g1 = jnp.stack([c, -s], axis=1).reshape(2 * N1, N1 // 2)
    k1 = ar(N1)[:, None, None]
    a = ar(N2)[None, :, None]
    b = ar(N2)[None, None, :]
    c, s = _cos_sin((b * (k1 + N1 * a)) % N, N)
    g2 = jnp.concatenate([jnp.concatenate([c, s], axis=2), jnp.concatenate([-s, c], axis=2)], axis=1)
    c, s = _cos_sin((a * (k1 + N1 * b)) % N, N)
    g2i = jnp.concatenate([jnp.concatenate([c, -s], axis=2), jnp.concatenate([s, c], axis=2)], axis=1)
    c, s = _cos_sin((ar(N1 // 2)[:, None] * ar(N1)[None, :]) % N1, N1)
    g4 = jnp.stack([c, -s], axis=2).reshape(N1 // 2, 2 * N1) * (1.0 / N)
    return g1, g2, g2i, g4


def _lmul_kernel(g_ref, x_ref, o_ref):
    o_ref[0] = _dot(g_ref[...], x_ref[0], HIGHEST)


def left_matmul(g, x, tl):
    B, K, cols = x.shape
    M = g.shape[0]
    tl = _row_tile(cols, tl)
    return pl.pallas_call(
        _lmul_kernel,
        grid=(B, cols // tl),
        in_specs=[pl.BlockSpec((M, K), lambda b, j: (0, 0)),
                  pl.BlockSpec((1, K, tl), lambda b, j: (b, 0, j))],
        out_specs=pl.BlockSpec((1, M, tl), lambda b, j: (b, 0, j)),
        out_shape=jax.ShapeDtypeStruct((B, M, cols), F32),
        compiler_params=_cparams("parallel", "parallel"),
        name="dft_stage1",
    )(g, x)


def _combine_spectrum(X, nrm_ref, D):
    n = X.shape[0] // 2
    nrm = nrm_ref[:, :D] + nrm_ref[:, D:]
    hre = (X[:n, :D] + X[:n, D:]) / nrm
    him = (X[n:, :D] - X[n:, D:]) / nrm
    return hre, him


def _filter_spec_kernel(g_ref, a_ref, nrm_ref, o_ref):
    _, _, n2, D2 = a_ref.shape
    slab = a_ref[0].reshape(2 * n2, D2)
    X = _dot(g_ref[0], slab, HIGHEST)
    hre, him = _combine_spectrum(X, nrm_ref, D2 // 2)
    o_ref[0, 0] = hre
    o_ref[0, 1] = him


def filter_spectrum(g2, a, nrm):
    N1, _, N2, D2 = a.shape
    D = D2 // 2
    return pl.pallas_call(
        _filter_spec_kernel,
        grid=(N1,),
        in_specs=[pl.BlockSpec((1, 2 * N2, 2 * N2), lambda k: (k, 0, 0)),
                  pl.BlockSpec((1, 2, N2, D2), lambda k: (k, 0, 0, 0)),
                  pl.BlockSpec((1, D2), lambda k: (0, 0))],
        out_specs=pl.BlockSpec((1, 2, N2, D), lambda k: (k, 0, 0, 0)),
        out_shape=jax.ShapeDtypeStruct((N1, 2, N2, D), F32),
        compiler_params=_cparams("parallel"),
        name="filter_spectrum",
    )(g2, a, nrm)


def _cmul(xr, xi, hr, hi):
    return xr * hr - xi * hi, xr * hi + xi * hr


def _spec_mul_kernel(g_ref, gi_ref, h_ref, a_ref, o_ref):
    _, _, _, n2, C = a_ref.shape
    slab = a_ref[0, 0].reshape(2 * n2, C)
    X = _dot(g_ref[0], slab, HIGHEST)
    pr, pi = _cmul(X[:n2], X[n2:], h_ref[0, 0], h_ref[0, 1])
    Y = _dot(gi_ref[0], jnp.concatenate([pr, pi], axis=0), HIGHEST)
    o_ref[0, 0] = Y.reshape(2, n2, C)


def spectrum_multiply(g2, g2i, H, a):
    B, N1, _, N2, C = a.shape
    blk = pl.BlockSpec((1, 1, 2, N2, C), lambda k, b: (b, k, 0, 0, 0))
    gsp = pl.BlockSpec((1, 2 * N2, 2 * N2), lambda k, b: (k, 0, 0))
    return pl.pallas_call(
        _spec_mul_kernel,
        grid=(N1, B),
        in_specs=[gsp, gsp, pl.BlockSpec((1, 2, N2, C), lambda k, b: (k, 0, 0, 0)), blk],
        out_specs=blk,
        out_shape=jax.ShapeDtypeStruct(a.shape, F32),
        compiler_params=_cparams("parallel", "parallel"),
        name="spectrum_multiply",
    )(g2, g2i, H, a)


def _idft_out_kernel(g_ref, y_ref, x0_ref, z_ref, bias_ref, o_ref):
    y = _dot(g_ref[...], y_ref[0], HIGHEST)
    o_ref[0] = (x0_ref[0] * (y + z_ref[0] * bias_ref[...])).astype(o_ref.dtype)


def idft_gate_out(g4, yv, x0, z, bias_tiled, tl):
    B, K, cols = yv.shape
    M = g4.shape[0]
    tl = _row_tile(cols, tl)
    dat = pl.BlockSpec((1, M, tl), lambda b, j: (b, 0, j))
    return pl.pallas_call(
        _idft_out_kernel,
        grid=(B, cols // tl),
        in_specs=[pl.BlockSpec((M, K), lambda b, j: (0, 0)),
                  pl.BlockSpec((1, K, tl), lambda b, j: (b, 0, j)),
                  dat, dat, pl.BlockSpec((1, tl), lambda b, j: (0, 0))],
        out_specs=dat,
        out_shape=jax.ShapeDtypeStruct((B, M, cols), BF16),
        compiler_params=_cparams("parallel", "parallel"),
        name="idft_gate_out",
    )(g4, yv, x0, z, bias_tiled)


def long_conv_two_stage(x0, z, hfb, nrm, bias):
    B, L, C = z.shape
    N2 = V7X_LANES
    N1 = 2 * L // N2
    g1, g2, g2i, g4 = _dft_tables(N1, N2)
    tl = 8 * C
    a_f = left_matmul(g1, hfb.reshape(1, N1 // 2, N2 * 2 * C), tl)
    H = filter_spectrum(g2, a_f.reshape(N1, 2, N2, 2 * C), nrm)
    a = left_matmul(g1, z.reshape(B, N1 // 2, N2 * C), tl)
    yv = spectrum_multiply(g2, g2i, H, a.reshape(B, N1, 2, N2, C))
    out = idft_gate_out(g4, yv.reshape(B, 2 * N1, N2 * C), x0.reshape(B, N1 // 2, N2 * C),
                        z.reshape(B, N1 // 2, N2 * C), jnp.tile(bias.reshape(1, C), (1, tl // C)), tl)
    return out.reshape(B, L, C)


def _dense_tables(L):
    N = 2 * L
    ar = lambda n: jnp.arange(n, dtype=jnp.int32)
    c, s = _cos_sin((ar(N)[:, None] * ar(L)[None, :]) % N, N)
    g = jnp.concatenate([c, -s], axis=0)
    c, s = _cos_sin((ar(L)[:, None] * ar(N)[None, :]) % N, N)
    gi = jnp.concatenate([c, -s], axis=1) * (1.0 / N)
    return g, gi


def _dense_spec_kernel(g_ref, hfb_ref, nrm_ref, o_ref):
    X = _dot(g_ref[...], hfb_ref[...], HIGHEST)
    hre, him = _combine_spectrum(X, nrm_ref, o_ref.shape[-1])
    o_ref[0] = hre
    o_ref[1] = him


def _dense_conv_kernel(g_ref, gi_ref, h_ref, z_ref, x0_ref, bias_ref, o_ref):
    z = z_ref[0]
    X = _dot(g_ref[...], z, HIGHEST)
    n = X.shape[0] // 2
    pr, pi = _cmul(X[:n], X[n:], h_ref[0], h_ref[1])
    y = _dot(gi_ref[...], jnp.concatenate([pr, pi], axis=0), HIGHEST)
    o_ref[0] = (x0_ref[0] * (y + z * bias_ref[...])).astype(o_ref.dtype)


def long_conv_dense(x0, z, hfb, nrm, bias):
    B, L, C = z.shape
    N = 2 * L
    g, gi = _dense_tables(L)
    full2 = lambda shape: pl.BlockSpec(shape, lambda *_: (0,) * len(shape))
    H = pl.pallas_call(
        _dense_spec_kernel,
        grid=(1,),
        in_specs=[full2((2 * N, L)), full2((L, 2 * C)), full2((1, 2 * C))],
        out_specs=full2((2, N, C)),
        out_shape=jax.ShapeDtypeStruct((2, N, C), F32),
        compiler_params=_cparams("arbitrary"),
        name="dense_filter_spectrum",
    )(g, hfb, nrm)
    dat = pl.BlockSpec((1, L, C), lambda b: (b, 0, 0))
    return pl.pallas_call(
        _dense_conv_kernel,
        grid=(B,),
        in_specs=[full2((2 * N, L)), full2((L, 2 * N)), full2((2, N, C)), dat, dat, full2((1, C))],
        out_specs=dat,
        out_shape=jax.ShapeDtypeStruct((B, L, C), BF16),
        compiler_params=_cparams("parallel"),
        name="dense_long_conv",
    )(g, gi, H, z, x0, bias.reshape(1, C))


def hyena_mix_pre(u_args, p, dense):
    x, g, shift, scale = u_args
    (w_in, b_in, sc_w, sc_b, f_w1, f_b1, f_w2, f_b2, f_w3, f_b3, f_wout, freq, bias) = p
    L = x.shape[1]
    proj = norm_mod_matmul(x, g, shift, scale, w_in.astype(BF16), b_in, tm=256)
    x0, z = hy_gate(proj, sc_w, sc_b)
    hfb, nrm = hyena_filter(L, f_w1, f_b1, f_w2, f_b2, f_w3, f_b3, f_wout, freq)
    conv = long_conv_dense if dense else long_conv_two_stage
    return conv(x0, z, hfb, nrm, bias)


def _ml_prep_kernel(xm_ref, prev_ref, next_ref, cw_ref, cb_ref, wq_ref, wk_ref, wv_ref, wg_ref, bg_ref,
                    q_ref, k_ref, v_ref, xc_ref, g_ref, *, k_scale):
    xm = xm_ref[0]
    inner = xm.shape[1]
    conv = _conv3_rows(xm, prev_ref, next_ref, cw_ref, cb_ref)
    xc = conv * jax.nn.sigmoid(conv)
    xc_ref[0] = xc
    xcb = xc.astype(BF16)
    xmb = xm.astype(BF16)
    gw = V7X_MXU_DIM
    gates = bg_ref[...]
    for j in range(inner // gw):
        sl = slice(j * gw, (j + 1) * gw)
        q = _dot(xcb[:, sl], wq_ref[j])
        k = _dot(xcb[:, sl], wk_ref[j])
        v = _dot(xmb[:, sl], wv_ref[j])
        q_ref[0, :, sl] = q.astype(BF16)
        k_ref[0, :, sl] = (k * k_scale).astype(BF16)
        v_ref[0, :, sl] = v.astype(BF16)
        gates = gates + _dot(q.astype(BF16), wg_ref[j * gw:(j + 1) * gw, :])
        gates = gates + _dot(k.astype(BF16), wg_ref[inner + j * gw:inner + (j + 1) * gw, :])
        gates = gates + _dot(v.astype(BF16), wg_ref[2 * inner + j * gw:2 * inner + (j + 1) * gw, :])
    g_ref[0] = gates


def _block_diag(w, group):
    nb, bs, _ = w.shape
    per = group // bs
    w = w.reshape(nb // per, per, bs, bs)
    eye = jnp.eye(per, dtype=w.dtype)
    dense = jnp.einsum("gpce,pr->gpcre", w, eye)
    return dense.reshape(nb // per, group, group).astype(BF16)


def ml_prep(xz, conv_w, conv_b, wq, wk, wv, w_gate, b_gate, tm=256):
    B, L, C2 = xz.shape
    inner = C2 // 2
    dh = inner // ML_HEADS
    tm = _row_tile(L, tm)
    gw = V7X_MXU_DIM
    ng = inner // gw
    P = V7X_LANES
    wg = jnp.pad(w_gate, ((0, 0), (0, P - w_gate.shape[1]))).astype(BF16)
    bg = jnp.pad(b_gate.reshape(1, -1), ((0, 0), (0, P - b_gate.shape[0])))
    prev, nxt = _halo_specs(tm, L, inner)
    c2 = lambda shape: pl.BlockSpec(shape, lambda b, i: (0,) * len(shape))
    row = lambda n: pl.BlockSpec((1, tm, n), lambda b, i: (b, i, 0))
    sd = lambda n, dt: jax.ShapeDtypeStruct((B, L, n), dt)
    return pl.pallas_call(
        functools.partial(_ml_prep_kernel, k_scale=dh ** -0.5),
        grid=(B, L // tm),
        in_specs=[row(inner), prev, nxt, c2((3, inner)), c2((1, inner)),
                  c2((ng, gw, gw)), c2((ng, gw, gw)), c2((ng, gw, gw)), c2((3 * inner, P)), c2((1, P))],
        out_specs=(row(inner), row(inner), row(inner), row(inner), row(P)),
        out_shape=(sd(inner, BF16), sd(inner, BF16), sd(inner, BF16), sd(inner, F32), sd(P, F32)),
        compiler_params=_cparams("parallel", "arbitrary"),
        name="ml_prep",
    )(xz, xz, xz, conv_w, conv_b.reshape(1, inner), _block_diag(wq, gw), _block_diag(wk, gw),
      _block_diag(wv, gw), wg, bg)


def _gates_layout(gates, nc):
    B, L, _ = gates.shape
    g = gates[..., :4 * ML_HEADS].reshape(B, nc, ML_CHUNK, 2, 2, ML_HEADS)
    return jnp.transpose(g, (3, 0, 5, 1, 4, 2))


def _mlstm_kernel(q_ref, k_ref, v_ref, g_ref, c0_ref, m0_ref, h_ref, cf_ref, mf_ref, c_scr, m_scr):
    d = pl.program_id(0)
    t = pl.program_id(3)
    nct = pl.num_programs(3)
    T = ML_CHUNK
    dh = q_ref.shape[-1]

    @pl.when(t == 0)
    def _():
        c_scr[...] = c0_ref[0, 0, 0]
        m_scr[...] = m0_ref[0, 0, 0]

    rev = d == 1
    gt = g_ref[0, 0, 0, 0]
    ig = gt[0:1, :]
    fg = gt[1:2, :]
    lf = jnp.minimum(fg, 0.0) - jnp.log1p(jnp.exp(-jnp.abs(fg)))
    lf8 = jnp.broadcast_to(lf, (V7X_SUBLANES, T))
    lane = lax.broadcasted_iota(jnp.int32, (V7X_SUBLANES, T), 1)
    csf = lf8
    csr = lf8
    s = 1
    while s < T:
        csf = csf + jnp.where(lane >= s, pltpu.roll(csf, s, 1), 0.0)
        csr = csr + jnp.where(lane < T - s, pltpu.roll(csr, T - s, 1), 0.0)
        s *= 2
    bcs = jnp.where(rev, csr, csf)[0:1, :]
    b_last = jnp.sum(lf, axis=1, keepdims=True)
    gr = ig - bcs
    bc = jnp.broadcast_to(bcs, (T, T)).T
    row = lax.broadcasted_iota(jnp.int32, (T, T), 0)
    col = lax.broadcasted_iota(jnp.int32, (T, T), 1)
    mask = (col - row) * (1 - 2 * d) <= 0
    dmat = jnp.where(mask, bc + gr, -jnp.inf)
    m_prev = m_scr[0:1, 0:1]
    a_inter = bc[:, 0:1] + m_prev
    m_row = jnp.maximum(a_inter, jnp.max(dmat, axis=1, keepdims=True))
    w_inter = jnp.exp(a_inter - m_row)
    pmat = jnp.exp(dmat - m_row)

    q = q_ref[0]
    v = v_ref[0]
    kT = k_ref[0].astype(F32).T.astype(BF16)
    smat = _dot(q, kT) * pmat
    r1 = _dot(q, c_scr[...].astype(BF16))
    ones_col = (lax.broadcasted_iota(jnp.int32, (T, V7X_LANES), 1) == 0).astype(F32)
    v_aug = jnp.concatenate([v.astype(F32), ones_col], axis=1)
    r2 = _dot(smat.astype(BF16), v_aug.astype(BF16))
    num = w_inter * r1[:, :dh] + r2[:, :dh]
    den = w_inter * r1[:, dh:dh + 1] + r2[:, dh:dh + 1]
    h_ref[0, 0] = num / jnp.maximum(jnp.abs(den), jnp.exp(-m_row))

    e_row = b_last + gr
    m_new = jnp.maximum(b_last + m_prev, jnp.max(e_row, axis=1, keepdims=True))
    g_old = jnp.exp(b_last + m_prev - m_new)
    w_row = jnp.exp(e_row - m_new)
    w_col = jnp.broadcast_to(w_row, (T, T)).T[:, 0:1]
    c_scr[...] = g_old * c_scr[...] + _dot(kT, (v_aug * w_col).astype(BF16))
    m_scr[...] = jnp.broadcast_to(m_new, m_scr.shape)

    @pl.when(t == nct - 1)
    def _():
        cf_ref[0, 0, 0] = c_scr[...]
        mf_ref[0, 0, 0] = m_scr[...]


def mlstm_scan(q, k, v, gates, c0, m0):
    B, L, inner = q.shape
    H = ML_HEADS
    dh = inner // H
    T = ML_CHUNK
    nc = L // T
    da = dh + V7X_LANES
    cidx = lambda d, t: t + d * (nc - 1 - 2 * t)
    qkv = pl.BlockSpec((1, T, dh), lambda d, b, h, t: (b, cidx(d, t), h))
    cst = pl.BlockSpec((1, 1, 1, dh, da), lambda d, b, h, t: (d, b, h, 0, 0))
    mst = pl.BlockSpec((1, 1, 1, V7X_SUBLANES, V7X_LANES), lambda d, b, h, t: (d, b, h, 0, 0))
    return pl.pallas_call(
        _mlstm_kernel,
        grid=(2, B, H, nc),
        in_specs=[qkv, qkv, qkv,
                  pl.BlockSpec((1, 1, 1, 1, 2, T), lambda d, b, h, t: (d, b, h, cidx(d, t), 0, 0)),
                  cst, mst],
        out_specs=(pl.BlockSpec((1, 1, T, dh), lambda d, b, h, t: (d, b, cidx(d, t), h)), cst, mst),
        out_shape=(jax.ShapeDtypeStruct((2, B, L, inner), F32),
                   jax.ShapeDtypeStruct(c0.shape, F32), jax.ShapeDtypeStruct(m0.shape, F32)),
        scratch_shapes=[pltpu.VMEM((dh, da), F32), pltpu.VMEM((V7X_SUBLANES, V7X_LANES), F32)],
        compiler_params=_cparams("parallel", "parallel", "parallel", "arbitrary"),
        name="mlstm_scan",
    )(q, k, v, gates, c0, m0)


def _ml_out_kernel(hf_ref, hb_ref, xc_ref, z_ref, nw_ref, sk_ref, o_ref):
    h = hf_ref[0, 0] + hb_ref[0, 0]
    dh = h.shape[1] // ML_HEADS
    z = z_ref[0]
    gate = z * jax.nn.sigmoid(z)
    for j in range(ML_HEADS):
        sl = slice(j * dh, (j + 1) * dh)
        seg = h[:, sl]
        mu = jnp.mean(seg, axis=-1, keepdims=True)
        cen = seg - mu
        var = jnp.mean(cen * cen, axis=-1, keepdims=True)
        hn = cen * lax.rsqrt(var + ML_NORM_EPS)
        hs = hn * nw_ref[:, sl] + sk_ref[:, sl] * xc_ref[0, :, sl]
        o_ref[0, :, sl] = (hs * gate[:, sl]).astype(o_ref.dtype)


def ml_out(h2, xc, xz, norm_w, skip, tm=256):
    _, B, L, inner = h2.shape
    tm = _row_tile(L, tm)
    vec = pl.BlockSpec((1, inner), lambda b, i: (0, 0))
    return pl.pallas_call(
        _ml_out_kernel,
        grid=(B, L // tm),
        in_specs=[pl.BlockSpec((1, 1, tm, inner), lambda b, i: (0, b, i, 0)),
                  pl.BlockSpec((1, 1, tm, inner), lambda b, i: (1, b, i, 0)),
                  pl.BlockSpec((1, tm, inner), lambda b, i: (b, i, 0)),
                  pl.BlockSpec((1, tm, inner), lambda b, i: (b, i, 1)), vec, vec],
        out_specs=pl.BlockSpec((1, tm, inner), lambda b, i: (b, i, 0)),
        out_shape=jax.ShapeDtypeStruct((B, L, inner), BF16),
        compiler_params=_cparams("parallel", "parallel"),
        name="ml_out",
    )(h2, h2, xc, xz, norm_w.reshape(1, inner), skip.reshape(1, inner))


def mlstm_mix_pre(lat_args, ctx_args, p):
    w_in, conv_w, conv_b, wq, wk, wv, w_gate, b_gate, norm_w, skip = p
    w_in_b = w_in.astype(BF16)
    zero_b = jnp.zeros((w_in.shape[1],), F32)

    def prep(args):
        x, g, shift, scale = args
        xz = norm_mod_matmul(x, g, shift, scale, w_in_b, zero_b, tm=256)
        q, k, v, xc, gates = ml_prep(xz, conv_w, conv_b, wq, wk, wv, w_gate, b_gate)
        return q, k, v, xc, xz, _gates_layout(gates, x.shape[1] // ML_CHUNK)

    qc, kc, vc, _, _, gc = prep(ctx_args)
    q, k, v, xc, xz, gl = prep(lat_args)
    B, _, inner = q.shape
    dh = inner // ML_HEADS
    c0 = jnp.zeros((2, B, ML_HEADS, dh, dh + V7X_LANES), F32)
    m0 = jnp.zeros((2, B, ML_HEADS, V7X_SUBLANES, V7X_LANES), F32)
    _, c1, m1 = mlstm_scan(qc, kc, vc, gc, c0, m0)
    h2, _, _ = mlstm_scan(q, k, v, gl, c1, m1)
    return ml_out(h2, xc, xz, norm_w, skip)


def _ffn_conv_kernel(a_ref, g_ref, prev_ref, next_ref, w_ref, b_ref, o_ref, *, cols, vertical):
    tm = g_ref.shape[1]
    i = pl.program_id(1)
    last = pl.num_programs(1) - 1
    g = g_ref[0]
    if vertical:
        top = jnp.where(i > 0, prev_ref[0], 0.0)
        bot = jnp.where(i < last, next_ref[0], 0.0)
        g = jnp.concatenate([top, g, bot], axis=0)
    R = g.shape[0]
    cpos = jnp.bitwise_and(lax.broadcasted_iota(jnp.int32, (R, 1), 0), cols - 1)
    left = jnp.where(cpos == 0, 0.0, pltpu.roll(g, 1, 0))
    right = jnp.where(cpos == cols - 1, 0.0, pltpu.roll(g, R - 1, 0))
    acc = b_ref[...]
    for dr in (range(3) if vertical else (1,)):
        off = dr * cols if vertical else 0
        sl = slice(off, off + tm)
        acc = acc + (w_ref[3 * dr:3 * dr + 1, :] * left[sl] + w_ref[3 * dr + 1:3 * dr + 2, :] * g[sl]
                     + w_ref[3 * dr + 2:3 * dr + 3, :] * right[sl])
    o_ref[0] = ((acc * jax.nn.sigmoid(acc)) * a_ref[0]).astype(o_ref.dtype)


def ffn_conv_act(ag, conv_w, conv_b, rows, cols, tm=1024):
    B, L, F2 = ag.shape
    F = F2 // 2
    assert cols & (cols - 1) == 0 and rows * cols == L
    vertical = rows > 1
    tm = _row_tile(L, tm) if vertical else L
    cb = V7X_MXU_DIM
    nf = F // cb
    assert F % cb == 0 and tm % cols == 0
    hb = cols if vertical else V7X_SUBLANES
    nhb = L // hb
    return pl.pallas_call(
        functools.partial(_ffn_conv_kernel, cols=cols, vertical=vertical),
        grid=(B, L // tm, nf),
        in_specs=[pl.BlockSpec((1, tm, cb), lambda b, i, j: (b, i, j)),
                  pl.BlockSpec((1, tm, cb), lambda b, i, j: (b, i, nf + j)),
                  pl.BlockSpec((1, hb, cb), lambda b, i, j: (b, jnp.maximum(i * (tm // hb) - 1, 0), nf + j)),
                  pl.BlockSpec((1, hb, cb), lambda b, i, j: (b, jnp.minimum((i + 1) * (tm // hb), nhb - 1), nf + j)),
                  pl.BlockSpec((9, cb), lambda b, i, j: (0, j)),
                  pl.BlockSpec((1, cb), lambda b, i, j: (0, j))],
        out_specs=pl.BlockSpec((1, tm, cb), lambda b, i, j: (b, i, j)),
        out_shape=jax.ShapeDtypeStruct((B, L, F), BF16),
        compiler_params=_cparams("parallel", "arbitrary", "parallel"),
        name="ffn_conv_act",
    )(ag, ag, ag, ag, conv_w.reshape(9, F), conv_b.reshape(1, F))


def conv_ffn_residual(x, g, shift, scale, gate, w_up_b, conv_w, conv_b, w_down_b, rows, cols, final_g=None):
    ag = norm_mod_matmul(x, g, shift, scale, w_up_b, jnp.zeros((w_up_b.shape[1],), F32), tm=256)
    act = ffn_conv_act(ag, conv_w, conv_b, rows, cols)
    return res_gate_matmul(act, w_down_b, jnp.zeros((x.shape[-1],), F32), gate, x, final_g)


def kernel(x, c, ctx, c_ctx, mod_w, mod_b, norm_g, final_g, hy_w_in, hy_b_in, hy_sc_w, hy_sc_b, hy_f_w1, hy_f_b1, hy_f_w2, hy_f_b2, hy_f_w3, hy_f_b3, hy_f_wout, hy_freq, hy_bias, hy_w_out, hy_b_out, ml_w_in, ml_conv_w, ml_conv_b, ml_wq, ml_wk, ml_wv, ml_w_gate, ml_b_gate, ml_norm_w, ml_skip, ml_w_down, ffn_w_up, ffn_conv_w, ffn_conv_b, ffn_w_down):
    B, L, D = x.shape
    ctx_len = ctx.shape[1]
    depth = mod_w.shape[0]
    n_mixers = 2
    rows = L // GRID_W
    hy_params = (hy_w_in, hy_b_in, hy_sc_w, hy_sc_b, hy_f_w1, hy_f_b1, hy_f_w2, hy_f_b2,
                 hy_f_w3, hy_f_b3, hy_f_wout, hy_freq, hy_bias)
    ml_params = (ml_w_in, ml_conv_w, ml_conv_b, ml_wq, ml_wk, ml_wv, ml_w_gate, ml_b_gate,
                 ml_norm_w, ml_skip)
    cond = jnp.concatenate([c, c_ctx.reshape(1, D), jnp.zeros((V7X_SUBLANES - B - 1, D), F32)], axis=0)
    for i in range(depth):
        last = i == depth - 1
        mod = adaln(cond, mod_w[i], mod_b[i])
        lat = [mod[:B, k * D:(k + 1) * D].reshape(B, 1, D) for k in range(6)]
        cm = [jnp.broadcast_to(mod[B:B + 1, k * D:(k + 1) * D].reshape(1, 1, D), (B, 1, D)) for k in range(6)]
        lat_args = (x, norm_g[i, 0], lat[0], lat[1])
        ctx_args = (ctx, norm_g[i, 0], cm[0], cm[1])
        j = i // n_mixers
        if i % n_mixers == 0:
            p = tuple(a[j] for a in hy_params)
            w_out_b = hy_w_out[j].astype(BF16)
            x = res_gate_matmul(hyena_mix_pre(lat_args, p, dense=False), w_out_b, hy_b_out[j], lat[2], x)
            if not last:
                ctx = res_gate_matmul(hyena_mix_pre(ctx_args, p, dense=True), w_out_b, hy_b_out[j], cm[2], ctx)
        else:
            assert last, "the mLSTM mixer is only implemented for the last layer (no context output)"
            p = tuple(a[j] for a in ml_params)
            act = mlstm_mix_pre(lat_args, ctx_args, p)
            x = res_gate_matmul(act, ml_w_down[j].astype(BF16), jnp.zeros((D,), F32), lat[2], x)
        w_up_b = ffn_w_up[i].astype(BF16)
        w_down_b = ffn_w_down[i].astype(BF16)
        x = conv_ffn_residual(x, norm_g[i, 1], lat[3], lat[4], lat[5], w_up_b, ffn_conv_w[i], ffn_conv_b[i],
                              w_down_b, rows, GRID_W, final_g if last else None)
        if not last:
            ctx = conv_ffn_residual(ctx, norm_g[i, 1], cm[3], cm[4], cm[5], w_up_b, ffn_conv_w[i],
                                    ffn_conv_b[i], w_down_b, 1, ctx_len)
    return x
```

```python
import functools
import math

import jax
import jax.numpy as jnp
from jax import lax
from jax.experimental import pallas as pl
from jax.experimental.pallas import tpu as pltpu

F32 = jnp.float32
BF16 = jnp.bfloat16
HIGHEST = lax.Precision.HIGHEST

EPS = 1e-6
ML_NORM_EPS = 1e-5
GRID_W = 64
ML_HEADS = 4
ML_CHUNK = 128
ML_QKV_BLOCK = 4
HY_EMB_BANDS = 16
HY_MAX_DECAY = math.log(1e-2) / 0.3
HY_MIN_DECAY = math.log(1e-2) / 1.5

V7X_LANES = 128
V7X_SUBLANES = 8
V7X_MXU_DIM = 256
V7X_VMEM_BYTES = 64 * 1024 * 1024
VMEM_LIMIT = V7X_VMEM_BYTES - 8 * 1024 * 1024


def _cparams(*sem):
    return pltpu.CompilerParams(dimension_semantics=("arbitrary",) * len(sem), vmem_limit_bytes=VMEM_LIMIT)


def _dot(a, b, precision=None):
    return jnp.dot(a, b, preferred_element_type=F32, precision=precision)


def _row_tile(n, target):
    t = min(n, target)
    assert n % t == 0, (n, t)
    return t


def _adaln_kernel(c_ref, w_ref, b_ref, o_ref):
    c = c_ref[...]
    s = c * jax.nn.sigmoid(c)
    o_ref[...] = _dot(s, w_ref[...], HIGHEST) + b_ref[...]


def adaln(cond8, w, b):
    r, d = cond8.shape
    n = w.shape[1]
    tn = _row_tile(n, 1536)
    return pl.pallas_call(
        _adaln_kernel,
        grid=(n // tn,),
        in_specs=[pl.BlockSpec((r, d), lambda j: (0, 0)),
                  pl.BlockSpec((d, tn), lambda j: (0, j)),
                  pl.BlockSpec((1, tn), lambda j: (0, j))],
        out_specs=pl.BlockSpec((r, tn), lambda j: (0, j)),
        out_shape=jax.ShapeDtypeStruct((r, n), F32),
        compiler_params=_cparams("arbitrary"),
        name="adaln",
    )(cond8, w, b.reshape(1, n))


def _nmm_kernel(x_ref, g_ref, sh_ref, sc_ref, w_ref, b_ref, o_ref):
    x = x_ref[0]
    y = x * lax.rsqrt(jnp.mean(x * x, axis=-1, keepdims=True) + EPS)
    u = (y * g_ref[...]) * (1.0 + sc_ref[0]) + sh_ref[0]
    acc = _dot(u.astype(BF16), w_ref[...])
    o_ref[0] = (acc + b_ref[...]).astype(o_ref.dtype)


def norm_mod_matmul(x, g, shift, scale, w_bf16, bias, tm, out_dtype=BF16):
    B, L, D = x.shape
    n = w_bf16.shape[1]
    tm = _row_tile(L, tm)
    return pl.pallas_call(
        _nmm_kernel,
        grid=(B, L // tm),
        in_specs=[pl.BlockSpec((1, tm, D), lambda b, i: (b, i, 0)),
                  pl.BlockSpec((1, D), lambda b, i: (0, 0)),
                  pl.BlockSpec((1, 1, D), lambda b, i: (b, 0, 0)),
                  pl.BlockSpec((1, 1, D), lambda b, i: (b, 0, 0)),
                  pl.BlockSpec((D, n), lambda b, i: (0, 0)),
                  pl.BlockSpec((1, n), lambda b, i: (0, 0))],
        out_specs=pl.BlockSpec((1, tm, n), lambda b, i: (b, i, 0)),
        out_shape=jax.ShapeDtypeStruct((B, L, n), out_dtype),
        compiler_params=_cparams("parallel", "parallel"),
        name="norm_mod_matmul",
    )(x, g.reshape(1, D), shift, scale, w_bf16, bias.reshape(1, n))


def _rgm_kernel(a_ref, w_ref, b_ref, gate_ref, res_ref, fg_ref, o_ref, *, final_norm):
    acc = _dot(a_ref[0], w_ref[...]) + b_ref[...]
    x = res_ref[0] + gate_ref[0] * acc
    if final_norm:
        x = (x * lax.rsqrt(jnp.mean(x * x, axis=-1, keepdims=True) + EPS)) * fg_ref[...]
    o_ref[0] = x


def res_gate_matmul(a_bf16, w_bf16, bias, gate, res, final_g=None, tm=512):
    B, L, K = a_bf16.shape
    D = w_bf16.shape[1]
    tm = _row_tile(L, tm)
    final_norm = final_g is not None
    fg = final_g if final_norm else jnp.ones((D,), F32)
    return pl.pallas_call(
        functools.partial(_rgm_kernel, final_norm=final_norm),
        grid=(B, L // tm),
        in_specs=[pl.BlockSpec((1, tm, K), lambda b, i: (b, i, 0)),
                  pl.BlockSpec((K, D), lambda b, i: (0, 0)),
                  pl.BlockSpec((1, D), lambda b, i: (0, 0)),
                  pl.BlockSpec((1, 1, D), lambda b, i: (b, 0, 0)),
                  pl.BlockSpec((1, tm, D), lambda b, i: (b, i, 0)),
                  pl.BlockSpec((1, D), lambda b, i: (0, 0))],
        out_specs=pl.BlockSpec((1, tm, D), lambda b, i: (b, i, 0)),
        out_shape=jax.ShapeDtypeStruct((B, L, D), F32),
        compiler_params=_cparams("parallel", "parallel"),
        name="res_gate_matmul",
    )(a_bf16, w_bf16, bias.reshape(1, D), gate, res, fg.reshape(1, D))


HALO_ROWS = 16


def _halo_specs(tm, L, C, cblock=0):
    r = HALO_ROWS
    nb = L // r
    prev = pl.BlockSpec((1, r, C), lambda b, i: (b, jnp.maximum(i * (tm // r) - 1, 0), cblock))
    nxt = pl.BlockSpec((1, r, C), lambda b, i: (b, jnp.minimum((i + 1) * (tm // r), nb - 1), cblock))
    return prev, nxt


def _conv3_rows(x, prev_ref, next_ref, w_ref, b_ref):
    tm = x.shape[0]
    i = pl.program_id(1)
    last = pl.num_programs(1) - 1
    prev_row = jnp.where(i > 0, prev_ref[0].astype(F32)[HALO_ROWS - 1:HALO_ROWS, :], 0.0)
    next_row = jnp.where(i < last, next_ref[0].astype(F32)[0:1, :], 0.0)
    row = lax.broadcasted_iota(jnp.int32, x.shape, 0)
    xm1 = jnp.where(row == 0, prev_row, pltpu.roll(x, 1, 0))
    xp1 = jnp.where(row == tm - 1, next_row, pltpu.roll(x, tm - 1, 0))
    return w_ref[0:1, :] * xm1 + w_ref[1:2, :] * x + w_ref[2:3, :] * xp1 + b_ref[...]


def _hy_gate_kernel(p_ref, prev_ref, next_ref, w_ref, b_ref, x0_ref, z_ref):
    D = x0_ref.shape[-1]
    conv = _conv3_rows(p_ref[0].astype(F32), prev_ref, next_ref, w_ref, b_ref)
    x0_ref[0] = conv[:, :D]
    z_ref[0] = conv[:, D:2 * D] * conv[:, 2 * D:]


def hy_gate(proj, sc_w, sc_b, tm=512):
    B, L, C3 = proj.shape
    D = C3 // 3
    tm = _row_tile(L, tm)
    prev, nxt = _halo_specs(tm, L, C3)
    out = jax.ShapeDtypeStruct((B, L, D), F32)
    ospec = pl.BlockSpec((1, tm, D), lambda b, i: (b, i, 0))
    return pl.pallas_call(
        _hy_gate_kernel,
        grid=(B, L // tm),
        in_specs=[pl.BlockSpec((1, tm, C3), lambda b, i: (b, i, 0)), prev, nxt,
                  pl.BlockSpec((3, C3), lambda b, i: (0, 0)),
                  pl.BlockSpec((1, C3), lambda b, i: (0, 0))],
        out_specs=(ospec, ospec),
        out_shape=(out, out),
        compiler_params=_cparams("parallel", "arbitrary"),
        name="hy_gate",
    )(proj, proj, proj, sc_w, sc_b.reshape(1, C3))


def _filter_kernel(w1_ref, b1_ref, w2_ref, b2_ref, w3_ref, b3_ref, wo_ref, fr_ref, band_ref,
                   h_ref, s_ref, *, L, D):
    i = pl.program_id(0)
    tl = h_ref.shape[0]
    rowi = lax.broadcasted_iota(jnp.int32, (tl, 1), 0) + i * tl
    row = rowi.astype(F32)
    t = row / (L - 1.0)
    w = (2.0 * math.pi) * row / L
    col = lax.broadcasted_iota(jnp.int32, (tl, V7X_LANES), 1)
    ang = w * band_ref[...]
    pos = jnp.where(col == 0, t,
                    jnp.where(col <= HY_EMB_BANDS, jnp.cos(ang),
                              jnp.where(col <= 2 * HY_EMB_BANDS, -jnp.sin(ang), 0.0)))
    fr = fr_ref[...]
    h = jnp.sin(fr * (_dot(pos, w1_ref[...], HIGHEST) + b1_ref[...]))
    h = jnp.sin(fr * (_dot(h, w2_ref[...], HIGHEST) + b2_ref[...]))
    h = jnp.sin(fr * (_dot(h, w3_ref[...], HIGHEST) + b3_ref[...]))
    h = _dot(h, wo_ref[...], HIGHEST)
    dcol = lax.broadcasted_iota(jnp.int32, (1, 2 * D), 1)
    chan = jnp.where(dcol >= D, dcol - D, dcol).astype(F32)
    deltas = HY_MIN_DECAY + chan * ((HY_MAX_DECAY - HY_MIN_DECAY) / (D - 1.0))
    h = h * jnp.exp(-t * jnp.abs(deltas))
    h = jnp.where((rowi == 0) & (dcol >= D), 0.0, h)
    h_ref[...] = h
    part = jnp.sum(jnp.abs(h), axis=0, keepdims=True)

    @pl.when(i == 0)
    def _():
        s_ref[...] = part

    @pl.when(i > 0)
    def _():
        s_ref[...] += part


def hyena_filter(L, f_w1, f_b1, f_w2, f_b2, f_w3, f_b3, f_wout, freq):
    W = f_w2.shape[0]
    D2 = f_wout.shape[1]
    P = V7X_LANES
    pad2 = lambda a: jnp.pad(a, ((0, P - a.shape[0]), (0, P - a.shape[1])))
    padv = lambda a: jnp.pad(a.reshape(1, -1), ((0, 0), (0, P - a.shape[0])))
    w1 = pad2(f_w1)
    w2 = pad2(f_w2)
    w3 = pad2(f_w3)
    wo = jnp.pad(f_wout, ((0, P - W), (0, 0)))
    bands = jnp.linspace(1e-4, HY_EMB_BANDS - 1, HY_EMB_BANDS, dtype=F32)
    band_row = jnp.concatenate([jnp.zeros((1,), F32), bands, bands,
                                jnp.zeros((P - 1 - 2 * HY_EMB_BANDS,), F32)]).reshape(1, P)
    tl = _row_tile(L, 512)
    full = lambda shape: pl.BlockSpec(shape, lambda i: (0, 0))
    return pl.pallas_call(
        functools.partial(_filter_kernel, L=L, D=D2 // 2),
        grid=(L // tl,),
        in_specs=[full((P, P)), full((1, P)), full((P, P)), full((1, P)), full((P, P)), full((1, P)),
                  full((P, D2)), full((1, P)), full((1, P))],
        out_specs=(pl.BlockSpec((tl, D2), lambda i: (i, 0)), pl.BlockSpec((1, D2), lambda i: (0, 0))),
        out_shape=(jax.ShapeDtypeStruct((L, D2), F32), jax.ShapeDtypeStruct((1, D2), F32)),
        compiler_params=_cparams("arbitrary"),
        name="hyena_filter",
    )(w1, padv(f_b1), w2, padv(f_b2), w3, padv(f_b3), wo, padv(freq), band_row)


DFT_DTYPE = BF16


def _cos_sin(p, n):
    ang = (2.0 * math.pi / n) * p.astype(F32)
    return jnp.cos(ang), jnp.sin(ang)


def _dft_dot(g, x):
    return _dot(g.astype(DFT_DTYPE), x.astype(DFT_DTYPE), HIGHEST if DFT_DTYPE == F32 else None)


def _dft_tables(N1, N2):
    N = N1 * N2
    ar = lambda n: jnp.arange(n, dtype=jnp.int32)
    c, s = _cos_sin((ar(N1)[:, None] * ar(N1 // 2)[None, :]) % N1, N1)
    g1 = jnp.concatenate([c, -s], axis=0)
    k1 = ar(N1)[:, None, None]
    a = ar(N2)[None, :, None]
    b = ar(N2)[None, None, :]
    c, s = _cos_sin((b * (k1 + N1 * a)) % N, N)
    g2 = jnp.concatenate([jnp.concatenate([c, s], axis=2), jnp.concatenate([-s, c], axis=2)], axis=1)
    c, s = _cos_sin((a * (k1 + N1 * b)) % N, N)
    g2i = jnp.concatenate([jnp.concatenate([c, -s], axis=2), jnp.concatenate([s, c], axis=2)], axis=1)
    c, s = _cos_sin((ar(N1 // 2)[:, None] * ar(N1)[None, :]) % N1, N1)
    g4 = jnp.concatenate([c, -s], axis=1) * (1.0 / N)
    return tuple(t.astype(DFT_DTYPE) for t in (g1, g2, g2i, g4))


def _stage1_kernel(g_ref, x_ref, o_ref):
    for s in range(x_ref.shape[2]):
        o_ref[0, :, s, :] = _dft_dot(g_ref[...], x_ref[0, :, s, :])


def dft_stage1(g, x4, ns):
    B, K, N2, C = x4.shape
    M = g.shape[0]
    return pl.pallas_call(
        _stage1_kernel,
        grid=(B, N2 // ns),
        in_specs=[pl.BlockSpec((M, K), lambda b, j: (0, 0)),
                  pl.BlockSpec((1, K, ns, C), lambda b, j: (b, 0, j, 0))],
        out_specs=pl.BlockSpec((1, M, ns, C), lambda b, j: (b, 0, j, 0)),
        out_shape=jax.ShapeDtypeStruct((B, M, N2, C), F32),
        compiler_params=_cparams("parallel", "parallel"),
        name="dft_stage1",
    )(g, x4)


def _combine_spectrum(X, nrm_ref, D):
    n = X.shape[0] // 2
    nrm = nrm_ref[:, :D] + nrm_ref[:, D:]
    hre = (X[:n, :D] + X[:n, D:]) / nrm
    him = (X[n:, :D] - X[n:, D:]) / nrm
    return hre, him


def _filter_spec_kernel(g_ref, a_ref, nrm_ref, o_ref):
    _, _, n2, D2 = a_ref.shape
    slab = a_ref[:, 0].reshape(2 * n2, D2)
    X = _dft_dot(g_ref[0], slab)
    hre, him = _combine_spectrum(X, nrm_ref, D2 // 2)
    o_ref[0, 0] = hre
    o_ref[1, 0] = him


def filter_spectrum(g2, a, nrm):
    _, N1, N2, D2 = a.shape
    D = D2 // 2
    return pl.pallas_call(
        _filter_spec_kernel,
        grid=(N1,),
        in_specs=[pl.BlockSpec((1, 2 * N2, 2 * N2), lambda k: (k, 0, 0)),
                  pl.BlockSpec((2, 1, N2, D2), lambda k: (0, k, 0, 0)),
                  pl.BlockSpec((1, D2), lambda k: (0, 0))],
        out_specs=pl.BlockSpec((2, 1, N2, D), lambda k: (0, k, 0, 0)),
        out_shape=jax.ShapeDtypeStruct((2, N1, N2, D), F32),
        compiler_params=_cparams("parallel"),
        name="filter_spectrum",
    )(g2, a, nrm)


def _cmul(xr, xi, hr, hi):
    return xr * hr - xi * hi, xr * hi + xi * hr


def _spec_mul_kernel(g_ref, gi_ref, h_ref, a_ref, o_ref):
    n2, C = a_ref.shape[-2:]
    slab = a_ref[0, :, 0].reshape(2 * n2, C)
    X = _dft_dot(g_ref[0], slab)
    pr, pi = _cmul(X[:n2], X[n2:], h_ref[0, 0], h_ref[1, 0])
    Y = _dft_dot(gi_ref[0], jnp.concatenate([pr, pi], axis=0))
    o_ref[0, :, 0] = Y.reshape(2, n2, C)


def spectrum_multiply(g2, g2i, H, a):
    B, _, N1, N2, C = a.shape
    blk = pl.BlockSpec((1, 2, 1, N2, C), lambda k, b: (b, 0, k, 0, 0))
    gsp = pl.BlockSpec((1, 2 * N2, 2 * N2), lambda k, b: (k, 0, 0))
    return pl.pallas_call(
        _spec_mul_kernel,
        grid=(N1, B),
        in_specs=[gsp, gsp, pl.BlockSpec((2, 1, N2, C), lambda k, b: (0, k, 0, 0)), blk],
        out_specs=blk,
        out_shape=jax.ShapeDtypeStruct(a.shape, F32),
        compiler_params=_cparams("parallel", "parallel"),
        name="spectrum_multiply",
    )(g2, g2i, H, a)


def _idft_out_kernel(g_ref, y_ref, x0_ref, z_ref, bias_ref, o_ref, acc_ref):
    for s in range(y_ref.shape[2]):
        y = _dft_dot(g_ref[...], y_ref[0, :, s, :])
        acc_ref[:, s, :] = x0_ref[0, :, s, :] * (y + z_ref[0, :, s, :] * bias_ref[...])
    o_ref[0] = acc_ref[...].astype(o_ref.dtype)


def idft_gate_out(g4, yv, x0, z, bias, ns, cl):
    B, K, N2, C = yv.shape
    M = g4.shape[0]
    dat = pl.BlockSpec((1, M, ns, cl), lambda b, j, c: (b, 0, j, c))
    return pl.pallas_call(
        _idft_out_kernel,
        grid=(B, N2 // ns, C // cl),
        in_specs=[pl.BlockSpec((M, K), lambda b, j, c: (0, 0)),
                  pl.BlockSpec((1, K, ns, cl), lambda b, j, c: (b, 0, j, c)),
                  dat, dat, pl.BlockSpec((1, cl), lambda b, j, c: (0, c))],
        out_specs=dat,
        out_shape=jax.ShapeDtypeStruct((B, M, N2, C), BF16),
        scratch_shapes=[pltpu.VMEM((M, ns, cl), F32)],
        compiler_params=_cparams("parallel", "parallel", "parallel"),
        name="idft_gate_out",
    )(g4, yv, x0, z, bias.reshape(1, C))


def long_conv_two_stage(x0, z, hfb, nrm, bias):
    B, L, C = z.shape
    N2 = V7X_LANES
    N1 = 2 * L // N2
    g1, g2, g2i, g4 = _dft_tables(N1, N2)
    ns = V7X_SUBLANES
    a_f = dft_stage1(g1, hfb.reshape(1, N1 // 2, N2, 2 * C), ns)
    H = filter_spectrum(g2, a_f.reshape(2, N1, N2, 2 * C), nrm)
    a = dft_stage1(g1, z.reshape(B, N1 // 2, N2, C), ns)
    yv = spectrum_multiply(g2, g2i, H, a.reshape(B, 2, N1, N2, C))
    out = idft_gate_out(g4, yv.reshape(B, 2 * N1, N2, C), x0.reshape(B, N1 // 2, N2, C),
                        z.reshape(B, N1 // 2, N2, C), bias, 2 * ns, min(C, 512))
    return out.reshape(B, L, C)


def _dense_tables(L):
    N = 2 * L
    ar = lambda n: jnp.arange(n, dtype=jnp.int32)
    c, s = _cos_sin((ar(N)[:, None] * ar(L)[None, :]) % N, N)
    g = jnp.concatenate([c, -s], axis=0)
    c, s = _cos_sin((ar(L)[:, None] * ar(N)[None, :]) % N, N)
    gi = jnp.concatenate([c, -s], axis=1) * (1.0 / N)
    return g.astype(DFT_DTYPE), gi.astype(DFT_DTYPE)


def _dense_spec_kernel(g_ref, hfb_ref, nrm_ref, o_ref):
    X = _dft_dot(g_ref[...], hfb_ref[...])
    hre, him = _combine_spectrum(X, nrm_ref, o_ref.shape[-1])
    o_ref[0] = hre
    o_ref[1] = him


def _dense_conv_kernel(g_ref, gi_ref, h_ref, z_ref, x0_ref, bias_ref, o_ref):
    z = z_ref[0]
    X = _dft_dot(g_ref[...], z)
    n = X.shape[0] // 2
    pr, pi = _cmul(X[:n], X[n:], h_ref[0], h_ref[1])
    y = _dft_dot(gi_ref[...], jnp.concatenate([pr, pi], axis=0))
    o_ref[0] = (x0_ref[0] * (y + z * bias_ref[...])).astype(o_ref.dtype)


def long_conv_dense(x0, z, hfb, nrm, bias):
    B, L, C = z.shape
    N = 2 * L
    g, gi = _dense_tables(L)
    full2 = lambda shape: pl.BlockSpec(shape, lambda *_: (0,) * len(shape))
    H = pl.pallas_call(
        _dense_spec_kernel,
        grid=(1,),
        in_specs=[full2((2 * N, L)), full2((L, 2 * C)), full2((1, 2 * C))],
        out_specs=full2((2, N, C)),
        out_shape=jax.ShapeDtypeStruct((2, N, C), F32),
        compiler_params=_cparams("arbitrary"),
        name="dense_filter_spectrum",
    )(g, hfb, nrm)
    dat = pl.BlockSpec((1, L, C), lambda b: (b, 0, 0))
    return pl.pallas_call(
        _dense_conv_kernel,
        grid=(B,),
        in_specs=[full2((2 * N, L)), full2((L, 2 * N)), full2((2, N, C)), dat, dat, full2((1, C))],
        out_specs=dat,
        out_shape=jax.ShapeDtypeStruct((B, L, C), BF16),
        compiler_params=_cparams("parallel"),
        name="dense_long_conv",
    )(g, gi, H, z, x0, bias.reshape(1, C))


def hyena_mix_pre(u_args, p, dense):
    x, g, shift, scale = u_args
    (w_in, b_in, sc_w, sc_b, f_w1, f_b1, f_w2, f_b2, f_w3, f_b3, f_wout, freq, bias) = p
    L = x.shape[1]
    proj = norm_mod_matmul(x, g, shift, scale, w_in.astype(BF16), b_in, tm=512)
    x0, z = hy_gate(proj, sc_w, sc_b)
    hfb, nrm = hyena_filter(L, f_w1, f_b1, f_w2, f_b2, f_w3, f_b3, f_wout, freq)
    conv = long_conv_dense if dense else long_conv_two_stage
    return conv(x0, z, hfb, nrm, bias)


def _ml_prep_kernel(xm_ref, prev_ref, next_ref, cw_ref, cb_ref, wq_ref, wk_ref, wv_ref, wg_ref, bg_ref,
                    q_ref, k_ref, v_ref, xc_ref, g_ref, *, k_scale):
    xm = xm_ref[0].astype(F32)
    inner = xm.shape[1]
    conv = _conv3_rows(xm, prev_ref, next_ref, cw_ref, cb_ref)
    xc = conv * jax.nn.sigmoid(conv)
    xc_ref[0] = xc
    xcb = xc.astype(BF16)
    xmb = xm.astype(BF16)
    gw = V7X_MXU_DIM
    gates = bg_ref[...]
    for j in range(inner // gw):
        sl = slice(j * gw, (j + 1) * gw)
        q = _dot(xcb[:, sl], wq_ref[j])
        k = _dot(xcb[:, sl], wk_ref[j])
        v = _dot(xmb[:, sl], wv_ref[j])
        q_ref[0, :, sl] = q.astype(BF16)
        k_ref[0, :, sl] = (k * k_scale).astype(BF16)
        v_ref[0, :, sl] = v.astype(BF16)
        gates = gates + _dot(q.astype(BF16), wg_ref[j * gw:(j + 1) * gw, :])
        gates = gates + _dot(k.astype(BF16), wg_ref[inner + j * gw:inner + (j + 1) * gw, :])
        gates = gates + _dot(v.astype(BF16), wg_ref[2 * inner + j * gw:2 * inner + (j + 1) * gw, :])
    g_ref[0] = gates


def _block_diag(w, group):
    nb, bs, _ = w.shape
    per = group // bs
    w = w.reshape(nb // per, per, bs, bs)
    eye = jnp.eye(per, dtype=w.dtype)
    dense = jnp.einsum("gpce,pr->gpcre", w, eye)
    return dense.reshape(nb // per, group, group).astype(BF16)


def ml_prep(xz, conv_w, conv_b, wq, wk, wv, w_gate, b_gate, tm=256):
    B, L, C2 = xz.shape
    inner = C2 // 2
    dh = inner // ML_HEADS
    tm = _row_tile(L, tm)
    gw = V7X_MXU_DIM
    ng = inner // gw
    P = V7X_LANES
    wg = jnp.pad(w_gate, ((0, 0), (0, P - w_gate.shape[1]))).astype(BF16)
    bg = jnp.pad(b_gate.reshape(1, -1), ((0, 0), (0, P - b_gate.shape[0])))
    prev, nxt = _halo_specs(tm, L, inner)
    c2 = lambda shape: pl.BlockSpec(shape, lambda b, i: (0,) * len(shape))
    row = lambda n: pl.BlockSpec((1, tm, n), lambda b, i: (b, i, 0))
    sd = lambda n, dt: jax.ShapeDtypeStruct((B, L, n), dt)
    return pl.pallas_call(
        functools.partial(_ml_prep_kernel, k_scale=dh ** -0.5),
        grid=(B, L // tm),
        in_specs=[row(inner), prev, nxt, c2((3, inner)), c2((1, inner)),
                  c2((ng, gw, gw)), c2((ng, gw, gw)), c2((ng, gw, gw)), c2((3 * inner, P)), c2((1, P))],
        out_specs=(row(inner), row(inner), row(inner), row(inner), row(P)),
        out_shape=(sd(inner, BF16), sd(inner, BF16), sd(inner, BF16), sd(inner, F32), sd(P, F32)),
        compiler_params=_cparams("parallel", "arbitrary"),
        name="ml_prep",
    )(xz, xz, xz, conv_w, conv_b.reshape(1, inner), _block_diag(wq, gw), _block_diag(wk, gw),
      _block_diag(wv, gw), wg, bg)


def _gates_layout(gates, nc):
    B, L, _ = gates.shape
    g = gates[..., :4 * ML_HEADS].reshape(B, nc, ML_CHUNK, 2, 2, ML_HEADS)
    return jnp.transpose(g, (3, 0, 5, 1, 4, 2))


def _mlstm_kernel(q_ref, k_ref, v_ref, g_ref, c0_ref, m0_ref, h_ref, cf_ref, mf_ref, c_scr, m_scr):
    d = pl.program_id(0)
    t = pl.program_id(3)
    nct = pl.num_programs(3)
    T = ML_CHUNK
    dh = q_ref.shape[-1]

    @pl.when(t == 0)
    def _():
        c_scr[...] = c0_ref[0, 0, 0]
        m_scr[...] = m0_ref[0, 0, 0]

    rev = d == 1
    gt = g_ref[0, 0, 0, 0]
    ig = gt[0:1, :]
    fg = gt[1:2, :]
    lf = jnp.minimum(fg, 0.0) - jnp.log1p(jnp.exp(-jnp.abs(fg)))
    lf8 = jnp.broadcast_to(lf, (V7X_SUBLANES, T))
    lane = lax.broadcasted_iota(jnp.int32, (V7X_SUBLANES, T), 1)
    csf = lf8
    csr = lf8
    s = 1
    while s < T:
        csf = csf + jnp.where(lane >= s, pltpu.roll(csf, s, 1), 0.0)
        csr = csr + jnp.where(lane < T - s, pltpu.roll(csr, T - s, 1), 0.0)
        s *= 2
    bcs = jnp.where(rev, csr, csf)[0:1, :]
    b_last = jnp.sum(lf, axis=1, keepdims=True)
    gr = ig - bcs
    bc = jnp.broadcast_to(bcs, (T, T)).T
    row = lax.broadcasted_iota(jnp.int32, (T, T), 0)
    col = lax.broadcasted_iota(jnp.int32, (T, T), 1)
    mask = (col - row) * (1 - 2 * d) <= 0
    dmat = jnp.where(mask, bc + gr, -jnp.inf)
    m_prev = m_scr[0:1, 0:1]
    a_inter = bc[:, 0:1] + m_prev
    m_row = jnp.maximum(a_inter, jnp.max(dmat, axis=1, keepdims=True))
    w_inter = jnp.exp(a_inter - m_row)
    pmat = jnp.exp(dmat - m_row)

    q = q_ref[0]
    v = v_ref[0]
    kT = k_ref[0].astype(F32).T.astype(BF16)
    smat = _dot(q, kT) * pmat
    r1 = _dot(q, c_scr[...].astype(BF16))
    ones_col = (lax.broadcasted_iota(jnp.int32, (T, V7X_LANES), 1) == 0).astype(F32)
    v_aug = jnp.concatenate([v.astype(F32), ones_col], axis=1)
    r2 = _dot(smat.astype(BF16), v_aug.astype(BF16))
    num = w_inter * r1[:, :dh] + r2[:, :dh]
    den = w_inter * r1[:, dh:dh + 1] + r2[:, dh:dh + 1]
    h_ref[0, 0] = num / jnp.maximum(jnp.abs(den), jnp.exp(-m_row))

    e_row = b_last + gr
    m_new = jnp.maximum(b_last + m_prev, jnp.max(e_row, axis=1, keepdims=True))
    g_old = jnp.exp(b_last + m_prev - m_new)
    w_row = jnp.exp(e_row - m_new)
    w_col = jnp.broadcast_to(w_row, (T, T)).T[:, 0:1]
    c_scr[...] = g_old * c_scr[...] + _dot(kT, (v_aug * w_col).astype(BF16))
    m_scr[...] = jnp.broadcast_to(m_new, m_scr.shape)

    @pl.when(t == nct - 1)
    def _():
        cf_ref[0, 0, 0] = c_scr[...]
        mf_ref[0, 0, 0] = m_scr[...]


def mlstm_scan(q, k, v, gates, c0, m0):
    B, L, inner = q.shape
    H = ML_HEADS
    dh = inner // H
    T = ML_CHUNK
    nc = L // T
    da = dh + V7X_LANES
    cidx = lambda d, t: t + d * (nc - 1 - 2 * t)
    qkv = pl.BlockSpec((1, T, dh), lambda d, b, h, t: (b, cidx(d, t), h))
    cst = pl.BlockSpec((1, 1, 1, dh, da), lambda d, b, h, t: (d, b, h, 0, 0))
    mst = pl.BlockSpec((1, 1, 1, V7X_SUBLANES, V7X_LANES), lambda d, b, h, t: (d, b, h, 0, 0))
    return pl.pallas_call(
        _mlstm_kernel,
        grid=(2, B, H, nc),
        in_specs=[qkv, qkv, qkv,
                  pl.BlockSpec((1, 1, 1, 1, 2, T), lambda d, b, h, t: (d, b, h, cidx(d, t), 0, 0)),
                  cst, mst],
        out_specs=(pl.BlockSpec((1, 1, T, dh), lambda d, b, h, t: (d, b, cidx(d, t), h)), cst, mst),
        out_shape=(jax.ShapeDtypeStruct((2, B, L, inner), F32),
                   jax.ShapeDtypeStruct(c0.shape, F32), jax.ShapeDtypeStruct(m0.shape, F32)),
        scratch_shapes=[pltpu.VMEM((dh, da), F32), pltpu.VMEM((V7X_SUBLANES, V7X_LANES), F32)],
        compiler_params=_cparams("parallel", "parallel", "parallel", "arbitrary"),
        name="mlstm_scan",
    )(q, k, v, gates, c0, m0)


def _ml_out_kernel(hf_ref, hb_ref, xc_ref, z_ref, nw_ref, sk_ref, o_ref):
    h = hf_ref[0, 0] + hb_ref[0, 0]
    dh = h.shape[1] // ML_HEADS
    z = z_ref[0].astype(F32)
    gate = z * jax.nn.sigmoid(z)
    for j in range(ML_HEADS):
        sl = slice(j * dh, (j + 1) * dh)
        seg = h[:, sl]
        mu = jnp.mean(seg, axis=-1, keepdims=True)
        cen = seg - mu
        var = jnp.mean(cen * cen, axis=-1, keepdims=True)
        hn = cen * lax.rsqrt(var + ML_NORM_EPS)
        hs = hn * nw_ref[:, sl] + sk_ref[:, sl] * xc_ref[0, :, sl]
        o_ref[0, :, sl] = (hs * gate[:, sl]).astype(o_ref.dtype)


def ml_out(h2, xc, xz, norm_w, skip, tm=256):
    _, B, L, inner = h2.shape
    tm = _row_tile(L, tm)
    vec = pl.BlockSpec((1, inner), lambda b, i: (0, 0))
    return pl.pallas_call(
        _ml_out_kernel,
        grid=(B, L // tm),
        in_specs=[pl.BlockSpec((1, 1, tm, inner), lambda b, i: (0, b, i, 0)),
                  pl.BlockSpec((1, 1, tm, inner), lambda b, i: (1, b, i, 0)),
                  pl.BlockSpec((1, tm, inner), lambda b, i: (b, i, 0)),
                  pl.BlockSpec((1, tm, inner), lambda b, i: (b, i, 1)), vec, vec],
        out_specs=pl.BlockSpec((1, tm, inner), lambda b, i: (b, i, 0)),
        out_shape=jax.ShapeDtypeStruct((B, L, inner), BF16),
        compiler_params=_cparams("parallel", "parallel"),
        name="ml_out",
    )(h2, h2, xc, xz, norm_w.reshape(1, inner), skip.reshape(1, inner))


def mlstm_mix_pre(lat_args, ctx_args, p):
    w_in, conv_w, conv_b, wq, wk, wv, w_gate, b_gate, norm_w, skip = p
    w_in_b = w_in.astype(BF16)
    zero_b = jnp.zeros((w_in.shape[1],), F32)

    def prep(args):
        x, g, shift, scale = args
        xz = norm_mod_matmul(x, g, shift, scale, w_in_b, zero_b, tm=512)
        q, k, v, xc, gates = ml_prep(xz, conv_w, conv_b, wq, wk, wv, w_gate, b_gate)
        return q, k, v, xc, xz, _gates_layout(gates, x.shape[1] // ML_CHUNK)

    qc, kc, vc, _, _, gc = prep(ctx_args)
    q, k, v, xc, xz, gl = prep(lat_args)
    B, _, inner = q.shape
    dh = inner // ML_HEADS
    c0 = jnp.zeros((2, B, ML_HEADS, dh, dh + V7X_LANES), F32)
    m0 = jnp.zeros((2, B, ML_HEADS, V7X_SUBLANES, V7X_LANES), F32)
    _, c1, m1 = mlstm_scan(qc, kc, vc, gc, c0, m0)
    h2, _, _ = mlstm_scan(q, k, v, gl, c1, m1)
    return ml_out(h2, xc, xz, norm_w, skip)


def _ffn_conv_kernel(a_ref, g_ref, prev_ref, next_ref, w_ref, b_ref, o_ref, *, cols, vertical):
    tm = g_ref.shape[1]
    i = pl.program_id(1)
    last = pl.num_programs(1) - 1
    g = g_ref[0].astype(F32)
    if vertical:
        top = jnp.where(i > 0, prev_ref[0].astype(F32), 0.0)
        bot = jnp.where(i < last, next_ref[0].astype(F32), 0.0)
        g = jnp.concatenate([top, g, bot], axis=0)
    R = g.shape[0]
    cpos = jnp.bitwise_and(lax.broadcasted_iota(jnp.int32, (R, 1), 0), cols - 1)
    left = jnp.where(cpos == 0, 0.0, pltpu.roll(g, 1, 0))
    right = jnp.where(cpos == cols - 1, 0.0, pltpu.roll(g, R - 1, 0))
    acc = b_ref[...]
    for dr in (range(3) if vertical else (1,)):
        off = dr * cols if vertical else 0
        sl = slice(off, off + tm)
        acc = acc + (w_ref[3 * dr:3 * dr + 1, :] * left[sl] + w_ref[3 * dr + 1:3 * dr + 2, :] * g[sl]
                     + w_ref[3 * dr + 2:3 * dr + 3, :] * right[sl])
    o_ref[0] = ((acc * jax.nn.sigmoid(acc)) * a_ref[0].astype(F32)).astype(o_ref.dtype)


def ffn_conv_act(ag, conv_w, conv_b, rows, cols, tm=1024):
    B, L, F2 = ag.shape
    F = F2 // 2
    assert cols & (cols - 1) == 0 and rows * cols == L
    vertical = rows > 1
    tm = _row_tile(L, tm) if vertical else L
    cb = V7X_MXU_DIM
    nf = F // cb
    assert F % cb == 0 and tm % cols == 0
    hb = cols if vertical else HALO_ROWS
    nhb = L // hb
    return pl.pallas_call(
        functools.partial(_ffn_conv_kernel, cols=cols, vertical=vertical),
        grid=(B, L // tm, nf),
        in_specs=[pl.BlockSpec((1, tm, cb), lambda b, i, j: (b, i, j)),
                  pl.BlockSpec((1, tm, cb), lambda b, i, j: (b, i, nf + j)),
                  pl.BlockSpec((1, hb, cb), lambda b, i, j: (b, jnp.maximum(i * (tm // hb) - 1, 0), nf + j)),
                  pl.BlockSpec((1, hb, cb), lambda b, i, j: (b, jnp.minimum((i + 1) * (tm // hb), nhb - 1), nf + j)),
                  pl.BlockSpec((9, cb), lambda b, i, j: (0, j)),
                  pl.BlockSpec((1, cb), lambda b, i, j: (0, j))],
        out_specs=pl.BlockSpec((1, tm, cb), lambda b, i, j: (b, i, j)),
        out_shape=jax.ShapeDtypeStruct((B, L, F), BF16),
        compiler_params=_cparams("parallel", "arbitrary", "parallel"),
        name="ffn_conv_act",
    )(ag, ag, ag, ag, conv_w.reshape(9, F), conv_b.reshape(1, F))


def conv_ffn_residual(x, g, shift, scale, gate, w_up_b, conv_w, conv_b, w_down_b, rows, cols, final_g=None):
    ag = norm_mod_matmul(x, g, shift, scale, w_up_b, jnp.zeros((w_up_b.shape[1],), F32), tm=512)
    act = ffn_conv_act(ag, conv_w, conv_b, rows, cols)
    return res_gate_matmul(act, w_down_b, jnp.zeros((x.shape[-1],), F32), gate, x, final_g)


def kernel(x, c, ctx, c_ctx, mod_w, mod_b, norm_g, final_g, hy_w_in, hy_b_in, hy_sc_w, hy_sc_b, hy_f_w1, hy_f_b1, hy_f_w2, hy_f_b2, hy_f_w3, hy_f_b3, hy_f_wout, hy_freq, hy_bias, hy_w_out, hy_b_out, ml_w_in, ml_conv_w, ml_conv_b, ml_wq, ml_wk, ml_wv, ml_w_gate, ml_b_gate, ml_norm_w, ml_skip, ml_w_down, ffn_w_up, ffn_conv_w, ffn_conv_b, ffn_w_down):
    B, L, D = x.shape
    ctx_len = ctx.shape[1]
    depth = mod_w.shape[0]
    n_mixers = 2
    rows = L // GRID_W
    hy_params = (hy_w_in, hy_b_in, hy_sc_w, hy_sc_b, hy_f_w1, hy_f_b1, hy_f_w2, hy_f_b2,
                 hy_f_w3, hy_f_b3, hy_f_wout, hy_freq, hy_bias)
    ml_params = (ml_w_in, ml_conv_w, ml_conv_b, ml_wq, ml_wk, ml_wv, ml_w_gate, ml_b_gate,
                 ml_norm_w, ml_skip)
    cond = jnp.concatenate([c, c_ctx.reshape(1, D), jnp.zeros((V7X_SUBLANES - B - 1, D), F32)], axis=0)
    for i in range(depth):
        last = i == depth - 1
        mod = adaln(cond, mod_w[i], mod_b[i])
        lat = [mod[:B, k * D:(k + 1) * D].reshape(B, 1, D) for k in range(6)]
        cm = [jnp.broadcast_to(mod[B:B + 1, k * D:(k + 1) * D].reshape(1, 1, D), (B, 1, D)) for k in range(6)]
        lat_args = (x, norm_g[i, 0], lat[0], lat[1])
        ctx_args = (ctx, norm_g[i, 0], cm[0], cm[1])
        j = i // n_mixers
        if i % n_mixers == 0:
            p = tuple(a[j] for a in hy_params)
            w_out_b = hy_w_out[j].astype(BF16)
            x = res_gate_matmul(hyena_mix_pre(lat_args, p, dense=False), w_out_b, hy_b_out[j], lat[2], x)
            if not last:
                ctx = res_gate_matmul(hyena_mix_pre(ctx_args, p, dense=True), w_out_b, hy_b_out[j], cm[2], ctx)
        else:
            assert last, "the mLSTM mixer is only implemented for the last layer (no context output)"
            p = tuple(a[j] for a in ml_params)
            act = mlstm_mix_pre(lat_args, ctx_args, p)
            x = res_gate_matmul(act, ml_w_down[j].astype(BF16), jnp.zeros((D,), F32), lat[2], x)
        w_up_b = ffn_w_up[i].astype(BF16)
        w_down_b = ffn_w_down[i].astype(BF16)
        x = conv_ffn_residual(x, norm_g[i, 1], lat[3], lat[4], lat[5], w_up_b, ffn_conv_w[i], ffn_conv_b[i],
                              w_down_b, rows, GRID_W, final_g if last else None)
        if not last:
            ctx = conv_ffn_residual(ctx, norm_g[i, 1], cm[3], cm[4], cm[5], w_up_b, ffn_conv_w[i],
                                    ffn_conv_b[i], w_down_b, 1, ctx_len)
    return x
```

```python
import functools
import math

import jax
import jax.numpy as jnp
from jax import lax
from jax.experimental import pallas as pl
from jax.experimental.pallas import tpu as pltpu

F32 = jnp.float32
BF16 = jnp.bfloat16
HIGHEST = lax.Precision.HIGHEST

EPS = 1e-6
ML_NORM_EPS = 1e-5
GRID_W = 64
ML_HEADS = 4
ML_CHUNK = 128
ML_QKV_BLOCK = 4
HY_EMB_BANDS = 16
HY_MAX_DECAY = math.log(1e-2) / 0.3
HY_MIN_DECAY = math.log(1e-2) / 1.5

V7X_LANES = 128
V7X_SUBLANES = 8
V7X_MXU_DIM = 256
V7X_VMEM_BYTES = 64 * 1024 * 1024
VMEM_LIMIT = V7X_VMEM_BYTES - 8 * 1024 * 1024
HALO_ROWS = 16


def _cparams(*sem):
    return pltpu.CompilerParams(dimension_semantics=("arbitrary",) * len(sem), vmem_limit_bytes=VMEM_LIMIT)


def _dot(a, b, precision=None):
    return jnp.dot(a, b, preferred_element_type=F32, precision=precision)


def _row_tile(n, target):
    t = min(n, target)
    assert n % t == 0, (n, t)
    return t


def _adaln_kernel(c_ref, w_ref, b_ref, o_ref):
    c = c_ref[...]
    s = c * jax.nn.sigmoid(c)
    o_ref[...] = _dot(s, w_ref[...], HIGHEST) + b_ref[...]


def adaln(cond8, w, b):
    r, d = cond8.shape
    n = w.shape[1]
    tn = _row_tile(n, 1536)
    return pl.pallas_call(
        _adaln_kernel,
        grid=(n // tn,),
        in_specs=[pl.BlockSpec((r, d), lambda j: (0, 0)),
                  pl.BlockSpec((d, tn), lambda j: (0, j)),
                  pl.BlockSpec((1, tn), lambda j: (0, j))],
        out_specs=pl.BlockSpec((r, tn), lambda j: (0, j)),
        out_shape=jax.ShapeDtypeStruct((r, n), F32),
        compiler_params=_cparams("arbitrary"),
        name="adaln",
    )(cond8, w, b.reshape(1, n))


def _nmm_kernel(x_ref, g_ref, sh_ref, sc_ref, w_ref, b_ref, o_ref):
    x = x_ref[0]
    y = x * lax.rsqrt(jnp.mean(x * x, axis=-1, keepdims=True) + EPS)
    u = (y * g_ref[...]) * (1.0 + sc_ref[0]) + sh_ref[0]
    acc = _dot(u.astype(BF16), w_ref[...])
    o_ref[0] = (acc + b_ref[...]).astype(o_ref.dtype)


def norm_mod_matmul(x, g, shift, scale, w_bf16, bias, tm, out_dtype=BF16):
    B, L, D = x.shape
    n = w_bf16.shape[1]
    tm = _row_tile(L, tm)
    return pl.pallas_call(
        _nmm_kernel,
        grid=(B, L // tm),
        in_specs=[pl.BlockSpec((1, tm, D), lambda b, i: (b, i, 0)),
                  pl.BlockSpec((1, D), lambda b, i: (0, 0)),
                  pl.BlockSpec((1, 1, D), lambda b, i: (b, 0, 0)),
                  pl.BlockSpec((1, 1, D), lambda b, i: (b, 0, 0)),
                  pl.BlockSpec((D, n), lambda b, i: (0, 0)),
                  pl.BlockSpec((1, n), lambda b, i: (0, 0))],
        out_specs=pl.BlockSpec((1, tm, n), lambda b, i: (b, i, 0)),
        out_shape=jax.ShapeDtypeStruct((B, L, n), out_dtype),
        compiler_params=_cparams("parallel", "parallel"),
        name="norm_mod_matmul",
    )(x, g.reshape(1, D), shift, scale, w_bf16, bias.reshape(1, n))


def _rgm_kernel(a_ref, w_ref, b_ref, gate_ref, res_ref, fg_ref, o_ref, *, final_norm):
    acc = _dot(a_ref[0], w_ref[...]) + b_ref[...]
    x = res_ref[0] + gate_ref[0] * acc
    if final_norm:
        x = (x * lax.rsqrt(jnp.mean(x * x, axis=-1, keepdims=True) + EPS)) * fg_ref[...]
    o_ref[0] = x


def res_gate_matmul(a_bf16, w_bf16, bias, gate, res, final_g=None, tm=512):
    B, L, K = a_bf16.shape
    D = w_bf16.shape[1]
    tm = _row_tile(L, tm)
    final_norm = final_g is not None
    fg = final_g if final_norm else jnp.ones((D,), F32)
    return pl.pallas_call(
        functools.partial(_rgm_kernel, final_norm=final_norm),
        grid=(B, L // tm),
        in_specs=[pl.BlockSpec((1, tm, K), lambda b, i: (b, i, 0)),
                  pl.BlockSpec((K, D), lambda b, i: (0, 0)),
                  pl.BlockSpec((1, D), lambda b, i: (0, 0)),
                  pl.BlockSpec((1, 1, D), lambda b, i: (b, 0, 0)),
                  pl.BlockSpec((1, tm, D), lambda b, i: (b, i, 0)),
                  pl.BlockSpec((1, D), lambda b, i: (0, 0))],
        out_specs=pl.BlockSpec((1, tm, D), lambda b, i: (b, i, 0)),
        out_shape=jax.ShapeDtypeStruct((B, L, D), F32),
        compiler_params=_cparams("parallel", "parallel"),
        name="res_gate_matmul",
    )(a_bf16, w_bf16, bias.reshape(1, D), gate, res, fg.reshape(1, D))


def _halo_specs(tm, L, C, cmap):
    r = HALO_ROWS
    nb = L // r
    prev = pl.BlockSpec((1, r, C), lambda b, i, *a: (b, jnp.maximum(i * (tm // r) - 1, 0), cmap(*a)))
    nxt = pl.BlockSpec((1, r, C), lambda b, i, *a: (b, jnp.minimum((i + 1) * (tm // r), nb - 1), cmap(*a)))
    return prev, nxt


def _conv3_rows(x, prev_ref, next_ref, w_ref, b_ref):
    tm = x.shape[0]
    i = pl.program_id(1)
    last = pl.num_programs(1) - 1
    prev_row = jnp.where(i > 0, prev_ref[0].astype(F32)[HALO_ROWS - 1:HALO_ROWS, :], 0.0)
    next_row = jnp.where(i < last, next_ref[0].astype(F32)[0:1, :], 0.0)
    row = lax.broadcasted_iota(jnp.int32, x.shape, 0)
    xm1 = jnp.where(row == 0, prev_row, pltpu.roll(x, 1, 0))
    xp1 = jnp.where(row == tm - 1, next_row, pltpu.roll(x, tm - 1, 0))
    return w_ref[0:1, :] * xm1 + w_ref[1:2, :] * x + w_ref[2:3, :] * xp1 + b_ref[...]


def _store_rows(o_ref, val, perm):
    if not perm:
        o_ref[0] = val
        return
    n2 = o_ref.shape[1]
    for j in range(o_ref.shape[2]):
        o_ref[0, :, j, :] = val[j * n2:(j + 1) * n2]


def _hy_gate_kernel(*refs, perm):
    parts = [refs[5 * k:5 * k + 5] for k in range(3)]
    x0_ref, z_ref = refs[15:]
    conv = [_conv3_rows(p[0][0].astype(F32), p[1], p[2], p[3], p[4]) for p in parts]
    _store_rows(x0_ref, conv[0], perm)
    _store_rows(z_ref, conv[1] * conv[2], perm)


def hy_gate(proj, sc_w, sc_b, perm_n2=None, cl=256):
    B, L, C3 = proj.shape
    D = C3 // 3
    cl = min(cl, D)
    nc = D // cl
    perm = perm_n2 is not None
    tm = V7X_SUBLANES * perm_n2 if perm else L
    assert L % tm == 0
    in_specs, args = [], []
    for k in range(3):
        cmap = lambda c, k=k: k * nc + c
        prev, nxt = _halo_specs(tm, L, cl, cmap)
        in_specs += [pl.BlockSpec((1, tm, cl), lambda b, i, c, k=k: (b, i, k * nc + c)), prev, nxt,
                     pl.BlockSpec((3, cl), lambda b, i, c, k=k: (0, k * nc + c)),
                     pl.BlockSpec((1, cl), lambda b, i, c, k=k: (0, k * nc + c))]
        args += [proj, proj, proj, sc_w, sc_b.reshape(1, C3)]
    if perm:
        out = jax.ShapeDtypeStruct((B, perm_n2, L // perm_n2, D), F32)
        ospec = pl.BlockSpec((1, perm_n2, V7X_SUBLANES, cl), lambda b, i, c: (b, 0, i, c))
    else:
        out = jax.ShapeDtypeStruct((B, L, D), F32)
        ospec = pl.BlockSpec((1, tm, cl), lambda b, i, c: (b, i, c))
    return pl.pallas_call(
        functools.partial(_hy_gate_kernel, perm=perm),
        grid=(B, L // tm, nc),
        in_specs=in_specs,
        out_specs=(ospec, ospec),
        out_shape=(out, out),
        compiler_params=_cparams("parallel", "arbitrary", "arbitrary"),
        name="hy_gate",
    )(*args)


def _filter_kernel(w1_ref, b1_ref, w2_ref, b2_ref, w3_ref, b3_ref, wo_ref, fr_ref, band_ref,
                   h_ref, s_ref, *, L, D, tl, perm):
    i = pl.program_id(0)
    rowi = lax.broadcasted_iota(jnp.int32, (tl, 1), 0) + i * tl
    row = rowi.astype(F32)
    t = row / (L - 1.0)
    w = (2.0 * math.pi) * row / L
    col = lax.broadcasted_iota(jnp.int32, (tl, V7X_LANES), 1)
    ang = w * band_ref[...]
    pos = jnp.where(col == 0, t,
                    jnp.where(col <= HY_EMB_BANDS, jnp.cos(ang),
                              jnp.where(col <= 2 * HY_EMB_BANDS, -jnp.sin(ang), 0.0)))
    fr = fr_ref[...]
    h = jnp.sin(fr * (_dot(pos, w1_ref[...], HIGHEST) + b1_ref[...]))
    h = jnp.sin(fr * (_dot(h, w2_ref[...], HIGHEST) + b2_ref[...]))
    h = jnp.sin(fr * (_dot(h, w3_ref[...], HIGHEST) + b3_ref[...]))
    h = _dot(h, wo_ref[...], HIGHEST)
    dcol = lax.broadcasted_iota(jnp.int32, (1, 2 * D), 1)
    chan = jnp.where(dcol >= D, dcol - D, dcol).astype(F32)
    deltas = HY_MIN_DECAY + chan * ((HY_MAX_DECAY - HY_MIN_DECAY) / (D - 1.0))
    h = h * jnp.exp(-t * jnp.abs(deltas))
    h = jnp.where((rowi == 0) & (dcol >= D), 0.0, h)
    if perm:
        _store_rows(h_ref, h, True)
    else:
        h_ref[...] = h
    part = jnp.sum(jnp.abs(h), axis=0, keepdims=True)

    @pl.when(i == 0)
    def _():
        s_ref[...] = part

    @pl.when(i > 0)
    def _():
        s_ref[...] += part


def hyena_filter(L, f_w1, f_b1, f_w2, f_b2, f_w3, f_b3, f_wout, freq, perm_n2=None):
    W = f_w2.shape[0]
    D2 = f_wout.shape[1]
    P = V7X_LANES
    pad2 = lambda a: jnp.pad(a, ((0, P - a.shape[0]), (0, P - a.shape[1])))
    padv = lambda a: jnp.pad(a.reshape(1, -1), ((0, 0), (0, P - a.shape[0])))
    w1 = pad2(f_w1)
    w2 = pad2(f_w2)
    w3 = pad2(f_w3)
    wo = jnp.pad(f_wout, ((0, P - W), (0, 0)))
    bands = jnp.linspace(1e-4, HY_EMB_BANDS - 1, HY_EMB_BANDS, dtype=F32)
    band_row = jnp.concatenate([jnp.zeros((1,), F32), bands, bands,
                                jnp.zeros((P - 1 - 2 * HY_EMB_BANDS,), F32)]).reshape(1, P)
    perm = perm_n2 is not None
    tl = V7X_SUBLANES * perm_n2 if perm else _row_tile(L, 512)
    assert L % tl == 0
    full = lambda shape: pl.BlockSpec(shape, lambda i: (0, 0))
    if perm:
        hshape = jax.ShapeDtypeStruct((1, perm_n2, L // perm_n2, D2), F32)
        hspec = pl.BlockSpec((1, perm_n2, V7X_SUBLANES, D2), lambda i: (0, 0, i, 0))
    else:
        hshape = jax.ShapeDtypeStruct((L, D2), F32)
        hspec = pl.BlockSpec((tl, D2), lambda i: (i, 0))
    return pl.pallas_call(
        functools.partial(_filter_kernel, L=L, D=D2 // 2, tl=tl, perm=perm),
        grid=(L // tl,),
        in_specs=[full((P, P)), full((1, P)), full((P, P)), full((1, P)), full((P, P)), full((1, P)),
                  full((P, D2)), full((1, P)), full((1, P))],
        out_specs=(hspec, pl.BlockSpec((1, D2), lambda i: (0, 0))),
        out_shape=(hshape, jax.ShapeDtypeStruct((1, D2), F32)),
        compiler_params=_cparams("arbitrary"),
        name="hyena_filter",
    )(w1, padv(f_b1), w2, padv(f_b2), w3, padv(f_b3), wo, padv(freq), band_row)


DFT_DTYPE = BF16
K1_GROUP = V7X_SUBLANES


def _cos_sin(p, n):
    ang = (2.0 * math.pi / n) * p.astype(F32)
    return jnp.cos(ang), jnp.sin(ang)


def _dft_dot(g, x):
    return _dot(g.astype(DFT_DTYPE), x.astype(DFT_DTYPE), HIGHEST if DFT_DTYPE == F32 else None)


def _dft_tables(N1, N2):
    N = N1 * N2
    ar = lambda n: jnp.arange(n, dtype=jnp.int32)
    c, s = _cos_sin((ar(N1)[:, None] * ar(N1 // 2)[None, :]) % N1, N1)
    g1 = jnp.concatenate([c, -s], axis=0)
    k1 = ar(N1)[:, None, None]
    a = ar(N2)[None, :, None]
    b = ar(N2)[None, None, :]
    c, s = _cos_sin((b * (k1 + N1 * a)) % N, N)
    g2 = jnp.concatenate([jnp.concatenate([c, s], axis=2), jnp.concatenate([-s, c], axis=2)], axis=1)
    c, s = _cos_sin((a * (k1 + N1 * b)) % N, N)
    g2i = jnp.concatenate([jnp.concatenate([c, -s], axis=2), jnp.concatenate([s, c], axis=2)], axis=1)
    c, s = _cos_sin((ar(N1 // 2)[:, None] * ar(N1)[None, :]) % N1, N1)
    g4 = jnp.concatenate([c, -s], axis=1) * (1.0 / N)
    return tuple(t.astype(DFT_DTYPE) for t in (g1, g2, g2i, g4))


def _stage1_kernel(g_ref, x_ref, o_ref):
    for s in range(x_ref.shape[1]):
        o_ref[0, :, s, :] = _dft_dot(g_ref[...], x_ref[0, s])


def dft_stage1(g, xp, ns):
    B, N2, K, C = xp.shape
    M = g.shape[0]
    return pl.pallas_call(
        _stage1_kernel,
        grid=(B, N2 // ns),
        in_specs=[pl.BlockSpec((M, K), lambda b, j: (0, 0)),
                  pl.BlockSpec((1, ns, K, C), lambda b, j: (b, j, 0, 0))],
        out_specs=pl.BlockSpec((1, M, ns, C), lambda b, j: (b, 0, j, 0)),
        out_shape=jax.ShapeDtypeStruct((B, M, N2, C), F32),
        compiler_params=_cparams("parallel", "parallel"),
        name="dft_stage1",
    )(g, xp)


def _combine_spectrum(X, nrm_ref, D):
    n = X.shape[0] // 2
    nrm = nrm_ref[:, :D] + nrm_ref[:, D:]
    hre = (X[:n, :D] + X[:n, D:]) / nrm
    him = (X[n:, :D] - X[n:, D:]) / nrm
    return hre, him


def _filter_spec_kernel(g_ref, a_ref, nrm_ref, o_ref):
    _, _, n2, D2 = a_ref.shape
    slab = a_ref[:, 0].reshape(2 * n2, D2)
    X = _dft_dot(g_ref[0], slab)
    hre, him = _combine_spectrum(X, nrm_ref, D2 // 2)
    o_ref[0, 0] = hre
    o_ref[1, 0] = him


def filter_spectrum(g2, a, nrm):
    _, N1, N2, D2 = a.shape
    D = D2 // 2
    return pl.pallas_call(
        _filter_spec_kernel,
        grid=(N1,),
        in_specs=[pl.BlockSpec((1, 2 * N2, 2 * N2), lambda k: (k, 0, 0)),
                  pl.BlockSpec((2, 1, N2, D2), lambda k: (0, k, 0, 0)),
                  pl.BlockSpec((1, D2), lambda k: (0, 0))],
        out_specs=pl.BlockSpec((2, 1, N2, D), lambda k: (0, k, 0, 0)),
        out_shape=jax.ShapeDtypeStruct((2, N1, N2, D), F32),
        compiler_params=_cparams("parallel"),
        name="filter_spectrum",
    )(g2, a, nrm)


def _cmul(xr, xi, hr, hi):
    return xr * hr - xi * hi, xr * hi + xi * hr


def _spec_mul_kernel(g_ref, gi_ref, h_ref, a_ref, o_ref):
    n2, C = a_ref.shape[-2:]
    for j in range(a_ref.shape[2]):
        slab = a_ref[0, :, j].reshape(2 * n2, C)
        X = _dft_dot(g_ref[j], slab)
        pr, pi = _cmul(X[:n2], X[n2:], h_ref[0, j], h_ref[1, j])
        Y = _dft_dot(gi_ref[j], jnp.concatenate([pr, pi], axis=0))
        o_ref[0, :, 0, j, :] = Y[:n2]
        o_ref[0, :, 1, j, :] = Y[n2:]


def spectrum_multiply(g2, g2i, H, a, cl):
    B, _, N1, N2, C = a.shape
    kg = K1_GROUP
    gsp = pl.BlockSpec((kg, 2 * N2, 2 * N2), lambda k, c, b: (k, 0, 0))
    return pl.pallas_call(
        _spec_mul_kernel,
        grid=(N1 // kg, C // cl, B),
        in_specs=[gsp, gsp, pl.BlockSpec((2, kg, N2, cl), lambda k, c, b: (0, k, 0, c)),
                  pl.BlockSpec((1, 2, kg, N2, cl), lambda k, c, b: (b, 0, k, 0, c))],
        out_specs=pl.BlockSpec((1, N2, 2, kg, cl), lambda k, c, b: (b, 0, 0, k, c)),
        out_shape=jax.ShapeDtypeStruct((B, N2, 2, N1, C), F32),
        compiler_params=_cparams("parallel", "parallel", "parallel"),
        name="spectrum_multiply",
    )(g2, g2i, H, a)


def _idft_out_kernel(g_ref, y_ref, x0_ref, z_ref, bias_ref, o_ref, acc_ref):
    for s in range(y_ref.shape[1]):
        y = _dft_dot(g_ref[...], y_ref[0, s])
        acc_ref[:, s, :] = x0_ref[0, s] * (y + z_ref[0, s] * bias_ref[...])
    o_ref[0] = acc_ref[...].astype(o_ref.dtype)


def idft_gate_out(g4, yv, x0p, zp, bias, ns, cl):
    B, N2, K, C = yv.shape
    M = g4.shape[0]
    pdat = pl.BlockSpec((1, ns, M, cl), lambda b, j, c: (b, j, 0, c))
    return pl.pallas_call(
        _idft_out_kernel,
        grid=(B, N2 // ns, C // cl),
        in_specs=[pl.BlockSpec((M, K), lambda b, j, c: (0, 0)),
                  pl.BlockSpec((1, ns, K, cl), lambda b, j, c: (b, j, 0, c)),
                  pdat, pdat, pl.BlockSpec((1, cl), lambda b, j, c: (0, c))],
        out_specs=pl.BlockSpec((1, M, ns, cl), lambda b, j, c: (b, 0, j, c)),
        out_shape=jax.ShapeDtypeStruct((B, M, N2, C), BF16),
        scratch_shapes=[pltpu.VMEM((M, ns, cl), F32)],
        compiler_params=_cparams("parallel", "parallel", "parallel"),
        name="idft_gate_out",
    )(g4, yv, x0p, zp, bias.reshape(1, C))


def long_conv_two_stage(x0p, zp, hfbp, nrm, bias):
    B, N2, M, C = zp.shape
    L = N2 * M
    N1 = 2 * M
    g1, g2, g2i, g4 = _dft_tables(N1, N2)
    ns = V7X_SUBLANES
    cl = min(C, 512)
    a_f = dft_stage1(g1, hfbp, ns)
    H = filter_spectrum(g2, a_f.reshape(2, N1, N2, 2 * C), nrm)
    a = dft_stage1(g1, zp, ns)
    yv = spectrum_multiply(g2, g2i, H, a.reshape(B, 2, N1, N2, C), cl)
    out = idft_gate_out(g4, yv.reshape(B, N2, 2 * N1, C), x0p, zp, bias, 2 * ns, cl)
    return out.reshape(B, L, C)


def _dense_tables(L):
    N = 2 * L
    ar = lambda n: jnp.arange(n, dtype=jnp.int32)
    c, s = _cos_sin((ar(N)[:, None] * ar(L)[None, :]) % N, N)
    g = jnp.concatenate([c, -s], axis=0)
    c, s = _cos_sin((ar(L)[:, None] * ar(N)[None, :]) % N, N)
    gi = jnp.concatenate([c, -s], axis=1) * (1.0 / N)
    return g.astype(DFT_DTYPE), gi.astype(DFT_DTYPE)


def _dense_spec_kernel(g_ref, hfb_ref, nrm_ref, o_ref):
    X = _dft_dot(g_ref[...], hfb_ref[...])
    hre, him = _combine_spectrum(X, nrm_ref, o_ref.shape[-1])
    o_ref[0] = hre
    o_ref[1] = him


def _dense_conv_kernel(g_ref, gi_ref, h_ref, z_ref, x0_ref, bias_ref, o_ref):
    z = z_ref[0]
    X = _dft_dot(g_ref[...], z)
    n = X.shape[0] // 2
    pr, pi = _cmul(X[:n], X[n:], h_ref[0], h_ref[1])
    y = _dft_dot(gi_ref[...], jnp.concatenate([pr, pi], axis=0))
    o_ref[0] = (x0_ref[0] * (y + z * bias_ref[...])).astype(o_ref.dtype)


def long_conv_dense(x0, z, hfb, nrm, bias):
    B, L, C = z.shape
    N = 2 * L
    g, gi = _dense_tables(L)
    full2 = lambda shape: pl.BlockSpec(shape, lambda *_: (0,) * len(shape))
    H = pl.pallas_call(
        _dense_spec_kernel,
        grid=(1,),
        in_specs=[full2((2 * N, L)), full2((L, 2 * C)), full2((1, 2 * C))],
        out_specs=full2((2, N, C)),
        out_shape=jax.ShapeDtypeStruct((2, N, C), F32),
        compiler_params=_cparams("arbitrary"),
        name="dense_filter_spectrum",
    )(g, hfb, nrm)
    dat = pl.BlockSpec((1, L, C), lambda b: (b, 0, 0))
    return pl.pallas_call(
        _dense_conv_kernel,
        grid=(B,),
        in_specs=[full2((2 * N, L)), full2((L, 2 * N)), full2((2, N, C)), dat, dat, full2((1, C))],
        out_specs=dat,
        out_shape=jax.ShapeDtypeStruct((B, L, C), BF16),
        compiler_params=_cparams("parallel"),
        name="dense_long_conv",
    )(g, gi, H, z, x0, bias.reshape(1, C))


def hyena_mix_pre(u_args, p, dense):
    x, g, shift, scale = u_args
    (w_in, b_in, sc_w, sc_b, f_w1, f_b1, f_w2, f_b2, f_w3, f_b3, f_wout, freq, bias) = p
    L = x.shape[1]
    perm_n2 = None if dense else V7X_LANES
    proj = norm_mod_matmul(x, g, shift, scale, w_in.astype(BF16), b_in, tm=512)
    x0, z = hy_gate(proj, sc_w, sc_b, perm_n2)
    hfb, nrm = hyena_filter(L, f_w1, f_b1, f_w2, f_b2, f_w3, f_b3, f_wout, freq, perm_n2)
    conv = long_conv_dense if dense else long_conv_two_stage
    return conv(x0, z, hfb, nrm, bias)


def _ml_prep_kernel(xm_ref, prev_ref, next_ref, cw_ref, cb_ref, wq_ref, wk_ref, wv_ref, wg_ref, bg_ref,
                    q_ref, k_ref, v_ref, xc_ref, g_ref, *, k_scale):
    xm = xm_ref[0].astype(F32)
    inner = xm.shape[1]
    conv = _conv3_rows(xm, prev_ref, next_ref, cw_ref, cb_ref)
    xc = conv * jax.nn.sigmoid(conv)
    xc_ref[0] = xc
    xcb = xc.astype(BF16)
    xmb = xm.astype(BF16)
    gw = V7X_MXU_DIM
    gates = bg_ref[...]
    for j in range(inner // gw):
        sl = slice(j * gw, (j + 1) * gw)
        q = _dot(xcb[:, sl], wq_ref[j])
        k = _dot(xcb[:, sl], wk_ref[j])
        v = _dot(xmb[:, sl], wv_ref[j])
        q_ref[0, :, sl] = q.astype(BF16)
        k_ref[0, :, sl] = (k * k_scale).astype(BF16)
        v_ref[0, :, sl] = v.astype(BF16)
        gates = gates + _dot(q.astype(BF16), wg_ref[j * gw:(j + 1) * gw, :])
        gates = gates + _dot(k.astype(BF16), wg_ref[inner + j * gw:inner + (j + 1) * gw, :])
        gates = gates + _dot(v.astype(BF16), wg_ref[2 * inner + j * gw:2 * inner + (j + 1) * gw, :])
    g_ref[0] = gates


def _block_diag(w, group):
    nb, bs, _ = w.shape
    per = group // bs
    w = w.reshape(nb // per, per, bs, bs)
    eye = jnp.eye(per, dtype=w.dtype)
    dense = jnp.einsum("gpce,pr->gpcre", w, eye)
    return dense.reshape(nb // per, group, group).astype(BF16)


def ml_prep(xz, conv_w, conv_b, wq, wk, wv, w_gate, b_gate, tm=256):
    B, L, C2 = xz.shape
    inner = C2 // 2
    dh = inner // ML_HEADS
    tm = _row_tile(L, tm)
    gw = V7X_MXU_DIM
    ng = inner // gw
    P = V7X_LANES
    wg = jnp.pad(w_gate, ((0, 0), (0, P - w_gate.shape[1]))).astype(BF16)
    bg = jnp.pad(b_gate.reshape(1, -1), ((0, 0), (0, P - b_gate.shape[0])))
    prev, nxt = _halo_specs(tm, L, inner, lambda: 0)
    c2 = lambda shape: pl.BlockSpec(shape, lambda b, i: (0,) * len(shape))
    row = lambda n: pl.BlockSpec((1, tm, n), lambda b, i: (b, i, 0))
    sd = lambda n, dt: jax.ShapeDtypeStruct((B, L, n), dt)
    return pl.pallas_call(
        functools.partial(_ml_prep_kernel, k_scale=dh ** -0.5),
        grid=(B, L // tm),
        in_specs=[row(inner), prev, nxt, c2((3, inner)), c2((1, inner)),
                  c2((ng, gw, gw)), c2((ng, gw, gw)), c2((ng, gw, gw)), c2((3 * inner, P)), c2((1, P))],
        out_specs=(row(inner), row(inner), row(inner), row(inner), row(P)),
        out_shape=(sd(inner, BF16), sd(inner, BF16), sd(inner, BF16), sd(inner, F32), sd(P, F32)),
        compiler_params=_cparams("parallel", "arbitrary"),
        name="ml_prep",
    )(xz, xz, xz, conv_w, conv_b.reshape(1, inner), _block_diag(wq, gw), _block_diag(wk, gw),
      _block_diag(wv, gw), wg, bg)


def _gates_scan_order(g_ctx, g_lat):
    T = ML_CHUNK

    def lay(g):
        B, L, _ = g.shape
        g = g[..., :4 * ML_HEADS].reshape(B, L // T, T, 2, 2, ML_HEADS)
        return jnp.transpose(g, (3, 0, 5, 4, 1, 2))

    gc, gl = lay(g_ctx), lay(g_lat)
    fwd = jnp.concatenate([gc[0], gl[0]], axis=3)
    bwd = jnp.concatenate([gc[1][..., ::-1, :], gl[1][..., ::-1, :]], axis=3)
    out = jnp.stack([fwd, bwd])
    nct = out.shape[4]
    pad = -nct % V7X_SUBLANES
    return jnp.pad(out, ((0, 0),) * 4 + ((0, pad), (0, 0))), nct


AUX_BMR, AUX_GR, AUX_WINTER, AUX_EMR, AUX_WROW, AUX_GOLD = range(6)


def _ml_gates_kernel(g_ref, aux_ref, bl_scr, me_scr, mp_scr):
    d = pl.program_id(0)
    ig = g_ref[0, 0, 0, 0]
    fg = g_ref[0, 0, 0, 1]
    nct, T = ig.shape
    lf = jnp.minimum(fg, 0.0) - jnp.log1p(jnp.exp(-jnp.abs(fg)))
    lane = lax.broadcasted_iota(jnp.int32, (nct, T), 1)
    rev = d == 1

    def scans(x, op, ident):
        f, r = x, x
        s = 1
        while s < T:
            f = op(f, jnp.where(lane >= s, pltpu.roll(f, s, 1), ident))
            r = op(r, jnp.where(lane < T - s, pltpu.roll(r, T - s, 1), ident))
            s *= 2
        return jnp.where(rev, r, f)

    bcs = scans(lf, jnp.add, 0.0)
    b_last = jnp.sum(lf, axis=1, keepdims=True)
    gr = ig - bcs
    cmax = scans(gr, jnp.maximum, -jnp.inf)
    max_e = b_last + jnp.max(gr, axis=1, keepdims=True)
    bl_scr[...] = jnp.broadcast_to(b_last, bl_scr.shape)
    me_scr[...] = jnp.broadcast_to(max_e, me_scr.shape)

    def body(t, m):
        mp_scr[pl.ds(t, 1), :] = m
        return jnp.maximum(bl_scr[pl.ds(t, 1), :] + m, me_scr[pl.ds(t, 1), :])

    lax.fori_loop(0, nct, body, jnp.zeros((1, V7X_LANES), F32))
    m_prev = mp_scr[:, 0:1]
    a_inter = bcs + m_prev
    m_row = jnp.maximum(a_inter, bcs + cmax)
    m_new = jnp.maximum(b_last + m_prev, max_e)
    rows = {AUX_BMR: bcs - m_row, AUX_GR: gr, AUX_WINTER: jnp.exp(a_inter - m_row),
            AUX_EMR: jnp.exp(-m_row), AUX_WROW: jnp.exp(b_last + gr - m_new),
            AUX_GOLD: jnp.broadcast_to(jnp.exp(b_last + m_prev - m_new), (nct, T))}
    zero = jnp.zeros((nct, T), F32)
    for k in range(V7X_SUBLANES):
        aux_ref[0, 0, 0, :, k, :] = rows.get(k, zero)


def ml_gates(gates):
    _, B, H, _, nct, T = gates.shape
    return pl.pallas_call(
        _ml_gates_kernel,
        grid=(2, B, H),
        in_specs=[pl.BlockSpec((1, 1, 1, 2, nct, T), lambda d, b, h: (d, b, h, 0, 0, 0))],
        out_specs=pl.BlockSpec((1, 1, 1, nct, V7X_SUBLANES, T), lambda d, b, h: (d, b, h, 0, 0, 0)),
        out_shape=jax.ShapeDtypeStruct((2, B, H, nct, V7X_SUBLANES, T), F32),
        scratch_shapes=[pltpu.VMEM((nct, V7X_LANES), F32)] * 3,
        compiler_params=_cparams("arbitrary", "arbitrary", "arbitrary"),
        name="ml_gates",
    )(gates)


def _mlstm_kernel(qf_ref, kf_ref, vf_ref, qb_ref, kb_ref, vb_ref, aux_ref, c0_ref,
                  hf_ref, hb_ref, cf_ref, c_scr):
    t = pl.program_id(2)
    nct = pl.num_programs(2)
    T = ML_CHUNK
    dh = qf_ref.shape[-1]

    @pl.when(t == 0)
    def _():
        c_scr[...] = c0_ref[:, 0, 0]

    row = lax.broadcasted_iota(jnp.int32, (T, T), 0)
    col = lax.broadcasted_iota(jnp.int32, (T, T), 1)
    ones_col = (lax.broadcasted_iota(jnp.int32, (T, V7X_LANES), 1) == 0).astype(F32)
    streams = ((qf_ref, kf_ref, vf_ref, hf_ref, col <= row), (qb_ref, kb_ref, vb_ref, hb_ref, col >= row))
    for d, (q_ref, k_ref, v_ref, h_ref, mask) in enumerate(streams):
        ax = aux_ref[d, 0, 0, 0]
        axt = jnp.concatenate([ax, jnp.zeros((T - V7X_SUBLANES, T), F32)], axis=0).T
        gr = ax[AUX_GR:AUX_GR + 1, :]
        g_old = ax[AUX_GOLD:AUX_GOLD + 1, 0:1]
        bmr = axt[:, AUX_BMR:AUX_BMR + 1]
        w_inter = axt[:, AUX_WINTER:AUX_WINTER + 1]
        emr = axt[:, AUX_EMR:AUX_EMR + 1]
        w_col = axt[:, AUX_WROW:AUX_WROW + 1]
        pmat = jnp.exp(jnp.where(mask, bmr + gr, -jnp.inf))

        q = q_ref[0]
        v = v_ref[0]
        kT = k_ref[0].astype(F32).T.astype(BF16)
        smat = _dot(q, kT) * pmat
        r1 = _dot(q, c_scr[d].astype(BF16))
        v_aug = jnp.concatenate([v.astype(F32), ones_col], axis=1)
        r2 = _dot(smat.astype(BF16), v_aug.astype(BF16))
        num = w_inter * r1[:, :dh] + r2[:, :dh]
        den = w_inter * r1[:, dh:dh + 1] + r2[:, dh:dh + 1]
        h_ref[0] = num / jnp.maximum(jnp.abs(den), emr)
        c_scr[d] = g_old * c_scr[d] + _dot(kT, (v_aug * w_col).astype(BF16))

    @pl.when(t == nct - 1)
    def _():
        cf_ref[:, 0, 0] = c_scr[...]


def mlstm_scan(q, k, v, aux, chunk0, c0):
    B, L, inner = q.shape
    H = ML_HEADS
    dh = inner // H
    T = ML_CHUNK
    nc = L // T
    da = dh + V7X_LANES
    fw = pl.BlockSpec((1, T, dh), lambda b, h, t: (b, t, h))
    bw = pl.BlockSpec((1, T, dh), lambda b, h, t: (b, nc - 1 - t, h))
    cst = pl.BlockSpec((2, 1, 1, dh, da), lambda b, h, t: (0, b, h, 0, 0))
    hshape = jax.ShapeDtypeStruct((B, L, inner), F32)
    return pl.pallas_call(
        _mlstm_kernel,
        grid=(B, H, nc),
        in_specs=[fw, fw, fw, bw, bw, bw,
                  pl.BlockSpec((2, 1, 1, 1, V7X_SUBLANES, T), lambda b, h, t: (0, b, h, chunk0 + t, 0, 0)),
                  cst],
        out_specs=(fw, bw, cst),
        out_shape=(hshape, hshape, jax.ShapeDtypeStruct(c0.shape, F32)),
        scratch_shapes=[pltpu.VMEM((2, dh, da), F32)],
        compiler_params=_cparams("parallel", "parallel", "arbitrary"),
        name="mlstm_scan",
    )(q, k, v, q, k, v, aux, c0)


def _ml_out_kernel(hf_ref, hb_ref, xc_ref, z_ref, nw_ref, sk_ref, o_ref):
    h = hf_ref[0] + hb_ref[0]
    dh = h.shape[1] // ML_HEADS
    z = z_ref[0].astype(F32)
    gate = z * jax.nn.sigmoid(z)
    for j in range(ML_HEADS):
        sl = slice(j * dh, (j + 1) * dh)
        seg = h[:, sl]
        mu = jnp.mean(seg, axis=-1, keepdims=True)
        cen = seg - mu
        var = jnp.mean(cen * cen, axis=-1, keepdims=True)
        hn = cen * lax.rsqrt(var + ML_NORM_EPS)
        hs = hn * nw_ref[:, sl] + sk_ref[:, sl] * xc_ref[0, :, sl]
        o_ref[0, :, sl] = (hs * gate[:, sl]).astype(o_ref.dtype)


def ml_out(hf, hb, xc, xz, norm_w, skip, tm=256):
    B, L, inner = hf.shape
    tm = _row_tile(L, tm)
    vec = pl.BlockSpec((1, inner), lambda b, i: (0, 0))
    dat = pl.BlockSpec((1, tm, inner), lambda b, i: (b, i, 0))
    return pl.pallas_call(
        _ml_out_kernel,
        grid=(B, L // tm),
        in_specs=[dat, dat, dat, pl.BlockSpec((1, tm, inner), lambda b, i: (b, i, 1)), vec, vec],
        out_specs=dat,
        out_shape=jax.ShapeDtypeStruct((B, L, inner), BF16),
        compiler_params=_cparams("parallel", "parallel"),
        name="ml_out",
    )(hf, hb, xc, xz, norm_w.reshape(1, inner), skip.reshape(1, inner))


def mlstm_mix_pre(lat_args, ctx_args, p):
    w_in, conv_w, conv_b, wq, wk, wv, w_gate, b_gate, norm_w, skip = p
    w_in_b = w_in.astype(BF16)
    zero_b = jnp.zeros((w_in.shape[1],), F32)

    def prep(args):
        x, g, shift, scale = args
        xz = norm_mod_matmul(x, g, shift, scale, w_in_b, zero_b, tm=512)
        return ml_prep(xz, conv_w, conv_b, wq, wk, wv, w_gate, b_gate) + (xz,)

    qc, kc, vc, _, gates_c, _ = prep(ctx_args)
    q, k, v, xc, gates_l, xz = prep(lat_args)
    B, _, inner = q.shape
    dh = inner // ML_HEADS
    gates, _ = _gates_scan_order(gates_c, gates_l)
    aux = ml_gates(gates)
    c0 = jnp.zeros((2, B, ML_HEADS, dh, dh + V7X_LANES), F32)
    _, _, c1 = mlstm_scan(qc, kc, vc, aux, 0, c0)
    hf, hb, _ = mlstm_scan(q, k, v, aux, qc.shape[1] // ML_CHUNK, c1)
    return ml_out(hf, hb, xc, xz, norm_w, skip)


def _ffn_conv_kernel(a_ref, g_ref, prev_ref, next_ref, w_ref, b_ref, o_ref, *, cols, vertical):
    tm = g_ref.shape[1]
    i = pl.program_id(1)
    last = pl.num_programs(1) - 1
    g = g_ref[0].astype(F32)
    if vertical:
        top = jnp.where(i > 0, prev_ref[0].astype(F32), 0.0)
        bot = jnp.where(i < last, next_ref[0].astype(F32), 0.0)
        g = jnp.concatenate([top, g, bot], axis=0)
    R = g.shape[0]
    cpos = jnp.bitwise_and(lax.broadcasted_iota(jnp.int32, (R, 1), 0), cols - 1)
    left = jnp.where(cpos == 0, 0.0, pltpu.roll(g, 1, 0))
    right = jnp.where(cpos == cols - 1, 0.0, pltpu.roll(g, R - 1, 0))
    acc = b_ref[...]
    for dr in (range(3) if vertical else (1,)):
        off = dr * cols if vertical else 0
        sl = slice(off, off + tm)
        acc = acc + (w_ref[3 * dr:3 * dr + 1, :] * left[sl] + w_ref[3 * dr + 1:3 * dr + 2, :] * g[sl]
                     + w_ref[3 * dr + 2:3 * dr + 3, :] * right[sl])
    o_ref[0] = ((acc * jax.nn.sigmoid(acc)) * a_ref[0].astype(F32)).astype(o_ref.dtype)


def ffn_conv_act(ag, conv_w, conv_b, rows, cols, tm=1024):
    B, L, F2 = ag.shape
    F = F2 // 2
    assert cols & (cols - 1) == 0 and rows * cols == L
    vertical = rows > 1
    tm = _row_tile(L, tm) if vertical else L
    cb = V7X_MXU_DIM
    nf = F // cb
    assert F % cb == 0 and tm % cols == 0
    hb = cols if vertical else HALO_ROWS
    nhb = L // hb
    return pl.pallas_call(
        functools.partial(_ffn_conv_kernel, cols=cols, vertical=vertical),
        grid=(B, L // tm, nf),
        in_specs=[pl.BlockSpec((1, tm, cb), lambda b, i, j: (b, i, j)),
                  pl.BlockSpec((1, tm, cb), lambda b, i, j: (b, i, nf + j)),
                  pl.BlockSpec((1, hb, cb), lambda b, i, j: (b, jnp.maximum(i * (tm // hb) - 1, 0), nf + j)),
                  pl.BlockSpec((1, hb, cb), lambda b, i, j: (b, jnp.minimum((i + 1) * (tm // hb), nhb - 1), nf + j)),
                  pl.BlockSpec((9, cb), lambda b, i, j: (0, j)),
                  pl.BlockSpec((1, cb), lambda b, i, j: (0, j))],
        out_specs=pl.BlockSpec((1, tm, cb), lambda b, i, j: (b, i, j)),
        out_shape=jax.ShapeDtypeStruct((B, L, F), BF16),
        compiler_params=_cparams("parallel", "arbitrary", "parallel"),
        name="ffn_conv_act",
    )(ag, ag, ag, ag, conv_w.reshape(9, F), conv_b.reshape(1, F))


def conv_ffn_residual(x, g, shift, scale, gate, w_up_b, conv_w, conv_b, w_down_b, rows, cols, final_g=None):
    ag = norm_mod_matmul(x, g, shift, scale, w_up_b, jnp.zeros((w_up_b.shape[1],), F32), tm=512)
    act = ffn_conv_act(ag, conv_w, conv_b, rows, cols)
    return res_gate_matmul(act, w_down_b, jnp.zeros((x.shape[-1],), F32), gate, x, final_g)


def kernel(x, c, ctx, c_ctx, mod_w, mod_b, norm_g, final_g, hy_w_in, hy_b_in, hy_sc_w, hy_sc_b, hy_f_w1, hy_f_b1, hy_f_w2, hy_f_b2, hy_f_w3, hy_f_b3, hy_f_wout, hy_freq, hy_bias, hy_w_out, hy_b_out, ml_w_in, ml_conv_w, ml_conv_b, ml_wq, ml_wk, ml_wv, ml_w_gate, ml_b_gate, ml_norm_w, ml_skip, ml_w_down, ffn_w_up, ffn_conv_w, ffn_conv_b, ffn_w_down):
    B, L, D = x.shape
    ctx_len = ctx.shape[1]
    depth = mod_w.shape[0]
    n_mixers = 2
    rows = L // GRID_W
    hy_params = (hy_w_in, hy_b_in, hy_sc_w, hy_sc_b, hy_f_w1, hy_f_b1, hy_f_w2, hy_f_b2,
                 hy_f_w3, hy_f_b3, hy_f_wout, hy_freq, hy_bias)
    ml_params = (ml_w_in, ml_conv_w, ml_conv_b, ml_wq, ml_wk, ml_wv, ml_w_gate, ml_b_gate,
                 ml_norm_w, ml_skip)
    cond = jnp.concatenate([c, c_ctx.reshape(1, D), jnp.zeros((V7X_SUBLANES - B - 1, D), F32)], axis=0)
    for i in range(depth):
        last = i == depth - 1
        mod = adaln(cond, mod_w[i], mod_b[i])
        lat = [mod[:B, k * D:(k + 1) * D].reshape(B, 1, D) for k in range(6)]
        cm = [jnp.broadcast_to(mod[B:B + 1, k * D:(k + 1) * D].reshape(1, 1, D), (B, 1, D)) for k in range(6)]
        lat_args = (x, norm_g[i, 0], lat[0], lat[1])
        ctx_args = (ctx, norm_g[i, 0], cm[0], cm[1])
        j = i // n_mixers
        if i % n_mixers == 0:
            p = tuple(a[j] for a in hy_params)
            w_out_b = hy_w_out[j].astype(BF16)
            x = res_gate_matmul(hyena_mix_pre(lat_args, p, dense=False), w_out_b, hy_b_out[j], lat[2], x)
            if not last:
                ctx = res_gate_matmul(hyena_mix_pre(ctx_args, p, dense=True), w_out_b, hy_b_out[j], cm[2], ctx)
        else:
            assert last, "the mLSTM mixer is only implemented for the last layer (no context output)"
            p = tuple(a[j] for a in ml_params)
            act = mlstm_mix_pre(lat_args, ctx_args, p)
            x = res_gate_matmul(act, ml_w_down[j].astype(BF16), jnp.zeros((D,), F32), lat[2], x)
        w_up_b = ffn_w_up[i].astype(BF16)
        w_down_b = ffn_w_down[i].astype(BF16)
        x = conv_ffn_residual(x, norm_g[i, 1], lat[3], lat[4], lat[5], w_up_b, ffn_conv_w[i], ffn_conv_b[i],
                              w_down_b, rows, GRID_W, final_g if last else None)
        if not last:
            ctx = conv_ffn_residual(ctx, norm_g[i, 1], cm[3], cm[4], cm[5], w_up_b, ffn_conv_w[i],
                                    ffn_conv_b[i], w_down_b, 1, ctx_len)
    return x
```

```python
import functools
import math

import jax
import jax.numpy as jnp
from jax import lax
from jax.experimental import pallas as pl
from jax.experimental.pallas import tpu as pltpu

F32 = jnp.float32
BF16 = jnp.bfloat16
HIGHEST = lax.Precision.HIGHEST

EPS = 1e-6
ML_NORM_EPS = 1e-5
GRID_W = 64
ML_HEADS = 4
ML_CHUNK = 128
ML_QKV_BLOCK = 4
HY_EMB_BANDS = 16
HY_MAX_DECAY = math.log(1e-2) / 0.3
HY_MIN_DECAY = math.log(1e-2) / 1.5

V7X_LANES = 128
V7X_SUBLANES = 8
V7X_MXU_DIM = 256
V7X_VMEM_BYTES = 64 * 1024 * 1024
VMEM_LIMIT = V7X_VMEM_BYTES - 8 * 1024 * 1024
HALO_ROWS = 16


def _cparams(*sem):
    return pltpu.CompilerParams(dimension_semantics=("arbitrary",) * len(sem), vmem_limit_bytes=VMEM_LIMIT)


def _dot(a, b, precision=None):
    return jnp.dot(a, b, preferred_element_type=F32, precision=precision)


def _row_tile(n, target):
    t = min(n, target)
    assert n % t == 0, (n, t)
    return t


def _adaln_kernel(c_ref, w_ref, b_ref, o_ref):
    c = c_ref[...]
    s = c * jax.nn.sigmoid(c)
    o_ref[...] = _dot(s, w_ref[...], HIGHEST) + b_ref[...]


def adaln(cond8, w, b):
    r, d = cond8.shape
    n = w.shape[1]
    tn = _row_tile(n, 1536)
    return pl.pallas_call(
        _adaln_kernel,
        grid=(n // tn,),
        in_specs=[pl.BlockSpec((r, d), lambda j: (0, 0)),
                  pl.BlockSpec((d, tn), lambda j: (0, j)),
                  pl.BlockSpec((1, tn), lambda j: (0, j))],
        out_specs=pl.BlockSpec((r, tn), lambda j: (0, j)),
        out_shape=jax.ShapeDtypeStruct((r, n), F32),
        compiler_params=_cparams("arbitrary"),
        name="adaln",
    )(cond8, w, b.reshape(1, n))


def _nmm_kernel(x_ref, g_ref, sh_ref, sc_ref, w_ref, b_ref, o_ref):
    x = x_ref[0]
    y = x * lax.rsqrt(jnp.mean(x * x, axis=-1, keepdims=True) + EPS)
    u = (y * g_ref[...]) * (1.0 + sc_ref[0]) + sh_ref[0]
    acc = _dot(u.astype(BF16), w_ref[...])
    o_ref[0] = (acc + b_ref[...]).astype(o_ref.dtype)


def norm_mod_matmul(x, g, shift, scale, w_bf16, bias, tm, out_dtype=BF16):
    B, L, D = x.shape
    n = w_bf16.shape[1]
    tm = _row_tile(L, tm)
    return pl.pallas_call(
        _nmm_kernel,
        grid=(B, L // tm),
        in_specs=[pl.BlockSpec((1, tm, D), lambda b, i: (b, i, 0)),
                  pl.BlockSpec((1, D), lambda b, i: (0, 0)),
                  pl.BlockSpec((1, 1, D), lambda b, i: (b, 0, 0)),
                  pl.BlockSpec((1, 1, D), lambda b, i: (b, 0, 0)),
                  pl.BlockSpec((D, n), lambda b, i: (0, 0)),
                  pl.BlockSpec((1, n), lambda b, i: (0, 0))],
        out_specs=pl.BlockSpec((1, tm, n), lambda b, i: (b, i, 0)),
        out_shape=jax.ShapeDtypeStruct((B, L, n), out_dtype),
        compiler_params=_cparams("parallel", "parallel"),
        name="norm_mod_matmul",
    )(x, g.reshape(1, D), shift, scale, w_bf16, bias.reshape(1, n))


def _rgm_kernel(a_ref, w_ref, b_ref, gate_ref, res_ref, fg_ref, o_ref, *, final_norm):
    acc = _dot(a_ref[0], w_ref[...]) + b_ref[...]
    x = res_ref[0] + gate_ref[0] * acc
    if final_norm:
        x = (x * lax.rsqrt(jnp.mean(x * x, axis=-1, keepdims=True) + EPS)) * fg_ref[...]
    o_ref[0] = x


def res_gate_matmul(a_bf16, w_bf16, bias, gate, res, final_g=None, tm=512):
    B, L, K = a_bf16.shape
    D = w_bf16.shape[1]
    tm = _row_tile(L, tm)
    final_norm = final_g is not None
    fg = final_g if final_norm else jnp.ones((D,), F32)
    return pl.pallas_call(
        functools.partial(_rgm_kernel, final_norm=final_norm),
        grid=(B, L // tm),
        in_specs=[pl.BlockSpec((1, tm, K), lambda b, i: (b, i, 0)),
                  pl.BlockSpec((K, D), lambda b, i: (0, 0)),
                  pl.BlockSpec((1, D), lambda b, i: (0, 0)),
                  pl.BlockSpec((1, 1, D), lambda b, i: (b, 0, 0)),
                  pl.BlockSpec((1, tm, D), lambda b, i: (b, i, 0)),
                  pl.BlockSpec((1, D), lambda b, i: (0, 0))],
        out_specs=pl.BlockSpec((1, tm, D), lambda b, i: (b, i, 0)),
        out_shape=jax.ShapeDtypeStruct((B, L, D), F32),
        compiler_params=_cparams("parallel", "parallel"),
        name="res_gate_matmul",
    )(a_bf16, w_bf16, bias.reshape(1, D), gate, res, fg.reshape(1, D))


def _halo_specs(tm, L, C, cmap):
    r = HALO_ROWS
    nb = L // r
    prev = pl.BlockSpec((1, r, C), lambda b, i, *a: (b, jnp.maximum(i * (tm // r) - 1, 0), cmap(*a)))
    nxt = pl.BlockSpec((1, r, C), lambda b, i, *a: (b, jnp.minimum((i + 1) * (tm // r), nb - 1), cmap(*a)))
    return prev, nxt


def _conv3_rows(x, prev_ref, next_ref, w_ref, b_ref):
    tm = x.shape[0]
    i = pl.program_id(1)
    last = pl.num_programs(1) - 1
    prev_row = jnp.where(i > 0, prev_ref[0].astype(F32)[HALO_ROWS - 1:HALO_ROWS, :], 0.0)
    next_row = jnp.where(i < last, next_ref[0].astype(F32)[0:1, :], 0.0)
    row = lax.broadcasted_iota(jnp.int32, x.shape, 0)
    xm1 = jnp.where(row == 0, prev_row, pltpu.roll(x, 1, 0))
    xp1 = jnp.where(row == tm - 1, next_row, pltpu.roll(x, tm - 1, 0))
    return w_ref[0:1, :] * xm1 + w_ref[1:2, :] * x + w_ref[2:3, :] * xp1 + b_ref[...]


def _store_rows(o_ref, val, perm):
    if not perm:
        o_ref[0] = val
        return
    n2 = o_ref.shape[1]
    for j in range(o_ref.shape[2]):
        o_ref[0, :, j, :] = val[j * n2:(j + 1) * n2]


def _hy_gate_kernel(*refs, perm):
    parts = [refs[5 * k:5 * k + 5] for k in range(3)]
    x0_ref, z_ref = refs[15:]
    conv = [_conv3_rows(p[0][0].astype(F32), p[1], p[2], p[3], p[4]) for p in parts]
    _store_rows(x0_ref, conv[0], perm)
    _store_rows(z_ref, conv[1] * conv[2], perm)


def hy_gate(proj, sc_w, sc_b, perm_n2=None, cl=256):
    B, L, C3 = proj.shape
    D = C3 // 3
    cl = min(cl, D)
    nc = D // cl
    perm = perm_n2 is not None
    tm = V7X_SUBLANES * perm_n2 if perm else L
    assert L % tm == 0
    in_specs, args = [], []
    for k in range(3):
        cmap = lambda c, k=k: k * nc + c
        prev, nxt = _halo_specs(tm, L, cl, cmap)
        in_specs += [pl.BlockSpec((1, tm, cl), lambda b, i, c, k=k: (b, i, k * nc + c)), prev, nxt,
                     pl.BlockSpec((3, cl), lambda b, i, c, k=k: (0, k * nc + c)),
                     pl.BlockSpec((1, cl), lambda b, i, c, k=k: (0, k * nc + c))]
        args += [proj, proj, proj, sc_w, sc_b.reshape(1, C3)]
    if perm:
        out = jax.ShapeDtypeStruct((B, perm_n2, L // perm_n2, D), F32)
        ospec = pl.BlockSpec((1, perm_n2, V7X_SUBLANES, cl), lambda b, i, c: (b, 0, i, c))
    else:
        out = jax.ShapeDtypeStruct((B, L, D), F32)
        ospec = pl.BlockSpec((1, tm, cl), lambda b, i, c: (b, i, c))
    return pl.pallas_call(
        functools.partial(_hy_gate_kernel, perm=perm),
        grid=(B, L // tm, nc),
        in_specs=in_specs,
        out_specs=(ospec, ospec),
        out_shape=(out, out),
        compiler_params=_cparams("parallel", "arbitrary", "arbitrary"),
        name="hy_gate",
    )(*args)


def _filter_kernel(w1_ref, b1_ref, w2_ref, b2_ref, w3_ref, b3_ref, wo_ref, fr_ref, band_ref,
                   h_ref, s_ref, *, L, D, tl, perm):
    i = pl.program_id(0)
    rowi = lax.broadcasted_iota(jnp.int32, (tl, 1), 0) + i * tl
    row = rowi.astype(F32)
    t = row / (L - 1.0)
    w = (2.0 * math.pi) * row / L
    col = lax.broadcasted_iota(jnp.int32, (tl, V7X_LANES), 1)
    ang = w * band_ref[...]
    pos = jnp.where(col == 0, t,
                    jnp.where(col <= HY_EMB_BANDS, jnp.cos(ang),
                              jnp.where(col <= 2 * HY_EMB_BANDS, -jnp.sin(ang), 0.0)))
    fr = fr_ref[...]
    h = jnp.sin(fr * (_dot(pos, w1_ref[...], HIGHEST) + b1_ref[...]))
    h = jnp.sin(fr * (_dot(h, w2_ref[...], HIGHEST) + b2_ref[...]))
    h = jnp.sin(fr * (_dot(h, w3_ref[...], HIGHEST) + b3_ref[...]))
    h = _dot(h, wo_ref[...], HIGHEST)
    dcol = lax.broadcasted_iota(jnp.int32, (1, 2 * D), 1)
    chan = jnp.where(dcol >= D, dcol - D, dcol).astype(F32)
    deltas = HY_MIN_DECAY + chan * ((HY_MAX_DECAY - HY_MIN_DECAY) / (D - 1.0))
    h = h * jnp.exp(-t * jnp.abs(deltas))
    h = jnp.where((rowi == 0) & (dcol >= D), 0.0, h)
    if perm:
        _store_rows(h_ref, h, True)
    else:
        h_ref[...] = h
    part = jnp.sum(jnp.abs(h), axis=0, keepdims=True)

    @pl.when(i == 0)
    def _():
        s_ref[...] = part

    @pl.when(i > 0)
    def _():
        s_ref[...] += part


def hyena_filter(L, f_w1, f_b1, f_w2, f_b2, f_w3, f_b3, f_wout, freq, perm_n2=None):
    W = f_w2.shape[0]
    D2 = f_wout.shape[1]
    P = V7X_LANES
    pad2 = lambda a: jnp.pad(a, ((0, P - a.shape[0]), (0, P - a.shape[1])))
    padv = lambda a: jnp.pad(a.reshape(1, -1), ((0, 0), (0, P - a.shape[0])))
    w1 = pad2(f_w1)
    w2 = pad2(f_w2)
    w3 = pad2(f_w3)
    wo = jnp.pad(f_wout, ((0, P - W), (0, 0)))
    bands = jnp.linspace(1e-4, HY_EMB_BANDS - 1, HY_EMB_BANDS, dtype=F32)
    band_row = jnp.concatenate([jnp.zeros((1,), F32), bands, bands,
                                jnp.zeros((P - 1 - 2 * HY_EMB_BANDS,), F32)]).reshape(1, P)
    perm = perm_n2 is not None
    tl = V7X_SUBLANES * perm_n2 if perm else _row_tile(L, 512)
    assert L % tl == 0
    full = lambda shape: pl.BlockSpec(shape, lambda i: (0, 0))
    if perm:
        hshape = jax.ShapeDtypeStruct((1, perm_n2, L // perm_n2, D2), F32)
        hspec = pl.BlockSpec((1, perm_n2, V7X_SUBLANES, D2), lambda i: (0, 0, i, 0))
    else:
        hshape = jax.ShapeDtypeStruct((L, D2), F32)
        hspec = pl.BlockSpec((tl, D2), lambda i: (i, 0))
    return pl.pallas_call(
        functools.partial(_filter_kernel, L=L, D=D2 // 2, tl=tl, perm=perm),
        grid=(L // tl,),
        in_specs=[full((P, P)), full((1, P)), full((P, P)), full((1, P)), full((P, P)), full((1, P)),
                  full((P, D2)), full((1, P)), full((1, P))],
        out_specs=(hspec, pl.BlockSpec((1, D2), lambda i: (0, 0))),
        out_shape=(hshape, jax.ShapeDtypeStruct((1, D2), F32)),
        compiler_params=_cparams("arbitrary"),
        name="hyena_filter",
    )(w1, padv(f_b1), w2, padv(f_b2), w3, padv(f_b3), wo, padv(freq), band_row)


DFT_DTYPE = BF16
K1_GROUP = HALO_ROWS


def _cos_sin(p, n):
    ang = (2.0 * math.pi / n) * p.astype(F32)
    return jnp.cos(ang), jnp.sin(ang)


def _dft_dot(g, x):
    return _dot(g.astype(DFT_DTYPE), x.astype(DFT_DTYPE), HIGHEST if DFT_DTYPE == F32 else None)


def _dft_tables(N1, N2):
    N = N1 * N2
    ar = lambda n: jnp.arange(n, dtype=jnp.int32)
    c, s = _cos_sin((ar(N1)[:, None] * ar(N1 // 2)[None, :]) % N1, N1)
    g1 = jnp.concatenate([c, -s], axis=0)
    k1 = ar(N1)[:, None, None]
    a = ar(N2)[None, :, None]
    b = ar(N2)[None, None, :]
    c, s = _cos_sin((b * (k1 + N1 * a)) % N, N)
    g2 = jnp.concatenate([jnp.concatenate([c, s], axis=2), jnp.concatenate([-s, c], axis=2)], axis=1)
    c, s = _cos_sin((a * (k1 + N1 * b)) % N, N)
    g2i = jnp.concatenate([jnp.concatenate([c, -s], axis=2), jnp.concatenate([s, c], axis=2)], axis=1)
    c, s = _cos_sin((ar(N1 // 2)[:, None] * ar(N1)[None, :]) % N1, N1)
    g4 = jnp.concatenate([c, -s], axis=1) * (1.0 / N)
    return tuple(t.astype(DFT_DTYPE) for t in (g1, g2, g2i, g4))


def _stage1_kernel(g_ref, x_ref, o_ref, acc_ref):
    for s in range(x_ref.shape[1]):
        acc_ref[:, s, :] = _dft_dot(g_ref[...], x_ref[0, s])
    o_ref[0] = acc_ref[...].astype(o_ref.dtype)


def dft_stage1(g, xp, ns, cl):
    B, N2, K, C = xp.shape
    M = g.shape[0]
    return pl.pallas_call(
        _stage1_kernel,
        grid=(B, N2 // ns, C // cl),
        in_specs=[pl.BlockSpec((M, K), lambda b, j, c: (0, 0)),
                  pl.BlockSpec((1, ns, K, cl), lambda b, j, c: (b, j, 0, c))],
        out_specs=pl.BlockSpec((1, M, ns, cl), lambda b, j, c: (b, 0, j, c)),
        out_shape=jax.ShapeDtypeStruct((B, M, N2, C), DFT_DTYPE),
        scratch_shapes=[pltpu.VMEM((M, ns, cl), F32)],
        compiler_params=_cparams("parallel", "parallel", "parallel"),
        name="dft_stage1",
    )(g, xp)


def _combine_spectrum(X, nrm_ref, D):
    n = X.shape[0] // 2
    nrm = nrm_ref[:, :D] + nrm_ref[:, D:]
    hre = (X[:n, :D] + X[:n, D:]) / nrm
    him = (X[n:, :D] - X[n:, D:]) / nrm
    return hre, him


def _filter_spec_kernel(g_ref, a_ref, nrm_ref, o_ref):
    _, _, n2, D2 = a_ref.shape
    slab = a_ref[:, 0].reshape(2 * n2, D2)
    X = _dft_dot(g_ref[0], slab)
    hre, him = _combine_spectrum(X, nrm_ref, D2 // 2)
    o_ref[0, 0] = hre
    o_ref[1, 0] = him


def filter_spectrum(g2, a, nrm):
    _, N1, N2, D2 = a.shape
    D = D2 // 2
    return pl.pallas_call(
        _filter_spec_kernel,
        grid=(N1,),
        in_specs=[pl.BlockSpec((1, 2 * N2, 2 * N2), lambda k: (k, 0, 0)),
                  pl.BlockSpec((2, 1, N2, D2), lambda k: (0, k, 0, 0)),
                  pl.BlockSpec((1, D2), lambda k: (0, 0))],
        out_specs=pl.BlockSpec((2, 1, N2, D), lambda k: (0, k, 0, 0)),
        out_shape=jax.ShapeDtypeStruct((2, N1, N2, D), F32),
        compiler_params=_cparams("parallel"),
        name="filter_spectrum",
    )(g2, a, nrm)


def _cmul(xr, xi, hr, hi):
    return xr * hr - xi * hi, xr * hi + xi * hr


def _spec_mul_kernel(g_ref, gi_ref, h_ref, a_ref, o_ref, acc_ref):
    n2, C = a_ref.shape[-2:]
    for j in range(a_ref.shape[2]):
        slab = a_ref[0, :, j].reshape(2 * n2, C)
        X = _dft_dot(g_ref[j], slab)
        pr, pi = _cmul(X[:n2], X[n2:], h_ref[0, j], h_ref[1, j])
        Y = _dft_dot(gi_ref[j], jnp.concatenate([pr, pi], axis=0))
        acc_ref[:, 0, j, :] = Y[:n2]
        acc_ref[:, 1, j, :] = Y[n2:]
    o_ref[0] = acc_ref[...].astype(o_ref.dtype)


def spectrum_multiply(g2, g2i, H, a, cl):
    B, _, N1, N2, C = a.shape
    kg = K1_GROUP
    gsp = pl.BlockSpec((kg, 2 * N2, 2 * N2), lambda k, c, b: (k, 0, 0))
    return pl.pallas_call(
        _spec_mul_kernel,
        grid=(N1 // kg, C // cl, B),
        in_specs=[gsp, gsp, pl.BlockSpec((2, kg, N2, cl), lambda k, c, b: (0, k, 0, c)),
                  pl.BlockSpec((1, 2, kg, N2, cl), lambda k, c, b: (b, 0, k, 0, c))],
        out_specs=pl.BlockSpec((1, N2, 2, kg, cl), lambda k, c, b: (b, 0, 0, k, c)),
        out_shape=jax.ShapeDtypeStruct((B, N2, 2, N1, C), DFT_DTYPE),
        scratch_shapes=[pltpu.VMEM((N2, 2, kg, cl), F32)],
        compiler_params=_cparams("parallel", "parallel", "parallel"),
        name="spectrum_multiply",
    )(g2, g2i, H, a)


def _idft_out_kernel(g_ref, y_ref, x0_ref, z_ref, bias_ref, o_ref, acc_ref):
    for s in range(y_ref.shape[1]):
        y = _dft_dot(g_ref[...], y_ref[0, s])
        acc_ref[:, s, :] = x0_ref[0, s] * (y + z_ref[0, s] * bias_ref[...])
    o_ref[0] = acc_ref[...].astype(o_ref.dtype)


def idft_gate_out(g4, yv, x0p, zp, bias, ns, cl):
    B, N2, K, C = yv.shape
    M = g4.shape[0]
    pdat = pl.BlockSpec((1, ns, M, cl), lambda b, j, c: (b, j, 0, c))
    return pl.pallas_call(
        _idft_out_kernel,
        grid=(B, N2 // ns, C // cl),
        in_specs=[pl.BlockSpec((M, K), lambda b, j, c: (0, 0)),
                  pl.BlockSpec((1, ns, K, cl), lambda b, j, c: (b, j, 0, c)),
                  pdat, pdat, pl.BlockSpec((1, cl), lambda b, j, c: (0, c))],
        out_specs=pl.BlockSpec((1, M, ns, cl), lambda b, j, c: (b, 0, j, c)),
        out_shape=jax.ShapeDtypeStruct((B, M, N2, C), BF16),
        scratch_shapes=[pltpu.VMEM((M, ns, cl), F32)],
        compiler_params=_cparams("parallel", "parallel", "parallel"),
        name="idft_gate_out",
    )(g4, yv, x0p, zp, bias.reshape(1, C))


def long_conv_two_stage(x0p, zp, hfbp, nrm, bias):
    B, N2, M, C = zp.shape
    L = N2 * M
    N1 = 2 * M
    g1, g2, g2i, g4 = _dft_tables(N1, N2)
    ns = HALO_ROWS
    cl = min(C, 512)
    a_f = dft_stage1(g1, hfbp, ns, cl)
    H = filter_spectrum(g2, a_f.reshape(2, N1, N2, 2 * C), nrm)
    a = dft_stage1(g1, zp, ns, cl)
    yv = spectrum_multiply(g2, g2i, H, a.reshape(B, 2, N1, N2, C), cl // 2)
    out = idft_gate_out(g4, yv.reshape(B, N2, 2 * N1, C), x0p, zp, bias, ns, cl)
    return out.reshape(B, L, C)


def _dense_tables(L):
    N = 2 * L
    ar = lambda n: jnp.arange(n, dtype=jnp.int32)
    c, s = _cos_sin((ar(N)[:, None] * ar(L)[None, :]) % N, N)
    g = jnp.concatenate([c, -s], axis=0)
    c, s = _cos_sin((ar(L)[:, None] * ar(N)[None, :]) % N, N)
    gi = jnp.concatenate([c, -s], axis=1) * (1.0 / N)
    return g.astype(DFT_DTYPE), gi.astype(DFT_DTYPE)


def _dense_spec_kernel(g_ref, hfb_ref, nrm_ref, o_ref):
    X = _dft_dot(g_ref[...], hfb_ref[...])
    hre, him = _combine_spectrum(X, nrm_ref, o_ref.shape[-1])
    o_ref[0] = hre
    o_ref[1] = him


def _dense_conv_kernel(g_ref, gi_ref, h_ref, z_ref, x0_ref, bias_ref, o_ref):
    z = z_ref[0]
    X = _dft_dot(g_ref[...], z)
    n = X.shape[0] // 2
    pr, pi = _cmul(X[:n], X[n:], h_ref[0], h_ref[1])
    y = _dft_dot(gi_ref[...], jnp.concatenate([pr, pi], axis=0))
    o_ref[0] = (x0_ref[0] * (y + z * bias_ref[...])).astype(o_ref.dtype)


def long_conv_dense(x0, z, hfb, nrm, bias):
    B, L, C = z.shape
    N = 2 * L
    g, gi = _dense_tables(L)
    full2 = lambda shape: pl.BlockSpec(shape, lambda *_: (0,) * len(shape))
    H = pl.pallas_call(
        _dense_spec_kernel,
        grid=(1,),
        in_specs=[full2((2 * N, L)), full2((L, 2 * C)), full2((1, 2 * C))],
        out_specs=full2((2, N, C)),
        out_shape=jax.ShapeDtypeStruct((2, N, C), F32),
        compiler_params=_cparams("arbitrary"),
        name="dense_filter_spectrum",
    )(g, hfb, nrm)
    dat = pl.BlockSpec((1, L, C), lambda b: (b, 0, 0))
    return pl.pallas_call(
        _dense_conv_kernel,
        grid=(B,),
        in_specs=[full2((2 * N, L)), full2((L, 2 * N)), full2((2, N, C)), dat, dat, full2((1, C))],
        out_specs=dat,
        out_shape=jax.ShapeDtypeStruct((B, L, C), BF16),
        compiler_params=_cparams("parallel"),
        name="dense_long_conv",
    )(g, gi, H, z, x0, bias.reshape(1, C))


def hyena_mix_pre(u_args, p, dense):
    x, g, shift, scale = u_args
    (w_in, b_in, sc_w, sc_b, f_w1, f_b1, f_w2, f_b2, f_w3, f_b3, f_wout, freq, bias) = p
    L = x.shape[1]
    perm_n2 = None if dense else V7X_LANES
    proj = norm_mod_matmul(x, g, shift, scale, w_in.astype(BF16), b_in, tm=512)
    x0, z = hy_gate(proj, sc_w, sc_b, perm_n2)
    hfb, nrm = hyena_filter(L, f_w1, f_b1, f_w2, f_b2, f_w3, f_b3, f_wout, freq, perm_n2)
    conv = long_conv_dense if dense else long_conv_two_stage
    return conv(x0, z, hfb, nrm, bias)


def _ml_prep_kernel(xm_ref, prev_ref, next_ref, cw_ref, cb_ref, wq_ref, wk_ref, wv_ref, wg_ref, bg_ref,
                    q_ref, k_ref, v_ref, xc_ref, g_ref, *, k_scale):
    xm = xm_ref[0].astype(F32)
    inner = xm.shape[1]
    conv = _conv3_rows(xm, prev_ref, next_ref, cw_ref, cb_ref)
    xc = conv * jax.nn.sigmoid(conv)
    xcb = xc.astype(BF16)
    xc_ref[0] = xcb
    xmb = xm.astype(BF16)
    gw = V7X_MXU_DIM
    gates = bg_ref[...]
    for j in range(inner // gw):
        sl = slice(j * gw, (j + 1) * gw)
        q = _dot(xcb[:, sl], wq_ref[j])
        k = _dot(xcb[:, sl], wk_ref[j])
        v = _dot(xmb[:, sl], wv_ref[j])
        q_ref[0, :, sl] = q.astype(BF16)
        k_ref[0, :, sl] = (k * k_scale).astype(BF16)
        v_ref[0, :, sl] = v.astype(BF16)
        gates = gates + _dot(q.astype(BF16), wg_ref[j * gw:(j + 1) * gw, :])
        gates = gates + _dot(k.astype(BF16), wg_ref[inner + j * gw:inner + (j + 1) * gw, :])
        gates = gates + _dot(v.astype(BF16), wg_ref[2 * inner + j * gw:2 * inner + (j + 1) * gw, :])
    g_ref[0] = gates


def _block_diag(w, group):
    nb, bs, _ = w.shape
    per = group // bs
    w = w.reshape(nb // per, per, bs, bs)
    eye = jnp.eye(per, dtype=w.dtype)
    dense = jnp.einsum("gpce,pr->gpcre", w, eye)
    return dense.reshape(nb // per, group, group).astype(BF16)


def ml_prep(xz, conv_w, conv_b, wq, wk, wv, w_gate, b_gate, tm=256):
    B, L, C2 = xz.shape
    inner = C2 // 2
    dh = inner // ML_HEADS
    tm = _row_tile(L, tm)
    gw = V7X_MXU_DIM
    ng = inner // gw
    P = V7X_LANES
    wg = jnp.pad(w_gate, ((0, 0), (0, P - w_gate.shape[1]))).astype(BF16)
    bg = jnp.pad(b_gate.reshape(1, -1), ((0, 0), (0, P - b_gate.shape[0])))
    prev, nxt = _halo_specs(tm, L, inner, lambda: 0)
    c2 = lambda shape: pl.BlockSpec(shape, lambda b, i: (0,) * len(shape))
    row = lambda n: pl.BlockSpec((1, tm, n), lambda b, i: (b, i, 0))
    sd = lambda n, dt: jax.ShapeDtypeStruct((B, L, n), dt)
    return pl.pallas_call(
        functools.partial(_ml_prep_kernel, k_scale=dh ** -0.5),
        grid=(B, L // tm),
        in_specs=[row(inner), prev, nxt, c2((3, inner)), c2((1, inner)),
                  c2((ng, gw, gw)), c2((ng, gw, gw)), c2((ng, gw, gw)), c2((3 * inner, P)), c2((1, P))],
        out_specs=(row(inner), row(inner), row(inner), row(inner), row(P)),
        out_shape=(sd(inner, BF16), sd(inner, BF16), sd(inner, BF16), sd(inner, BF16), sd(P, F32)),
        compiler_params=_cparams("parallel", "arbitrary"),
        name="ml_prep",
    )(xz, xz, xz, conv_w, conv_b.reshape(1, inner), _block_diag(wq, gw), _block_diag(wk, gw),
      _block_diag(wv, gw), wg, bg)


def _gates_scan_order(g_ctx, g_lat):
    T = ML_CHUNK

    def lay(g):
        B, L, _ = g.shape
        g = g[..., :4 * ML_HEADS].reshape(B, L // T, T, 2, 2, ML_HEADS)
        return jnp.transpose(g, (3, 0, 5, 4, 1, 2))

    gc, gl = lay(g_ctx), lay(g_lat)
    fwd = jnp.concatenate([gc[0], gl[0]], axis=3)
    bwd = jnp.concatenate([gc[1][..., ::-1, :], gl[1][..., ::-1, :]], axis=3)
    out = jnp.stack([fwd, bwd])
    nct = out.shape[4]
    pad = -nct % V7X_SUBLANES
    return jnp.pad(out, ((0, 0),) * 4 + ((0, pad), (0, 0))), nct


AUX_BMR, AUX_GR, AUX_WINTER, AUX_EMR, AUX_WROW, AUX_GOLD = range(6)


def _ml_gates_kernel(g_ref, aux_ref, bl_scr, me_scr, mp_scr):
    d = pl.program_id(0)
    ig = g_ref[0, 0, 0, 0]
    fg = g_ref[0, 0, 0, 1]
    nct, T = ig.shape
    lf = jnp.minimum(fg, 0.0) - jnp.log1p(jnp.exp(-jnp.abs(fg)))
    lane = lax.broadcasted_iota(jnp.int32, (nct, T), 1)
    rev = d == 1

    def scans(x, op, ident):
        f, r = x, x
        s = 1
        while s < T:
            f = op(f, jnp.where(lane >= s, pltpu.roll(f, s, 1), ident))
            r = op(r, jnp.where(lane < T - s, pltpu.roll(r, T - s, 1), ident))
            s *= 2
        return jnp.where(rev, r, f)

    bcs = scans(lf, jnp.add, 0.0)
    b_last = jnp.sum(lf, axis=1, keepdims=True)
    gr = ig - bcs
    cmax = scans(gr, jnp.maximum, -jnp.inf)
    max_e = b_last + jnp.max(gr, axis=1, keepdims=True)
    bl_scr[...] = jnp.broadcast_to(b_last, bl_scr.shape)
    me_scr[...] = jnp.broadcast_to(max_e, me_scr.shape)

    def body(t, m):
        mp_scr[pl.ds(t, 1), :] = m
        return jnp.maximum(bl_scr[pl.ds(t, 1), :] + m, me_scr[pl.ds(t, 1), :])

    lax.fori_loop(0, nct, body, jnp.zeros((1, V7X_LANES), F32))
    m_prev = mp_scr[:, 0:1]
    a_inter = bcs + m_prev
    m_row = jnp.maximum(a_inter, bcs + cmax)
    m_new = jnp.maximum(b_last + m_prev, max_e)
    rows = {AUX_BMR: bcs - m_row, AUX_GR: gr, AUX_WINTER: jnp.exp(a_inter - m_row),
            AUX_EMR: jnp.exp(-m_row), AUX_WROW: jnp.exp(b_last + gr - m_new),
            AUX_GOLD: jnp.broadcast_to(jnp.exp(b_last + m_prev - m_new), (nct, T))}
    zero = jnp.zeros((nct, T), F32)
    for k in range(V7X_SUBLANES):
        aux_ref[0, 0, 0, :, k, :] = rows.get(k, zero)


def ml_gates(gates):
    _, B, H, _, nct, T = gates.shape
    return pl.pallas_call(
        _ml_gates_kernel,
        grid=(2, B, H),
        in_specs=[pl.BlockSpec((1, 1, 1, 2, nct, T), lambda d, b, h: (d, b, h, 0, 0, 0))],
        out_specs=pl.BlockSpec((1, 1, 1, nct, V7X_SUBLANES, T), lambda d, b, h: (d, b, h, 0, 0, 0)),
        out_shape=jax.ShapeDtypeStruct((2, B, H, nct, V7X_SUBLANES, T), F32),
        scratch_shapes=[pltpu.VMEM((nct, V7X_LANES), F32)] * 3,
        compiler_params=_cparams("arbitrary", "arbitrary", "arbitrary"),
        name="ml_gates",
    )(gates)


SCAN_HEADS = 2


def _mlstm_kernel(qf_ref, kf_ref, vf_ref, qb_ref, kb_ref, vb_ref, aux_ref, c0_ref,
                  hf_ref, hb_ref, cf_ref, c_scr):
    t = pl.program_id(2)
    nct = pl.num_programs(2)
    T = ML_CHUNK
    dh = qf_ref.shape[-1] // SCAN_HEADS

    @pl.when(t == 0)
    def _():
        c_scr[...] = c0_ref[:, 0]

    row = lax.broadcasted_iota(jnp.int32, (T, T), 0)
    col = lax.broadcasted_iota(jnp.int32, (T, T), 1)
    ones_col = (lax.broadcasted_iota(jnp.int32, (T, V7X_LANES), 1) == 0).astype(F32)
    dirs = ((qf_ref, kf_ref, vf_ref, hf_ref, col <= row), (qb_ref, kb_ref, vb_ref, hb_ref, col >= row))
    streams = [(d, j) + dirs[d] for j in range(SCAN_HEADS) for d in range(2)]
    for d, j, q_ref, k_ref, v_ref, h_ref, mask in streams:
        hs = slice(j * dh, (j + 1) * dh)
        ax = aux_ref[d, 0, j, 0]
        axt = jnp.concatenate([ax, jnp.zeros((T - V7X_SUBLANES, T), F32)], axis=0).T
        gr = ax[AUX_GR:AUX_GR + 1, :]
        g_old = ax[AUX_GOLD:AUX_GOLD + 1, 0:1]
        bmr = axt[:, AUX_BMR:AUX_BMR + 1]
        w_inter = axt[:, AUX_WINTER:AUX_WINTER + 1]
        emr = axt[:, AUX_EMR:AUX_EMR + 1]
        w_col = axt[:, AUX_WROW:AUX_WROW + 1]
        pmat = jnp.exp(jnp.where(mask, bmr + gr, -jnp.inf))

        q = q_ref[0, :, hs]
        v = v_ref[0, :, hs]
        kT = k_ref[0, :, hs].astype(F32).T.astype(BF16)
        smat = _dot(q, kT) * pmat
        r1 = _dot(q, c_scr[d, j].astype(BF16))
        v_aug = jnp.concatenate([v.astype(F32), ones_col], axis=1)
        r2 = _dot(smat.astype(BF16), v_aug.astype(BF16))
        num = w_inter * r1[:, :dh] + r2[:, :dh]
        den = w_inter * r1[:, dh:dh + 1] + r2[:, dh:dh + 1]
        h_ref[0, :, hs] = (num / jnp.maximum(jnp.abs(den), emr)).astype(h_ref.dtype)
        c_scr[d, j] = g_old * c_scr[d, j] + _dot(kT, (v_aug * w_col).astype(BF16))

    @pl.when(t == nct - 1)
    def _():
        cf_ref[:, 0] = c_scr[...]


def mlstm_scan(q, k, v, aux, chunk0, c0):
    B, L, inner = q.shape
    H = ML_HEADS
    dh = inner // H
    T = ML_CHUNK
    nc = L // T
    da = dh + V7X_LANES
    hp = SCAN_HEADS
    fw = pl.BlockSpec((1, T, hp * dh), lambda b, h, t: (b, t, h))
    bw = pl.BlockSpec((1, T, hp * dh), lambda b, h, t: (b, nc - 1 - t, h))
    cst = pl.BlockSpec((2, 1, hp, dh, da), lambda b, h, t: (0, b, h, 0, 0))
    hshape = jax.ShapeDtypeStruct((B, L, inner), BF16)
    return pl.pallas_call(
        _mlstm_kernel,
        grid=(B, H // hp, nc),
        in_specs=[fw, fw, fw, bw, bw, bw,
                  pl.BlockSpec((2, 1, hp, 1, V7X_SUBLANES, T), lambda b, h, t: (0, b, h, chunk0 + t, 0, 0)),
                  cst],
        out_specs=(fw, bw, cst),
        out_shape=(hshape, hshape, jax.ShapeDtypeStruct(c0.shape, F32)),
        scratch_shapes=[pltpu.VMEM((2, hp, dh, da), F32)],
        compiler_params=_cparams("parallel", "parallel", "arbitrary"),
        name="mlstm_scan",
    )(q, k, v, q, k, v, aux, c0)


def _ml_out_kernel(hf_ref, hb_ref, xc_ref, z_ref, nw_ref, sk_ref, o_ref):
    h = hf_ref[0].astype(F32) + hb_ref[0].astype(F32)
    dh = h.shape[1] // ML_HEADS
    z = z_ref[0].astype(F32)
    gate = z * jax.nn.sigmoid(z)
    for j in range(ML_HEADS):
        sl = slice(j * dh, (j + 1) * dh)
        seg = h[:, sl]
        mu = jnp.mean(seg, axis=-1, keepdims=True)
        cen = seg - mu
        var = jnp.mean(cen * cen, axis=-1, keepdims=True)
        hn = cen * lax.rsqrt(var + ML_NORM_EPS)
        hs = hn * nw_ref[:, sl] + sk_ref[:, sl] * xc_ref[0, :, sl].astype(F32)
        o_ref[0, :, sl] = (hs * gate[:, sl]).astype(o_ref.dtype)


def ml_out(hf, hb, xc, xz, norm_w, skip, tm=256):
    B, L, inner = hf.shape
    tm = _row_tile(L, tm)
    vec = pl.BlockSpec((1, inner), lambda b, i: (0, 0))
    dat = pl.BlockSpec((1, tm, inner), lambda b, i: (b, i, 0))
    return pl.pallas_call(
        _ml_out_kernel,
        grid=(B, L // tm),
        in_specs=[dat, dat, dat, pl.BlockSpec((1, tm, inner), lambda b, i: (b, i, 1)), vec, vec],
        out_specs=dat,
        out_shape=jax.ShapeDtypeStruct((B, L, inner), BF16),
        compiler_params=_cparams("parallel", "parallel"),
        name="ml_out",
    )(hf, hb, xc, xz, norm_w.reshape(1, inner), skip.reshape(1, inner))


def mlstm_mix_pre(lat_args, ctx_args, p):
    w_in, conv_w, conv_b, wq, wk, wv, w_gate, b_gate, norm_w, skip = p
    w_in_b = w_in.astype(BF16)
    zero_b = jnp.zeros((w_in.shape[1],), F32)

    def prep(args):
        x, g, shift, scale = args
        xz = norm_mod_matmul(x, g, shift, scale, w_in_b, zero_b, tm=512)
        return ml_prep(xz, conv_w, conv_b, wq, wk, wv, w_gate, b_gate) + (xz,)

    qc, kc, vc, _, gates_c, _ = prep(ctx_args)
    q, k, v, xc, gates_l, xz = prep(lat_args)
    B, _, inner = q.shape
    dh = inner // ML_HEADS
    gates, _ = _gates_scan_order(gates_c, gates_l)
    aux = ml_gates(gates)
    c0 = jnp.zeros((2, B, ML_HEADS, dh, dh + V7X_LANES), F32)
    _, _, c1 = mlstm_scan(qc, kc, vc, aux, 0, c0)
    hf, hb, _ = mlstm_scan(q, k, v, aux, qc.shape[1] // ML_CHUNK, c1)
    return ml_out(hf, hb, xc, xz, norm_w, skip)


def _ffn_kernel(x_ref, xp_ref, xn_ref, ng_ref, sh_ref, sc_ref, wu_ref, cw_ref, cb_ref, wd_ref,
                gate_ref, fg_ref, o_ref, u_scr, *, cols, vertical, final_norm, cb):
    i = pl.program_id(1)
    last = pl.num_programs(1) - 1
    tm = x_ref.shape[1]
    F = wd_ref.shape[0]
    halo = cols if vertical else 0

    def norm_mod(x):
        y = x * lax.rsqrt(jnp.mean(x * x, axis=-1, keepdims=True) + EPS)
        return ((y * ng_ref[...]) * (1.0 + sc_ref[0]) + sh_ref[0]).astype(BF16)

    u_scr[halo:halo + tm] = norm_mod(x_ref[0])
    if vertical:
        u_scr[0:halo] = norm_mod(xp_ref[0])
        u_scr[halo + tm:] = norm_mod(xn_ref[0])
    R = tm + 2 * halo
    rowi = lax.broadcasted_iota(jnp.int32, (R, 1), 0)
    cpos = jnp.bitwise_and(rowi, cols - 1)
    if vertical:
        top_ok = jnp.where(i > 0, 1.0, 0.0)
        bot_ok = jnp.where(i < last, 1.0, 0.0)
        rowmask = jnp.where(rowi < halo, top_ok, jnp.where(rowi >= halo + tm, bot_ok, 1.0))
    acc = jnp.zeros((tm, o_ref.shape[-1]), F32)
    for f in range(F // cb):
        fs = slice(f * cb, (f + 1) * cb)
        g = _dot(u_scr[...], wu_ref[:, F + f * cb:F + (f + 1) * cb])
        a = _dot(u_scr[halo:halo + tm], wu_ref[:, fs])
        if vertical:
            g = g * rowmask
        left = jnp.where(cpos == 0, 0.0, pltpu.roll(g, 1, 0))
        right = jnp.where(cpos == cols - 1, 0.0, pltpu.roll(g, R - 1, 0))
        conv = cb_ref[:, fs]
        for dr in (range(3) if vertical else (1,)):
            sl = slice(dr * halo, dr * halo + tm)
            conv = conv + (cw_ref[3 * dr:3 * dr + 1, fs] * left[sl] + cw_ref[3 * dr + 1:3 * dr + 2, fs] * g[sl]
                           + cw_ref[3 * dr + 2:3 * dr + 3, fs] * right[sl])
        act = ((conv * jax.nn.sigmoid(conv)) * a).astype(BF16)
        acc = acc + _dot(act, wd_ref[fs, :])
    x = x_ref[0] + gate_ref[0] * acc
    if final_norm:
        x = (x * lax.rsqrt(jnp.mean(x * x, axis=-1, keepdims=True) + EPS)) * fg_ref[...]
    o_ref[0] = x


def conv_ffn_residual(x, g, shift, scale, gate, w_up_b, conv_w, conv_b, w_down_b, rows, cols, final_g=None,
                      tm=512):
    B, L, D = x.shape
    F = w_down_b.shape[0]
    assert cols & (cols - 1) == 0 and rows * cols == L
    vertical = rows > 1
    tm = _row_tile(L, tm) if vertical else L
    cb = V7X_MXU_DIM
    assert F % cb == 0 and tm % cols == 0
    hb = cols if vertical else HALO_ROWS
    nhb = L // hb
    final_norm = final_g is not None
    fg = final_g if final_norm else jnp.ones((D,), F32)
    vecb = pl.BlockSpec((1, 1, D), lambda b, i: (b, 0, 0))
    full = lambda shape: pl.BlockSpec(shape, lambda b, i: (0, 0))
    row = pl.BlockSpec((1, tm, D), lambda b, i: (b, i, 0))
    return pl.pallas_call(
        functools.partial(_ffn_kernel, cols=cols, vertical=vertical, final_norm=final_norm, cb=cb),
        grid=(B, L // tm),
        in_specs=[row,
                  pl.BlockSpec((1, hb, D), lambda b, i: (b, jnp.maximum(i * (tm // hb) - 1, 0), 0)),
                  pl.BlockSpec((1, hb, D), lambda b, i: (b, jnp.minimum((i + 1) * (tm // hb), nhb - 1), 0)),
                  full((1, D)), vecb, vecb, full((D, 2 * F)), full((9, F)), full((1, F)), full((F, D)),
                  vecb, full((1, D))],
        out_specs=row,
        out_shape=jax.ShapeDtypeStruct((B, L, D), F32),
        scratch_shapes=[pltpu.VMEM((tm + (2 * cols if vertical else 0), D), BF16)],
        compiler_params=_cparams("parallel", "arbitrary"),
        name="conv_ffn",
    )(x, x, x, g.reshape(1, D), shift, scale, w_up_b, conv_w.reshape(9, F), conv_b.reshape(1, F),
      w_down_b, gate, fg.reshape(1, D))


def kernel(x, c, ctx, c_ctx, mod_w, mod_b, norm_g, final_g, hy_w_in, hy_b_in, hy_sc_w, hy_sc_b, hy_f_w1, hy_f_b1, hy_f_w2, hy_f_b2, hy_f_w3, hy_f_b3, hy_f_wout, hy_freq, hy_bias, hy_w_out, hy_b_out, ml_w_in, ml_conv_w, ml_conv_b, ml_wq, ml_wk, ml_wv, ml_w_gate, ml_b_gate, ml_norm_w, ml_skip, ml_w_down, ffn_w_up, ffn_conv_w, ffn_conv_b, ffn_w_down):
    B, L, D = x.shape
    ctx_len = ctx.shape[1]
    depth = mod_w.shape[0]
    n_mixers = 2
    rows = L // GRID_W
    hy_params = (hy_w_in, hy_b_in, hy_sc_w, hy_sc_b, hy_f_w1, hy_f_b1, hy_f_w2, hy_f_b2,
                 hy_f_w3, hy_f_b3, hy_f_wout, hy_freq, hy_bias)
    ml_params = (ml_w_in, ml_conv_w, ml_conv_b, ml_wq, ml_wk, ml_wv, ml_w_gate, ml_b_gate,
                 ml_norm_w, ml_skip)
    cond = jnp.concatenate([c, c_ctx.reshape(1, D), jnp.zeros((V7X_SUBLANES - B - 1, D), F32)], axis=0)
    for i in range(depth):
        last = i == depth - 1
        mod = adaln(cond, mod_w[i], mod_b[i])
        lat = [mod[:B, k * D:(k + 1) * D].reshape(B, 1, D) for k in range(6)]
        cm = [jnp.broadcast_to(mod[B:B + 1, k * D:(k + 1) * D].reshape(1, 1, D), (B, 1, D)) for k in range(6)]
        lat_args = (x, norm_g[i, 0], lat[0], lat[1])
        ctx_args = (ctx, norm_g[i, 0], cm[0], cm[1])
        j = i // n_mixers
        if i % n_mixers == 0:
            p = tuple(a[j] for a in hy_params)
            w_out_b = hy_w_out[j].astype(BF16)
            x = res_gate_matmul(hyena_mix_pre(lat_args, p, dense=False), w_out_b, hy_b_out[j], lat[2], x)
            if not last:
                ctx = res_gate_matmul(hyena_mix_pre(ctx_args, p, dense=True), w_out_b, hy_b_out[j], cm[2], ctx)
        else:
            assert last, "the mLSTM mixer is only implemented for the last layer (no context output)"
            p = tuple(a[j] for a in ml_params)
            act = mlstm_mix_pre(lat_args, ctx_args, p)
            x = res_gate_matmul(act, ml_w_down[j].astype(BF16), jnp.zeros((D,), F32), lat[2], x)
        w_up_b = ffn_w_up[i].astype(BF16)
        w_down_b = ffn_w_down[i].astype(BF16)
        x = conv_ffn_residual(x, norm_g[i, 1], lat[3], lat[4], lat[5], w_up_b, ffn_conv_w[i], ffn_conv_b[i],
                              w_down_b, rows, GRID_W, final_g if last else None)
        if not last:
            ctx = conv_ffn_residual(ctx, norm_g[i, 1], cm[3], cm[4], cm[5], w_up_b, ffn_conv_w[i],
                                    ffn_conv_b[i], w_down_b, 1, ctx_len)
    return x
```

```python
import functools
import math

import jax
import jax.numpy as jnp
from jax import lax
from jax.experimental import pallas as pl
from jax.experimental.pallas import tpu as pltpu

F32 = jnp.float32
BF16 = jnp.bfloat16
HIGHEST = lax.Precision.HIGHEST

EPS = 1e-6
ML_NORM_EPS = 1e-5
GRID_W = 64
ML_HEADS = 4
ML_CHUNK = 128
ML_QKV_BLOCK = 4
HY_EMB_BANDS = 16
HY_MAX_DECAY = math.log(1e-2) / 0.3
HY_MIN_DECAY = math.log(1e-2) / 1.5

V7X_LANES = 128
V7X_SUBLANES = 8
V7X_MXU_DIM = 256
V7X_VMEM_BYTES = 64 * 1024 * 1024
VMEM_LIMIT = V7X_VMEM_BYTES - 8 * 1024 * 1024
HALO_ROWS = 16


def _cparams(*sem):
    return pltpu.CompilerParams(dimension_semantics=("arbitrary",) * len(sem), vmem_limit_bytes=VMEM_LIMIT)


def _dot(a, b, precision=None):
    return jnp.dot(a, b, preferred_element_type=F32, precision=precision)


def _row_tile(n, target):
    t = min(n, target)
    assert n % t == 0, (n, t)
    return t


def _adaln_kernel(c_ref, w_ref, b_ref, o_ref):
    c = c_ref[...]
    s = c * jax.nn.sigmoid(c)
    o_ref[...] = _dot(s, w_ref[...], HIGHEST) + b_ref[...]


def adaln(cond8, w, b):
    r, d = cond8.shape
    n = w.shape[1]
    tn = _row_tile(n, 1536)
    return pl.pallas_call(
        _adaln_kernel,
        grid=(n // tn,),
        in_specs=[pl.BlockSpec((r, d), lambda j: (0, 0)),
                  pl.BlockSpec((d, tn), lambda j: (0, j)),
                  pl.BlockSpec((1, tn), lambda j: (0, j))],
        out_specs=pl.BlockSpec((r, tn), lambda j: (0, j)),
        out_shape=jax.ShapeDtypeStruct((r, n), F32),
        compiler_params=_cparams("arbitrary"),
        name="adaln",
    )(cond8, w, b.reshape(1, n))


def _nmm_kernel(x_ref, g_ref, sh_ref, sc_ref, w_ref, b_ref, o_ref):
    x = x_ref[0]
    y = x * lax.rsqrt(jnp.mean(x * x, axis=-1, keepdims=True) + EPS)
    u = (y * g_ref[...]) * (1.0 + sc_ref[0]) + sh_ref[0]
    acc = _dot(u.astype(BF16), w_ref[...])
    o_ref[0] = (acc + b_ref[...]).astype(o_ref.dtype)


def norm_mod_matmul(x, g, shift, scale, w_bf16, bias, tm, out_dtype=BF16):
    B, L, D = x.shape
    n = w_bf16.shape[1]
    tm = _row_tile(L, tm)
    return pl.pallas_call(
        _nmm_kernel,
        grid=(B, L // tm),
        in_specs=[pl.BlockSpec((1, tm, D), lambda b, i: (b, i, 0)),
                  pl.BlockSpec((1, D), lambda b, i: (0, 0)),
                  pl.BlockSpec((1, 1, D), lambda b, i: (b, 0, 0)),
                  pl.BlockSpec((1, 1, D), lambda b, i: (b, 0, 0)),
                  pl.BlockSpec((D, n), lambda b, i: (0, 0)),
                  pl.BlockSpec((1, n), lambda b, i: (0, 0))],
        out_specs=pl.BlockSpec((1, tm, n), lambda b, i: (b, i, 0)),
        out_shape=jax.ShapeDtypeStruct((B, L, n), out_dtype),
        compiler_params=_cparams("parallel", "parallel"),
        name="norm_mod_matmul",
    )(x, g.reshape(1, D), shift, scale, w_bf16, bias.reshape(1, n))


def _rgm_kernel(a_ref, w_ref, b_ref, gate_ref, res_ref, fg_ref, o_ref, *, final_norm):
    acc = _dot(a_ref[0], w_ref[...]) + b_ref[...]
    x = res_ref[0] + gate_ref[0] * acc
    if final_norm:
        x = (x * lax.rsqrt(jnp.mean(x * x, axis=-1, keepdims=True) + EPS)) * fg_ref[...]
    o_ref[0] = x


def res_gate_matmul(a_bf16, w_bf16, bias, gate, res, final_g=None, tm=512):
    B, L, K = a_bf16.shape
    D = w_bf16.shape[1]
    tm = _row_tile(L, tm)
    final_norm = final_g is not None
    fg = final_g if final_norm else jnp.ones((D,), F32)
    return pl.pallas_call(
        functools.partial(_rgm_kernel, final_norm=final_norm),
        grid=(B, L // tm),
        in_specs=[pl.BlockSpec((1, tm, K), lambda b, i: (b, i, 0)),
                  pl.BlockSpec((K, D), lambda b, i: (0, 0)),
                  pl.BlockSpec((1, D), lambda b, i: (0, 0)),
                  pl.BlockSpec((1, 1, D), lambda b, i: (b, 0, 0)),
                  pl.BlockSpec((1, tm, D), lambda b, i: (b, i, 0)),
                  pl.BlockSpec((1, D), lambda b, i: (0, 0))],
        out_specs=pl.BlockSpec((1, tm, D), lambda b, i: (b, i, 0)),
        out_shape=jax.ShapeDtypeStruct((B, L, D), F32),
        compiler_params=_cparams("parallel", "parallel"),
        name="res_gate_matmul",
    )(a_bf16, w_bf16, bias.reshape(1, D), gate, res, fg.reshape(1, D))


def _halo_specs(tm, L, C, cmap):
    r = HALO_ROWS
    nb = L // r
    prev = pl.BlockSpec((1, r, C), lambda b, i, *a: (b, jnp.maximum(i * (tm // r) - 1, 0), cmap(*a)))
    nxt = pl.BlockSpec((1, r, C), lambda b, i, *a: (b, jnp.minimum((i + 1) * (tm // r), nb - 1), cmap(*a)))
    return prev, nxt


def _conv3_rows(x, prev_ref, next_ref, w_ref, b_ref):
    tm = x.shape[0]
    i = pl.program_id(1)
    last = pl.num_programs(1) - 1
    prev_row = jnp.where(i > 0, prev_ref[0].astype(F32)[HALO_ROWS - 1:HALO_ROWS, :], 0.0)
    next_row = jnp.where(i < last, next_ref[0].astype(F32)[0:1, :], 0.0)
    row = lax.broadcasted_iota(jnp.int32, x.shape, 0)
    xm1 = jnp.where(row == 0, prev_row, pltpu.roll(x, 1, 0))
    xp1 = jnp.where(row == tm - 1, next_row, pltpu.roll(x, tm - 1, 0))
    return w_ref[0:1, :] * xm1 + w_ref[1:2, :] * x + w_ref[2:3, :] * xp1 + b_ref[...]


def _store_rows(o_ref, val, perm, cs=slice(None)):
    if not perm:
        o_ref[0, :, cs] = val
        return
    n2 = o_ref.shape[1]
    for j in range(o_ref.shape[2]):
        o_ref[0, :, j, cs] = val[j * n2:(j + 1) * n2]


def _x_halo_specs(tm, L, D):
    r = HALO_ROWS
    nb = L // r
    return (pl.BlockSpec((1, tm, D), lambda b, i: (b, i, 0)),
            pl.BlockSpec((1, r, D), lambda b, i: (b, jnp.maximum(i * (tm // r) - 1, 0), 0)),
            pl.BlockSpec((1, r, D), lambda b, i: (b, jnp.minimum((i + 1) * (tm // r), nb - 1), 0)))


def _norm_mod_ext(x_ref, xp_ref, xn_ref, ng_ref, sh_ref, sc_ref, u_scr):
    tm = x_ref.shape[1]
    h = HALO_ROWS

    def norm_mod(x):
        y = x * lax.rsqrt(jnp.mean(x * x, axis=-1, keepdims=True) + EPS)
        return ((y * ng_ref[...]) * (1.0 + sc_ref[0]) + sh_ref[0]).astype(BF16)

    u_scr[0:h] = norm_mod(xp_ref[0])
    u_scr[h:h + tm] = norm_mod(x_ref[0])
    u_scr[h + tm:] = norm_mod(xn_ref[0])
    i = pl.program_id(1)
    rowi = lax.broadcasted_iota(jnp.int32, (tm + 2 * h, 1), 0)
    top_ok = jnp.where(i > 0, 1.0, 0.0)
    bot_ok = jnp.where(i < pl.num_programs(1) - 1, 1.0, 0.0)
    return jnp.where(rowi < h, top_ok, jnp.where(rowi >= h + tm, bot_ok, 1.0))


def _conv3_ext(p, w_ref, b_ref, cs, tm):
    h = HALO_ROWS
    R = p.shape[0]
    return (w_ref[0:1, cs] * pltpu.roll(p, 1, 0)[h:h + tm] + w_ref[1:2, cs] * p[h:h + tm]
            + w_ref[2:3, cs] * pltpu.roll(p, R - 1, 0)[h:h + tm] + b_ref[:, cs])


def _hy_in_gate_kernel(x_ref, xp_ref, xn_ref, ng_ref, sh_ref, sc_ref, w_ref, b_ref, cw_ref, cb_ref,
                       x0_ref, z_ref, u_scr, *, perm, cb):
    tm, D = x_ref.shape[1:]
    rowmask = _norm_mod_ext(x_ref, xp_ref, xn_ref, ng_ref, sh_ref, sc_ref, u_scr)
    for c in range(D // cb):
        conv = []
        for k in range(3):
            cs = slice(k * D + c * cb, k * D + (c + 1) * cb)
            p = (_dot(u_scr[...], w_ref[:, cs]) + b_ref[:, cs]) * rowmask
            conv.append(_conv3_ext(p, cw_ref, cb_ref, cs, tm))
        cs = slice(c * cb, (c + 1) * cb)
        _store_rows(x0_ref, conv[0], perm, cs)
        _store_rows(z_ref, conv[1] * conv[2], perm, cs)


def hy_in_gate(x, g, shift, scale, w_bf16, b_in, sc_w, sc_b, perm_n2=None):
    B, L, D = x.shape
    C3 = w_bf16.shape[1]
    perm = perm_n2 is not None
    tm = V7X_SUBLANES * perm_n2 if perm else L
    assert L % tm == 0
    full = lambda shape: pl.BlockSpec(shape, lambda b, i: (0, 0))
    vecb = pl.BlockSpec((1, 1, D), lambda b, i: (b, 0, 0))
    if perm:
        out = jax.ShapeDtypeStruct((B, perm_n2, L // perm_n2, D), F32)
        ospec = pl.BlockSpec((1, perm_n2, V7X_SUBLANES, D), lambda b, i: (b, 0, i, 0))
    else:
        out = jax.ShapeDtypeStruct((B, L, D), F32)
        ospec = pl.BlockSpec((1, tm, D), lambda b, i: (b, i, 0))
    return pl.pallas_call(
        functools.partial(_hy_in_gate_kernel, perm=perm, cb=V7X_MXU_DIM),
        grid=(B, L // tm),
        in_specs=[*_x_halo_specs(tm, L, D), full((1, D)), vecb, vecb, full((D, C3)), full((1, C3)),
                  full((3, C3)), full((1, C3))],
        out_specs=(ospec, ospec),
        out_shape=(out, out),
        scratch_shapes=[pltpu.VMEM((tm + 2 * HALO_ROWS, D), BF16)],
        compiler_params=_cparams("parallel", "arbitrary"),
        name="hy_in_gate",
    )(x, x, x, g.reshape(1, D), shift, scale, w_bf16, b_in.reshape(1, C3), sc_w, sc_b.reshape(1, C3))


def _filter_kernel(w1_ref, b1_ref, w2_ref, b2_ref, w3_ref, b3_ref, wo_ref, fr_ref, band_ref,
                   h_ref, s_ref, *, L, D, tl, perm):
    i = pl.program_id(0)
    rowi = lax.broadcasted_iota(jnp.int32, (tl, 1), 0) + i * tl
    row = rowi.astype(F32)
    t = row / (L - 1.0)
    w = (2.0 * math.pi) * row / L
    col = lax.broadcasted_iota(jnp.int32, (tl, V7X_LANES), 1)
    ang = w * band_ref[...]
    pos = jnp.where(col == 0, t,
                    jnp.where(col <= HY_EMB_BANDS, jnp.cos(ang),
                              jnp.where(col <= 2 * HY_EMB_BANDS, -jnp.sin(ang), 0.0)))
    fr = fr_ref[...]
    h = jnp.sin(fr * (_dot(pos, w1_ref[...], HIGHEST) + b1_ref[...]))
    h = jnp.sin(fr * (_dot(h, w2_ref[...], HIGHEST) + b2_ref[...]))
    h = jnp.sin(fr * (_dot(h, w3_ref[...], HIGHEST) + b3_ref[...]))
    h = _dot(h, wo_ref[...], HIGHEST)
    dcol = lax.broadcasted_iota(jnp.int32, (1, 2 * D), 1)
    chan = jnp.where(dcol >= D, dcol - D, dcol).astype(F32)
    deltas = HY_MIN_DECAY + chan * ((HY_MAX_DECAY - HY_MIN_DECAY) / (D - 1.0))
    h = h * jnp.exp(-t * jnp.abs(deltas))
    h = jnp.where((rowi == 0) & (dcol >= D), 0.0, h)
    if perm:
        _store_rows(h_ref, h, True)
    else:
        h_ref[...] = h
    part = jnp.sum(jnp.abs(h), axis=0, keepdims=True)

    @pl.when(i == 0)
    def _():
        s_ref[...] = part

    @pl.when(i > 0)
    def _():
        s_ref[...] += part


def hyena_filter(L, f_w1, f_b1, f_w2, f_b2, f_w3, f_b3, f_wout, freq, perm_n2=None):
    W = f_w2.shape[0]
    D2 = f_wout.shape[1]
    P = V7X_LANES
    pad2 = lambda a: jnp.pad(a, ((0, P - a.shape[0]), (0, P - a.shape[1])))
    padv = lambda a: jnp.pad(a.reshape(1, -1), ((0, 0), (0, P - a.shape[0])))
    w1 = pad2(f_w1)
    w2 = pad2(f_w2)
    w3 = pad2(f_w3)
    wo = jnp.pad(f_wout, ((0, P - W), (0, 0)))
    bands = jnp.linspace(1e-4, HY_EMB_BANDS - 1, HY_EMB_BANDS, dtype=F32)
    band_row = jnp.concatenate([jnp.zeros((1,), F32), bands, bands,
                                jnp.zeros((P - 1 - 2 * HY_EMB_BANDS,), F32)]).reshape(1, P)
    perm = perm_n2 is not None
    tl = V7X_SUBLANES * perm_n2 if perm else _row_tile(L, 512)
    assert L % tl == 0
    full = lambda shape: pl.BlockSpec(shape, lambda i: (0, 0))
    if perm:
        hshape = jax.ShapeDtypeStruct((1, perm_n2, L // perm_n2, D2), F32)
        hspec = pl.BlockSpec((1, perm_n2, V7X_SUBLANES, D2), lambda i: (0, 0, i, 0))
    else:
        hshape = jax.ShapeDtypeStruct((L, D2), F32)
        hspec = pl.BlockSpec((tl, D2), lambda i: (i, 0))
    return pl.pallas_call(
        functools.partial(_filter_kernel, L=L, D=D2 // 2, tl=tl, perm=perm),
        grid=(L // tl,),
        in_specs=[full((P, P)), full((1, P)), full((P, P)), full((1, P)), full((P, P)), full((1, P)),
                  full((P, D2)), full((1, P)), full((1, P))],
        out_specs=(hspec, pl.BlockSpec((1, D2), lambda i: (0, 0))),
        out_shape=(hshape, jax.ShapeDtypeStruct((1, D2), F32)),
        compiler_params=_cparams("arbitrary"),
        name="hyena_filter",
    )(w1, padv(f_b1), w2, padv(f_b2), w3, padv(f_b3), wo, padv(freq), band_row)


DFT_DTYPE = BF16
K1_GROUP = HALO_ROWS


def _cos_sin(p, n):
    ang = (2.0 * math.pi / n) * p.astype(F32)
    return jnp.cos(ang), jnp.sin(ang)


def _dft_dot(g, x):
    return _dot(g.astype(DFT_DTYPE), x.astype(DFT_DTYPE), HIGHEST if DFT_DTYPE == F32 else None)


def _dft_tables(N1, N2):
    N = N1 * N2
    ar = lambda n: jnp.arange(n, dtype=jnp.int32)
    c, s = _cos_sin((ar(N1)[:, None] * ar(N1 // 2)[None, :]) % N1, N1)
    g1 = jnp.concatenate([c, -s], axis=0)
    k1 = ar(N1)[:, None, None]
    a = ar(N2)[None, :, None]
    b = ar(N2)[None, None, :]
    c, s = _cos_sin((b * (k1 + N1 * a)) % N, N)
    g2 = jnp.concatenate([jnp.concatenate([c, s], axis=2), jnp.concatenate([-s, c], axis=2)], axis=1)
    c, s = _cos_sin((a * (k1 + N1 * b)) % N, N)
    g2i = jnp.concatenate([jnp.concatenate([c, -s], axis=2), jnp.concatenate([s, c], axis=2)], axis=1)
    c, s = _cos_sin((ar(N1 // 2)[:, None] * ar(N1)[None, :]) % N1, N1)
    g4 = jnp.concatenate([c, -s], axis=1) * (1.0 / N)
    return tuple(t.astype(DFT_DTYPE) for t in (g1, g2, g2i, g4))


def _stage1_kernel(g_ref, x_ref, o_ref, acc_ref):
    for s in range(x_ref.shape[1]):
        acc_ref[:, s, :] = _dft_dot(g_ref[...], x_ref[0, s])
    o_ref[0] = acc_ref[...].astype(o_ref.dtype)


def dft_stage1(g, xp, ns, cl):
    B, N2, K, C = xp.shape
    M = g.shape[0]
    return pl.pallas_call(
        _stage1_kernel,
        grid=(B, N2 // ns, C // cl),
        in_specs=[pl.BlockSpec((M, K), lambda b, j, c: (0, 0)),
                  pl.BlockSpec((1, ns, K, cl), lambda b, j, c: (b, j, 0, c))],
        out_specs=pl.BlockSpec((1, M, ns, cl), lambda b, j, c: (b, 0, j, c)),
        out_shape=jax.ShapeDtypeStruct((B, M, N2, C), DFT_DTYPE),
        scratch_shapes=[pltpu.VMEM((M, ns, cl), F32)],
        compiler_params=_cparams("parallel", "parallel", "parallel"),
        name="dft_stage1",
    )(g, xp)


def _combine_spectrum(X, nrm_ref, D):
    n = X.shape[0] // 2
    nrm = nrm_ref[:, :D] + nrm_ref[:, D:]
    hre = (X[:n, :D] + X[:n, D:]) / nrm
    him = (X[n:, :D] - X[n:, D:]) / nrm
    return hre, him


def _filter_spec_kernel(g_ref, a_ref, nrm_ref, o_ref):
    _, _, n2, D2 = a_ref.shape
    slab = a_ref[:, 0].reshape(2 * n2, D2)
    X = _dft_dot(g_ref[0], slab)
    hre, him = _combine_spectrum(X, nrm_ref, D2 // 2)
    o_ref[0, 0] = hre
    o_ref[1, 0] = him


def filter_spectrum(g2, a, nrm):
    _, N1, N2, D2 = a.shape
    D = D2 // 2
    return pl.pallas_call(
        _filter_spec_kernel,
        grid=(N1,),
        in_specs=[pl.BlockSpec((1, 2 * N2, 2 * N2), lambda k: (k, 0, 0)),
                  pl.BlockSpec((2, 1, N2, D2), lambda k: (0, k, 0, 0)),
                  pl.BlockSpec((1, D2), lambda k: (0, 0))],
        out_specs=pl.BlockSpec((2, 1, N2, D), lambda k: (0, k, 0, 0)),
        out_shape=jax.ShapeDtypeStruct((2, N1, N2, D), F32),
        compiler_params=_cparams("parallel"),
        name="filter_spectrum",
    )(g2, a, nrm)


def _cmul(xr, xi, hr, hi):
    return xr * hr - xi * hi, xr * hi + xi * hr


def _spec_mul_kernel(g_ref, gi_ref, h_ref, a_ref, o_ref, acc_ref):
    n2, C = a_ref.shape[-2:]
    for j in range(a_ref.shape[2]):
        slab = a_ref[0, :, j].reshape(2 * n2, C)
        X = _dft_dot(g_ref[j], slab)
        pr, pi = _cmul(X[:n2], X[n2:], h_ref[0, j], h_ref[1, j])
        Y = _dft_dot(gi_ref[j], jnp.concatenate([pr, pi], axis=0))
        acc_ref[:, 0, j, :] = Y[:n2]
        acc_ref[:, 1, j, :] = Y[n2:]
    o_ref[0] = acc_ref[...].astype(o_ref.dtype)


def spectrum_multiply(g2, g2i, H, a, cl):
    B, _, N1, N2, C = a.shape
    kg = K1_GROUP
    gsp = pl.BlockSpec((kg, 2 * N2, 2 * N2), lambda k, c, b: (k, 0, 0))
    return pl.pallas_call(
        _spec_mul_kernel,
        grid=(N1 // kg, C // cl, B),
        in_specs=[gsp, gsp, pl.BlockSpec((2, kg, N2, cl), lambda k, c, b: (0, k, 0, c)),
                  pl.BlockSpec((1, 2, kg, N2, cl), lambda k, c, b: (b, 0, k, 0, c))],
        out_specs=pl.BlockSpec((1, N2, 2, kg, cl), lambda k, c, b: (b, 0, 0, k, c)),
        out_shape=jax.ShapeDtypeStruct((B, N2, 2, N1, C), DFT_DTYPE),
        scratch_shapes=[pltpu.VMEM((N2, 2, kg, cl), F32)],
        compiler_params=_cparams("parallel", "parallel", "parallel"),
        name="spectrum_multiply",
    )(g2, g2i, H, a)


def _idft_out_kernel(g_ref, y_ref, x0_ref, z_ref, bias_ref, o_ref, acc_ref):
    for s in range(y_ref.shape[1]):
        y = _dft_dot(g_ref[...], y_ref[0, s])
        acc_ref[:, s, :] = x0_ref[0, s] * (y + z_ref[0, s] * bias_ref[...])
    o_ref[0] = acc_ref[...].astype(o_ref.dtype)


def idft_gate_out(g4, yv, x0p, zp, bias, ns, cl):
    B, N2, K, C = yv.shape
    M = g4.shape[0]
    pdat = pl.BlockSpec((1, ns, M, cl), lambda b, j, c: (b, j, 0, c))
    return pl.pallas_call(
        _idft_out_kernel,
        grid=(B, N2 // ns, C // cl),
        in_specs=[pl.BlockSpec((M, K), lambda b, j, c: (0, 0)),
                  pl.BlockSpec((1, ns, K, cl), lambda b, j, c: (b, j, 0, c)),
                  pdat, pdat, pl.BlockSpec((1, cl), lambda b, j, c: (0, c))],
        out_specs=pl.BlockSpec((1, M, ns, cl), lambda b, j, c: (b, 0, j, c)),
        out_shape=jax.ShapeDtypeStruct((B, M, N2, C), BF16),
        scratch_shapes=[pltpu.VMEM((M, ns, cl), F32)],
        compiler_params=_cparams("parallel", "parallel", "parallel"),
        name="idft_gate_out",
    )(g4, yv, x0p, zp, bias.reshape(1, C))


def long_conv_two_stage(x0p, zp, hfbp, nrm, bias):
    B, N2, M, C = zp.shape
    L = N2 * M
    N1 = 2 * M
    g1, g2, g2i, g4 = _dft_tables(N1, N2)
    ns = HALO_ROWS
    cl = min(C, 512)
    a_f = dft_stage1(g1, hfbp, ns, cl)
    H = filter_spectrum(g2, a_f.reshape(2, N1, N2, 2 * C), nrm)
    a = dft_stage1(g1, zp, ns, cl)
    yv = spectrum_multiply(g2, g2i, H, a.reshape(B, 2, N1, N2, C), cl // 2)
    out = idft_gate_out(g4, yv.reshape(B, N2, 2 * N1, C), x0p, zp, bias, ns, cl)
    return out.reshape(B, L, C)


def _dense_tables(L):
    N = 2 * L
    ar = lambda n: jnp.arange(n, dtype=jnp.int32)
    c, s = _cos_sin((ar(N)[:, None] * ar(L)[None, :]) % N, N)
    g = jnp.concatenate([c, -s], axis=0)
    c, s = _cos_sin((ar(L)[:, None] * ar(N)[None, :]) % N, N)
    gi = jnp.concatenate([c, -s], axis=1) * (1.0 / N)
    return g.astype(DFT_DTYPE), gi.astype(DFT_DTYPE)


def _dense_spec_kernel(g_ref, hfb_ref, nrm_ref, o_ref):
    X = _dft_dot(g_ref[...], hfb_ref[...])
    hre, him = _combine_spectrum(X, nrm_ref, o_ref.shape[-1])
    o_ref[0] = hre
    o_ref[1] = him


def _dense_conv_kernel(g_ref, gi_ref, h_ref, z_ref, x0_ref, bias_ref, o_ref):
    z = z_ref[0]
    X = _dft_dot(g_ref[...], z)
    n = X.shape[0] // 2
    pr, pi = _cmul(X[:n], X[n:], h_ref[0], h_ref[1])
    y = _dft_dot(gi_ref[...], jnp.concatenate([pr, pi], axis=0))
    o_ref[0] = (x0_ref[0] * (y + z * bias_ref[...])).astype(o_ref.dtype)


def long_conv_dense(x0, z, hfb, nrm, bias):
    B, L, C = z.shape
    N = 2 * L
    g, gi = _dense_tables(L)
    full2 = lambda shape: pl.BlockSpec(shape, lambda *_: (0,) * len(shape))
    H = pl.pallas_call(
        _dense_spec_kernel,
        grid=(1,),
        in_specs=[full2((2 * N, L)), full2((L, 2 * C)), full2((1, 2 * C))],
        out_specs=full2((2, N, C)),
        out_shape=jax.ShapeDtypeStruct((2, N, C), F32),
        compiler_params=_cparams("arbitrary"),
        name="dense_filter_spectrum",
    )(g, hfb, nrm)
    dat = pl.BlockSpec((1, L, C), lambda b: (b, 0, 0))
    return pl.pallas_call(
        _dense_conv_kernel,
        grid=(B,),
        in_specs=[full2((2 * N, L)), full2((L, 2 * N)), full2((2, N, C)), dat, dat, full2((1, C))],
        out_specs=dat,
        out_shape=jax.ShapeDtypeStruct((B, L, C), BF16),
        compiler_params=_cparams("parallel"),
        name="dense_long_conv",
    )(g, gi, H, z, x0, bias.reshape(1, C))


def hyena_mix_pre(u_args, p, dense):
    x, g, shift, scale = u_args
    (w_in, b_in, sc_w, sc_b, f_w1, f_b1, f_w2, f_b2, f_w3, f_b3, f_wout, freq, bias) = p
    L = x.shape[1]
    perm_n2 = None if dense else V7X_LANES
    x0, z = hy_in_gate(x, g, shift, scale, w_in.astype(BF16), b_in, sc_w, sc_b, perm_n2)
    hfb, nrm = hyena_filter(L, f_w1, f_b1, f_w2, f_b2, f_w3, f_b3, f_wout, freq, perm_n2)
    conv = long_conv_dense if dense else long_conv_two_stage
    return conv(x0, z, hfb, nrm, bias)


def _ml_prep_kernel(xm_ref, prev_ref, next_ref, cw_ref, cb_ref, wq_ref, wk_ref, wv_ref, wg_ref, bg_ref,
                    q_ref, k_ref, kt_ref, v_ref, xc_ref, g_ref, *, k_scale):
    xm = xm_ref[0].astype(F32)
    inner = xm.shape[1]
    conv = _conv3_rows(xm, prev_ref, next_ref, cw_ref, cb_ref)
    xc = conv * jax.nn.sigmoid(conv)
    xcb = xc.astype(BF16)
    xc_ref[0] = xcb
    xmb = xm.astype(BF16)
    gw = V7X_MXU_DIM
    gates = bg_ref[...]
    for j in range(inner // gw):
        sl = slice(j * gw, (j + 1) * gw)
        q = _dot(xcb[:, sl], wq_ref[j])
        k = _dot(xcb[:, sl], wk_ref[j])
        v = _dot(xmb[:, sl], wv_ref[j])
        q_ref[0, :, sl] = q.astype(BF16)
        ks = k * k_scale
        k_ref[0, :, sl] = ks.astype(BF16)
        kt_ref[0, sl, :] = ks.T.astype(BF16)
        v_ref[0, :, sl] = v.astype(BF16)
        gates = gates + _dot(q.astype(BF16), wg_ref[j * gw:(j + 1) * gw, :])
        gates = gates + _dot(k.astype(BF16), wg_ref[inner + j * gw:inner + (j + 1) * gw, :])
        gates = gates + _dot(v.astype(BF16), wg_ref[2 * inner + j * gw:2 * inner + (j + 1) * gw, :])
    g_ref[0] = gates


def _block_diag(w, group):
    nb, bs, _ = w.shape
    per = group // bs
    w = w.reshape(nb // per, per, bs, bs)
    eye = jnp.eye(per, dtype=w.dtype)
    dense = jnp.einsum("gpce,pr->gpcre", w, eye)
    return dense.reshape(nb // per, group, group).astype(BF16)


def ml_prep(xz, conv_w, conv_b, wq, wk, wv, w_gate, b_gate, tm=256):
    B, L, C2 = xz.shape
    inner = C2 // 2
    dh = inner // ML_HEADS
    tm = _row_tile(L, tm)
    gw = V7X_MXU_DIM
    ng = inner // gw
    P = V7X_LANES
    wg = jnp.pad(w_gate, ((0, 0), (0, P - w_gate.shape[1]))).astype(BF16)
    bg = jnp.pad(b_gate.reshape(1, -1), ((0, 0), (0, P - b_gate.shape[0])))
    prev, nxt = _halo_specs(tm, L, inner, lambda: 0)
    c2 = lambda shape: pl.BlockSpec(shape, lambda b, i: (0,) * len(shape))
    row = lambda n: pl.BlockSpec((1, tm, n), lambda b, i: (b, i, 0))
    sd = lambda n, dt: jax.ShapeDtypeStruct((B, L, n), dt)
    return pl.pallas_call(
        functools.partial(_ml_prep_kernel, k_scale=dh ** -0.5),
        grid=(B, L // tm),
        in_specs=[row(inner), prev, nxt, c2((3, inner)), c2((1, inner)),
                  c2((ng, gw, gw)), c2((ng, gw, gw)), c2((ng, gw, gw)), c2((3 * inner, P)), c2((1, P))],
        out_specs=(row(inner), row(inner), pl.BlockSpec((1, inner, tm), lambda b, i: (b, 0, i)), row(inner),
                   row(inner), row(P)),
        out_shape=(sd(inner, BF16), sd(inner, BF16), jax.ShapeDtypeStruct((B, inner, L), BF16), sd(inner, BF16),
                   sd(inner, BF16), sd(P, F32)),
        compiler_params=_cparams("parallel", "arbitrary"),
        name="ml_prep",
    )(xz, xz, xz, conv_w, conv_b.reshape(1, inner), _block_diag(wq, gw), _block_diag(wk, gw),
      _block_diag(wv, gw), wg, bg)


def _gates_scan_order(g_ctx, g_lat):
    T = ML_CHUNK

    def lay(g):
        B, L, _ = g.shape
        g = g[..., :4 * ML_HEADS].reshape(B, L // T, T, 2, 2, ML_HEADS)
        return jnp.transpose(g, (3, 0, 5, 4, 1, 2))

    gc, gl = lay(g_ctx), lay(g_lat)
    fwd = jnp.concatenate([gc[0], gl[0]], axis=3)
    bwd = jnp.concatenate([gc[1][..., ::-1, :], gl[1][..., ::-1, :]], axis=3)
    out = jnp.stack([fwd, bwd])
    nct = out.shape[4]
    pad = -nct % V7X_SUBLANES
    return jnp.pad(out, ((0, 0),) * 4 + ((0, pad), (0, 0))), nct


AUX_BMR, AUX_GR, AUX_WINTER, AUX_EMR, AUX_WROW, AUX_GOLD = range(6)


def _ml_gates_kernel(g_ref, aux_ref, bl_scr, me_scr, mp_scr):
    d = pl.program_id(0)
    ig = g_ref[0, 0, 0, 0]
    fg = g_ref[0, 0, 0, 1]
    nct, T = ig.shape
    lf = jnp.minimum(fg, 0.0) - jnp.log1p(jnp.exp(-jnp.abs(fg)))
    lane = lax.broadcasted_iota(jnp.int32, (nct, T), 1)
    rev = d == 1

    def scans(x, op, ident):
        f, r = x, x
        s = 1
        while s < T:
            f = op(f, jnp.where(lane >= s, pltpu.roll(f, s, 1), ident))
            r = op(r, jnp.where(lane < T - s, pltpu.roll(r, T - s, 1), ident))
            s *= 2
        return jnp.where(rev, r, f)

    bcs = scans(lf, jnp.add, 0.0)
    b_last = jnp.sum(lf, axis=1, keepdims=True)
    gr = ig - bcs
    cmax = scans(gr, jnp.maximum, -jnp.inf)
    max_e = b_last + jnp.max(gr, axis=1, keepdims=True)
    bl_scr[...] = jnp.broadcast_to(b_last, bl_scr.shape)
    me_scr[...] = jnp.broadcast_to(max_e, me_scr.shape)

    def body(t, m):
        mp_scr[pl.ds(t, 1), :] = m
        return jnp.maximum(bl_scr[pl.ds(t, 1), :] + m, me_scr[pl.ds(t, 1), :])

    lax.fori_loop(0, nct, body, jnp.zeros((1, V7X_LANES), F32))
    m_prev = mp_scr[:, 0:1]
    a_inter = bcs + m_prev
    m_row = jnp.maximum(a_inter, bcs + cmax)
    m_new = jnp.maximum(b_last + m_prev, max_e)
    rows = {AUX_BMR: bcs - m_row, AUX_GR: gr, AUX_WINTER: jnp.exp(a_inter - m_row),
            AUX_EMR: jnp.exp(-m_row), AUX_WROW: jnp.exp(b_last + gr - m_new),
            AUX_GOLD: jnp.broadcast_to(jnp.exp(b_last + m_prev - m_new), (nct, T))}
    zero = jnp.zeros((nct, T), F32)
    for k in range(V7X_SUBLANES):
        aux_ref[0, 0, 0, :, k, :] = rows.get(k, zero)


def ml_gates(gates):
    _, B, H, _, nct, T = gates.shape
    return pl.pallas_call(
        _ml_gates_kernel,
        grid=(2, B, H),
        in_specs=[pl.BlockSpec((1, 1, 1, 2, nct, T), lambda d, b, h: (d, b, h, 0, 0, 0))],
        out_specs=pl.BlockSpec((1, 1, 1, nct, V7X_SUBLANES, T), lambda d, b, h: (d, b, h, 0, 0, 0)),
        out_shape=jax.ShapeDtypeStruct((2, B, H, nct, V7X_SUBLANES, T), F32),
        scratch_shapes=[pltpu.VMEM((nct, V7X_LANES), F32)] * 3,
        compiler_params=_cparams("arbitrary", "arbitrary", "arbitrary"),
        name="ml_gates",
    )(gates)


SCAN_HEADS = 4


def _mlstm_kernel(qf_ref, kf_ref, ktf_ref, vf_ref, qb_ref, kb_ref, ktb_ref, vb_ref, aux_ref, c0_ref, n0_ref,
                  hf_ref, hb_ref, cf_ref, nf_ref, c_scr, n_scr):
    t = pl.program_id(2)
    nct = pl.num_programs(2)
    T = ML_CHUNK
    dh = qf_ref.shape[-1] // SCAN_HEADS

    @pl.when(t == 0)
    def _():
        c_scr[...] = c0_ref[:, 0]
        n_scr[...] = n0_ref[:, 0]

    row = lax.broadcasted_iota(jnp.int32, (T, T), 0)
    col = lax.broadcasted_iota(jnp.int32, (T, T), 1)
    dirs = ((qf_ref, kf_ref, ktf_ref, vf_ref, hf_ref, col <= row), (qb_ref, kb_ref, ktb_ref, vb_ref, hb_ref, col >= row))
    streams = [(d, j) + dirs[d] for j in range(SCAN_HEADS) for d in range(2)]
    for d, j, q_ref, k_ref, kt_ref, v_ref, h_ref, mask in streams:
        hs = slice(j * dh, (j + 1) * dh)
        ax = aux_ref[d, 0, j, 0]
        axt = jnp.concatenate([ax, jnp.zeros((T - V7X_SUBLANES, T), F32)], axis=0).T
        gr = ax[AUX_GR:AUX_GR + 1, :]
        g_old = ax[AUX_GOLD:AUX_GOLD + 1, 0:1]
        bmr = axt[:, AUX_BMR:AUX_BMR + 1]
        w_inter = axt[:, AUX_WINTER:AUX_WINTER + 1]
        emr = axt[:, AUX_EMR:AUX_EMR + 1]
        w_col = axt[:, AUX_WROW:AUX_WROW + 1]
        pmat = jnp.exp(jnp.where(mask, bmr + gr, -jnp.inf))

        q = q_ref[0, :, hs]
        v = v_ref[0, :, hs]
        kT = kt_ref[0, hs, :]
        n_row = n_scr[d, j, 0:1, :]
        smat = _dot(q, kT) * pmat
        qn = jnp.sum(q.astype(F32) * n_row, axis=1, keepdims=True)
        den = w_inter * qn + jnp.sum(smat, axis=1, keepdims=True)
        num = w_inter * _dot(q, c_scr[d, j].astype(BF16)) + _dot(smat.astype(BF16), v)
        h_ref[0, :, hs] = (num * (1.0 / jnp.maximum(jnp.abs(den), emr))).astype(h_ref.dtype)
        c_scr[d, j] = g_old * c_scr[d, j] + _dot(kT, (v.astype(F32) * w_col).astype(BF16))
        dn = jnp.sum(k_ref[0, :, hs].astype(F32) * w_col, axis=0, keepdims=True)
        n_scr[d, j, 0:1, :] = g_old * n_row + dn

    @pl.when(t == nct - 1)
    def _():
        cf_ref[:, 0] = c_scr[...]
        nf_ref[:, 0] = n_scr[...]


def mlstm_scan(q, k, kt, v, aux, chunk0, c0, n0):
    B, L, inner = q.shape
    H = ML_HEADS
    dh = inner // H
    T = ML_CHUNK
    nc = L // T
    hp = SCAN_HEADS
    fw = pl.BlockSpec((1, T, hp * dh), lambda b, h, t: (b, t, h))
    bw = pl.BlockSpec((1, T, hp * dh), lambda b, h, t: (b, nc - 1 - t, h))
    fwt = pl.BlockSpec((1, hp * dh, T), lambda b, h, t: (b, h, t))
    bwt = pl.BlockSpec((1, hp * dh, T), lambda b, h, t: (b, h, nc - 1 - t))
    cst = pl.BlockSpec((2, 1, hp, dh, dh), lambda b, h, t: (0, b, h, 0, 0))
    nst = pl.BlockSpec((2, 1, hp, V7X_SUBLANES, dh), lambda b, h, t: (0, b, h, 0, 0))
    hshape = jax.ShapeDtypeStruct((B, L, inner), BF16)
    return pl.pallas_call(
        _mlstm_kernel,
        grid=(B, H // hp, nc),
        in_specs=[fw, fw, fwt, fw, bw, bw, bwt, bw,
                  pl.BlockSpec((2, 1, hp, 1, V7X_SUBLANES, T), lambda b, h, t: (0, b, h, chunk0 + t, 0, 0)),
                  cst, nst],
        out_specs=(fw, bw, cst, nst),
        out_shape=(hshape, hshape, jax.ShapeDtypeStruct(c0.shape, F32), jax.ShapeDtypeStruct(n0.shape, F32)),
        scratch_shapes=[pltpu.VMEM((2, hp, dh, dh), F32), pltpu.VMEM((2, hp, V7X_SUBLANES, dh), F32)],
        compiler_params=_cparams("parallel", "parallel", "arbitrary"),
        name="mlstm_scan",
    )(q, k, kt, v, q, k, kt, v, aux, c0, n0)


def _ml_out_down_kernel(hf_ref, hb_ref, xc_ref, z_ref, nw_ref, sk_ref, w_ref, gate_ref, res_ref, o_ref, a_scr):
    h = hf_ref[0].astype(F32) + hb_ref[0].astype(F32)
    dh = h.shape[1] // ML_HEADS
    z = z_ref[0].astype(F32)
    gate = z * jax.nn.sigmoid(z)
    for j in range(ML_HEADS):
        sl = slice(j * dh, (j + 1) * dh)
        seg = h[:, sl]
        mu = jnp.mean(seg, axis=-1, keepdims=True)
        cen = seg - mu
        var = jnp.mean(cen * cen, axis=-1, keepdims=True)
        hn = cen * lax.rsqrt(var + ML_NORM_EPS)
        hs = hn * nw_ref[:, sl] + sk_ref[:, sl] * xc_ref[0, :, sl].astype(F32)
        a_scr[:, sl] = (hs * gate[:, sl]).astype(BF16)
    o_ref[0] = res_ref[0] + gate_ref[0] * _dot(a_scr[...], w_ref[...])


def ml_out_down(hf, hb, xc, xz, norm_w, skip, w_down_b, gate, res, tm=256):
    B, L, inner = hf.shape
    D = w_down_b.shape[1]
    tm = _row_tile(L, tm)
    vec = pl.BlockSpec((1, inner), lambda b, i: (0, 0))
    dat = pl.BlockSpec((1, tm, inner), lambda b, i: (b, i, 0))
    row = pl.BlockSpec((1, tm, D), lambda b, i: (b, i, 0))
    return pl.pallas_call(
        _ml_out_down_kernel,
        grid=(B, L // tm),
        in_specs=[dat, dat, dat, pl.BlockSpec((1, tm, inner), lambda b, i: (b, i, 1)), vec, vec,
                  pl.BlockSpec((inner, D), lambda b, i: (0, 0)), pl.BlockSpec((1, 1, D), lambda b, i: (b, 0, 0)), row],
        out_specs=row,
        out_shape=jax.ShapeDtypeStruct((B, L, D), F32),
        scratch_shapes=[pltpu.VMEM((tm, inner), BF16)],
        compiler_params=_cparams("parallel", "parallel"),
        name="ml_out_down",
    )(hf, hb, xc, xz, norm_w.reshape(1, inner), skip.reshape(1, inner), w_down_b, gate, res)


def mlstm_mix_residual(lat_args, ctx_args, p, gate):
    w_in, conv_w, conv_b, wq, wk, wv, w_gate, b_gate, norm_w, skip, w_down = p
    w_in_b = w_in.astype(BF16)
    zero_b = jnp.zeros((w_in.shape[1],), F32)

    def prep(args):
        x, g, shift, scale = args
        xz = norm_mod_matmul(x, g, shift, scale, w_in_b, zero_b, tm=512)
        return ml_prep(xz, conv_w, conv_b, wq, wk, wv, w_gate, b_gate) + (xz,)

    qc, kc, ktc, vc, _, gates_c, _ = prep(ctx_args)
    q, k, kt, v, xc, gates_l, xz = prep(lat_args)
    B, _, inner = q.shape
    dh = inner // ML_HEADS
    gates, _ = _gates_scan_order(gates_c, gates_l)
    aux = ml_gates(gates)
    c0 = jnp.zeros((2, B, ML_HEADS, dh, dh), F32)
    n0 = jnp.zeros((2, B, ML_HEADS, V7X_SUBLANES, dh), F32)
    _, _, c1, n1 = mlstm_scan(qc, kc, ktc, vc, aux, 0, c0, n0)
    hf, hb, _, _ = mlstm_scan(q, k, kt, v, aux, qc.shape[1] // ML_CHUNK, c1, n1)
    return ml_out_down(hf, hb, xc, xz, norm_w, skip, w_down.astype(BF16), gate, lat_args[0])


def _ffn_kernel(x_ref, xp_ref, xn_ref, ng_ref, sh_ref, sc_ref, wu_ref, cw_ref, cb_ref, wd_ref,
                gate_ref, fg_ref, o_ref, u_scr, *, cols, vertical, final_norm, cb):
    i = pl.program_id(1)
    last = pl.num_programs(1) - 1
    tm = x_ref.shape[1]
    F = wd_ref.shape[0]
    halo = cols if vertical else 0

    def norm_mod(x):
        y = x * lax.rsqrt(jnp.mean(x * x, axis=-1, keepdims=True) + EPS)
        return ((y * ng_ref[...]) * (1.0 + sc_ref[0]) + sh_ref[0]).astype(BF16)

    u_scr[halo:halo + tm] = norm_mod(x_ref[0])
    if vertical:
        u_scr[0:halo] = norm_mod(xp_ref[0])
        u_scr[halo + tm:] = norm_mod(xn_ref[0])
    R = tm + 2 * halo
    rowi = lax.broadcasted_iota(jnp.int32, (R, 1), 0)
    cpos = jnp.bitwise_and(rowi, cols - 1)
    if vertical:
        top_ok = jnp.where(i > 0, 1.0, 0.0)
        bot_ok = jnp.where(i < last, 1.0, 0.0)
        rowmask = jnp.where(rowi < halo, top_ok, jnp.where(rowi >= halo + tm, bot_ok, 1.0))
    acc = jnp.zeros((tm, o_ref.shape[-1]), F32)
    for f in range(F // cb):
        fs = slice(f * cb, (f + 1) * cb)
        g = _dot(u_scr[...], wu_ref[:, F + f * cb:F + (f + 1) * cb])
        a = _dot(u_scr[halo:halo + tm], wu_ref[:, fs])
        if vertical:
            g = g * rowmask
        left = jnp.where(cpos == 0, 0.0, pltpu.roll(g, 1, 0))
        right = jnp.where(cpos == cols - 1, 0.0, pltpu.roll(g, R - 1, 0))
        conv = cb_ref[:, fs]
        for dr in (range(3) if vertical else (1,)):
            sl = slice(dr * halo, dr * halo + tm)
            conv = conv + (cw_ref[3 * dr:3 * dr + 1, fs] * left[sl] + cw_ref[3 * dr + 1:3 * dr + 2, fs] * g[sl]
                           + cw_ref[3 * dr + 2:3 * dr + 3, fs] * right[sl])
        act = ((conv * jax.nn.sigmoid(conv)) * a).astype(BF16)
        acc = acc + _dot(act, wd_ref[fs, :])
    x = x_ref[0] + gate_ref[0] * acc
    if final_norm:
        x = (x * lax.rsqrt(jnp.mean(x * x, axis=-1, keepdims=True) + EPS)) * fg_ref[...]
    o_ref[0] = x


def conv_ffn_residual(x, g, shift, scale, gate, w_up_b, conv_w, conv_b, w_down_b, rows, cols, final_g=None,
                      tm=512):
    B, L, D = x.shape
    F = w_down_b.shape[0]
    assert cols & (cols - 1) == 0 and rows * cols == L
    vertical = rows > 1
    tm = _row_tile(L, tm) if vertical else L
    cb = V7X_MXU_DIM
    assert F % cb == 0 and tm % cols == 0
    hb = cols if vertical else HALO_ROWS
    nhb = L // hb
    final_norm = final_g is not None
    fg = final_g if final_norm else jnp.ones((D,), F32)
    vecb = pl.BlockSpec((1, 1, D), lambda b, i: (b, 0, 0))
    full = lambda shape: pl.BlockSpec(shape, lambda b, i: (0, 0))
    row = pl.BlockSpec((1, tm, D), lambda b, i: (b, i, 0))
    return pl.pallas_call(
        functools.partial(_ffn_kernel, cols=cols, vertical=vertical, final_norm=final_norm, cb=cb),
        grid=(B, L // tm),
        in_specs=[row,
                  pl.BlockSpec((1, hb, D), lambda b, i: (b, jnp.maximum(i * (tm // hb) - 1, 0), 0)),
                  pl.BlockSpec((1, hb, D), lambda b, i: (b, jnp.minimum((i + 1) * (tm // hb), nhb - 1), 0)),
                  full((1, D)), vecb, vecb, full((D, 2 * F)), full((9, F)), full((1, F)), full((F, D)),
                  vecb, full((1, D))],
        out_specs=row,
        out_shape=jax.ShapeDtypeStruct((B, L, D), F32),
        scratch_shapes=[pltpu.VMEM((tm + (2 * cols if vertical else 0), D), BF16)],
        compiler_params=_cparams("parallel", "arbitrary"),
        name="conv_ffn",
    )(x, x, x, g.reshape(1, D), shift, scale, w_up_b, conv_w.reshape(9, F), conv_b.reshape(1, F),
      w_down_b, gate, fg.reshape(1, D))


def kernel(x, c, ctx, c_ctx, mod_w, mod_b, norm_g, final_g, hy_w_in, hy_b_in, hy_sc_w, hy_sc_b, hy_f_w1, hy_f_b1, hy_f_w2, hy_f_b2, hy_f_w3, hy_f_b3, hy_f_wout, hy_freq, hy_bias, hy_w_out, hy_b_out, ml_w_in, ml_conv_w, ml_conv_b, ml_wq, ml_wk, ml_wv, ml_w_gate, ml_b_gate, ml_norm_w, ml_skip, ml_w_down, ffn_w_up, ffn_conv_w, ffn_conv_b, ffn_w_down):
    B, L, D = x.shape
    ctx_len = ctx.shape[1]
    depth = mod_w.shape[0]
    n_mixers = 2
    rows = L // GRID_W
    hy_params = (hy_w_in, hy_b_in, hy_sc_w, hy_sc_b, hy_f_w1, hy_f_b1, hy_f_w2, hy_f_b2,
                 hy_f_w3, hy_f_b3, hy_f_wout, hy_freq, hy_bias)
    ml_params = (ml_w_in, ml_conv_w, ml_conv_b, ml_wq, ml_wk, ml_wv, ml_w_gate, ml_b_gate,
                 ml_norm_w, ml_skip, ml_w_down)
    cond = jnp.concatenate([c, c_ctx.reshape(1, D), jnp.zeros((V7X_SUBLANES - B - 1, D), F32)], axis=0)
    for i in range(depth):
        last = i == depth - 1
        mod = adaln(cond, mod_w[i], mod_b[i])
        lat = [mod[:B, k * D:(k + 1) * D].reshape(B, 1, D) for k in range(6)]
        cm = [jnp.broadcast_to(mod[B:B + 1, k * D:(k + 1) * D].reshape(1, 1, D), (B, 1, D)) for k in range(6)]
        lat_args = (x, norm_g[i, 0], lat[0], lat[1])
        ctx_args = (ctx, norm_g[i, 0], cm[0], cm[1])
        j = i // n_mixers
        if i % n_mixers == 0:
            p = tuple(a[j] for a in hy_params)
            w_out_b = hy_w_out[j].astype(BF16)
            x = res_gate_matmul(hyena_mix_pre(lat_args, p, dense=False), w_out_b, hy_b_out[j], lat[2], x)
            if not last:
                ctx = res_gate_matmul(hyena_mix_pre(ctx_args, p, dense=True), w_out_b, hy_b_out[j], cm[2], ctx)
        else:
            assert last, "the mLSTM mixer is only implemented for the last layer (no context output)"
            p = tuple(a[j] for a in ml_params)
            x = mlstm_mix_residual(lat_args, ctx_args, p, lat[2])
        w_up_b = ffn_w_up[i].astype(BF16)
        w_down_b = ffn_w_down[i].astype(BF16)
        x = conv_ffn_residual(x, norm_g[i, 1], lat[3], lat[4], lat[5], w_up_b, ffn_conv_w[i], ffn_conv_b[i],
                              w_down_b, rows, GRID_W, final_g if last else None)
        if not last:
            ctx = conv_ffn_residual(ctx, norm_g[i, 1], cm[3], cm[4], cm[5], w_up_b, ffn_conv_w[i],
                                    ffn_conv_b[i], w_down_b, 1, ctx_len)
    return x
```

```python
import functools
import math

import jax
import jax.numpy as jnp
from jax import lax
from jax.experimental import pallas as pl
from jax.experimental.pallas import tpu as pltpu

F32 = jnp.float32
BF16 = jnp.bfloat16
HIGHEST = lax.Precision.HIGHEST

EPS = 1e-6
ML_NORM_EPS = 1e-5
GRID_W = 64
ML_HEADS = 4
ML_CHUNK = 128
ML_QKV_BLOCK = 4
HY_EMB_BANDS = 16
HY_MAX_DECAY = math.log(1e-2) / 0.3
HY_MIN_DECAY = math.log(1e-2) / 1.5

V7X_LANES = 128
V7X_SUBLANES = 8
V7X_MXU_DIM = 256
V7X_VMEM_BYTES = 64 * 1024 * 1024
VMEM_LIMIT = V7X_VMEM_BYTES - 8 * 1024 * 1024
HALO_ROWS = 16


def _cparams(*sem):
    return pltpu.CompilerParams(dimension_semantics=("arbitrary",) * len(sem), vmem_limit_bytes=VMEM_LIMIT)


def _dot(a, b, precision=None):
    return jnp.dot(a, b, preferred_element_type=F32, precision=precision)


def _row_tile(n, target):
    t = min(n, target)
    assert n % t == 0, (n, t)
    return t


def _adaln_kernel(c_ref, w_ref, b_ref, o_ref):
    c = c_ref[...]
    s = c * jax.nn.sigmoid(c)
    o_ref[...] = _dot(s, w_ref[...], HIGHEST) + b_ref[...]


def adaln(cond8, w, b):
    r, d = cond8.shape
    n = w.shape[1]
    tn = _row_tile(n, 1536)
    return pl.pallas_call(
        _adaln_kernel,
        grid=(n // tn,),
        in_specs=[pl.BlockSpec((r, d), lambda j: (0, 0)),
                  pl.BlockSpec((d, tn), lambda j: (0, j)),
                  pl.BlockSpec((1, tn), lambda j: (0, j))],
        out_specs=pl.BlockSpec((r, tn), lambda j: (0, j)),
        out_shape=jax.ShapeDtypeStruct((r, n), F32),
        compiler_params=_cparams("arbitrary"),
        name="adaln",
    )(cond8, w, b.reshape(1, n))


def _nmm_kernel(x_ref, g_ref, sh_ref, sc_ref, w_ref, b_ref, o_ref):
    x = x_ref[0]
    y = x * lax.rsqrt(jnp.mean(x * x, axis=-1, keepdims=True) + EPS)
    u = (y * g_ref[...]) * (1.0 + sc_ref[0]) + sh_ref[0]
    acc = _dot(u.astype(BF16), w_ref[...])
    o_ref[0] = (acc + b_ref[...]).astype(o_ref.dtype)


def norm_mod_matmul(x, g, shift, scale, w_bf16, bias, tm, out_dtype=BF16):
    B, L, D = x.shape
    n = w_bf16.shape[1]
    tm = _row_tile(L, tm)
    return pl.pallas_call(
        _nmm_kernel,
        grid=(B, L // tm),
        in_specs=[pl.BlockSpec((1, tm, D), lambda b, i: (b, i, 0)),
                  pl.BlockSpec((1, D), lambda b, i: (0, 0)),
                  pl.BlockSpec((1, 1, D), lambda b, i: (b, 0, 0)),
                  pl.BlockSpec((1, 1, D), lambda b, i: (b, 0, 0)),
                  pl.BlockSpec((D, n), lambda b, i: (0, 0)),
                  pl.BlockSpec((1, n), lambda b, i: (0, 0))],
        out_specs=pl.BlockSpec((1, tm, n), lambda b, i: (b, i, 0)),
        out_shape=jax.ShapeDtypeStruct((B, L, n), out_dtype),
        compiler_params=_cparams("parallel", "parallel"),
        name="norm_mod_matmul",
    )(x, g.reshape(1, D), shift, scale, w_bf16, bias.reshape(1, n))


def _rgm_kernel(a_ref, w_ref, b_ref, gate_ref, res_ref, fg_ref, o_ref, *, final_norm):
    acc = _dot(a_ref[0], w_ref[...]) + b_ref[...]
    x = res_ref[0] + gate_ref[0] * acc
    if final_norm:
        x = (x * lax.rsqrt(jnp.mean(x * x, axis=-1, keepdims=True) + EPS)) * fg_ref[...]
    o_ref[0] = x


def res_gate_matmul(a_bf16, w_bf16, bias, gate, res, final_g=None, tm=512):
    B, L, K = a_bf16.shape
    D = w_bf16.shape[1]
    tm = _row_tile(L, tm)
    final_norm = final_g is not None
    fg = final_g if final_norm else jnp.ones((D,), F32)
    return pl.pallas_call(
        functools.partial(_rgm_kernel, final_norm=final_norm),
        grid=(B, L // tm),
        in_specs=[pl.BlockSpec((1, tm, K), lambda b, i: (b, i, 0)),
                  pl.BlockSpec((K, D), lambda b, i: (0, 0)),
                  pl.BlockSpec((1, D), lambda b, i: (0, 0)),
                  pl.BlockSpec((1, 1, D), lambda b, i: (b, 0, 0)),
                  pl.BlockSpec((1, tm, D), lambda b, i: (b, i, 0)),
                  pl.BlockSpec((1, D), lambda b, i: (0, 0))],
        out_specs=pl.BlockSpec((1, tm, D), lambda b, i: (b, i, 0)),
        out_shape=jax.ShapeDtypeStruct((B, L, D), F32),
        compiler_params=_cparams("parallel", "parallel"),
        name="res_gate_matmul",
    )(a_bf16, w_bf16, bias.reshape(1, D), gate, res, fg.reshape(1, D))


def _halo_specs(tm, L, C, cmap):
    r = HALO_ROWS
    nb = L // r
    prev = pl.BlockSpec((1, r, C), lambda b, i, *a: (b, jnp.maximum(i * (tm // r) - 1, 0), cmap(*a)))
    nxt = pl.BlockSpec((1, r, C), lambda b, i, *a: (b, jnp.minimum((i + 1) * (tm // r), nb - 1), cmap(*a)))
    return prev, nxt


def _conv3_rows(x, prev_ref, next_ref, w_ref, b_ref):
    tm = x.shape[0]
    i = pl.program_id(1)
    last = pl.num_programs(1) - 1
    prev_row = jnp.where(i > 0, prev_ref[0].astype(F32)[HALO_ROWS - 1:HALO_ROWS, :], 0.0)
    next_row = jnp.where(i < last, next_ref[0].astype(F32)[0:1, :], 0.0)
    row = lax.broadcasted_iota(jnp.int32, x.shape, 0)
    xm1 = jnp.where(row == 0, prev_row, pltpu.roll(x, 1, 0))
    xp1 = jnp.where(row == tm - 1, next_row, pltpu.roll(x, tm - 1, 0))
    return w_ref[0:1, :] * xm1 + w_ref[1:2, :] * x + w_ref[2:3, :] * xp1 + b_ref[...]


def _store_rows(o_ref, val, perm, cs=slice(None)):
    if not perm:
        o_ref[0, :, cs] = val
        return
    n2 = o_ref.shape[1]
    for j in range(o_ref.shape[2]):
        o_ref[0, :, j, cs] = val[j * n2:(j + 1) * n2]


def _x_halo_specs(tm, L, D):
    r = HALO_ROWS
    nb = L // r
    return (pl.BlockSpec((1, tm, D), lambda b, i: (b, i, 0)),
            pl.BlockSpec((1, r, D), lambda b, i: (b, jnp.maximum(i * (tm // r) - 1, 0), 0)),
            pl.BlockSpec((1, r, D), lambda b, i: (b, jnp.minimum((i + 1) * (tm // r), nb - 1), 0)))


def _norm_mod_ext(x_ref, xp_ref, xn_ref, ng_ref, sh_ref, sc_ref, u_scr):
    tm = x_ref.shape[1]
    h = HALO_ROWS

    def norm_mod(x):
        y = x * lax.rsqrt(jnp.mean(x * x, axis=-1, keepdims=True) + EPS)
        return ((y * ng_ref[...]) * (1.0 + sc_ref[0]) + sh_ref[0]).astype(BF16)

    u_scr[0:h] = norm_mod(xp_ref[0])
    u_scr[h:h + tm] = norm_mod(x_ref[0])
    u_scr[h + tm:] = norm_mod(xn_ref[0])
    i = pl.program_id(1)
    rowi = lax.broadcasted_iota(jnp.int32, (tm + 2 * h, 1), 0)
    top_ok = jnp.where(i > 0, 1.0, 0.0)
    bot_ok = jnp.where(i < pl.num_programs(1) - 1, 1.0, 0.0)
    return jnp.where(rowi < h, top_ok, jnp.where(rowi >= h + tm, bot_ok, 1.0))


def _conv3_ext(p, w_ref, b_ref, cs, tm):
    h = HALO_ROWS
    R = p.shape[0]
    return (w_ref[0:1, cs] * pltpu.roll(p, 1, 0)[h:h + tm] + w_ref[1:2, cs] * p[h:h + tm]
            + w_ref[2:3, cs] * pltpu.roll(p, R - 1, 0)[h:h + tm] + b_ref[:, cs])


def _hy_in_gate_kernel(x_ref, xp_ref, xn_ref, ng_ref, sh_ref, sc_ref, w_ref, b_ref, cw_ref, cb_ref,
                       x0_ref, z_ref, u_scr, *, perm, cb):
    tm, D = x_ref.shape[1:]
    rowmask = _norm_mod_ext(x_ref, xp_ref, xn_ref, ng_ref, sh_ref, sc_ref, u_scr)
    for c in range(D // cb):
        conv = []
        for k in range(3):
            cs = slice(k * D + c * cb, k * D + (c + 1) * cb)
            p = (_dot(u_scr[...], w_ref[:, cs]) + b_ref[:, cs]) * rowmask
            conv.append(_conv3_ext(p, cw_ref, cb_ref, cs, tm))
        cs = slice(c * cb, (c + 1) * cb)
        _store_rows(x0_ref, conv[0], perm, cs)
        _store_rows(z_ref, conv[1] * conv[2], perm, cs)


def hy_in_gate(x, g, shift, scale, w_bf16, b_in, sc_w, sc_b, perm_n2=None):
    B, L, D = x.shape
    C3 = w_bf16.shape[1]
    perm = perm_n2 is not None
    tm = V7X_SUBLANES * perm_n2 if perm else L
    assert L % tm == 0
    full = lambda shape: pl.BlockSpec(shape, lambda b, i: (0, 0))
    vecb = pl.BlockSpec((1, 1, D), lambda b, i: (b, 0, 0))
    if perm:
        out = jax.ShapeDtypeStruct((B, perm_n2, L // perm_n2, D), F32)
        ospec = pl.BlockSpec((1, perm_n2, V7X_SUBLANES, D), lambda b, i: (b, 0, i, 0))
    else:
        out = jax.ShapeDtypeStruct((B, L, D), F32)
        ospec = pl.BlockSpec((1, tm, D), lambda b, i: (b, i, 0))
    return pl.pallas_call(
        functools.partial(_hy_in_gate_kernel, perm=perm, cb=D),
        grid=(B, L // tm),
        in_specs=[*_x_halo_specs(tm, L, D), full((1, D)), vecb, vecb, full((D, C3)), full((1, C3)),
                  full((3, C3)), full((1, C3))],
        out_specs=(ospec, ospec),
        out_shape=(out, out),
        scratch_shapes=[pltpu.VMEM((tm + 2 * HALO_ROWS, D), BF16)],
        compiler_params=_cparams("parallel", "arbitrary"),
        name="hy_in_gate",
    )(x, x, x, g.reshape(1, D), shift, scale, w_bf16, b_in.reshape(1, C3), sc_w, sc_b.reshape(1, C3))


def _filter_kernel(w1_ref, b1_ref, w2_ref, b2_ref, w3_ref, b3_ref, wo_ref, fr_ref, band_ref,
                   h_ref, s_ref, *, L, D, tl, perm):
    i = pl.program_id(0)
    rowi = lax.broadcasted_iota(jnp.int32, (tl, 1), 0) + i * tl
    row = rowi.astype(F32)
    t = row / (L - 1.0)
    w = (2.0 * math.pi) * row / L
    col = lax.broadcasted_iota(jnp.int32, (tl, V7X_LANES), 1)
    ang = w * band_ref[...]
    pos = jnp.where(col == 0, t,
                    jnp.where(col <= HY_EMB_BANDS, jnp.cos(ang),
                              jnp.where(col <= 2 * HY_EMB_BANDS, -jnp.sin(ang), 0.0)))
    fr = fr_ref[...]
    h = jnp.sin(fr * (_dot(pos, w1_ref[...], HIGHEST) + b1_ref[...]))
    h = jnp.sin(fr * (_dot(h, w2_ref[...], HIGHEST) + b2_ref[...]))
    h = jnp.sin(fr * (_dot(h, w3_ref[...], HIGHEST) + b3_ref[...]))
    h = _dot(h, wo_ref[...], HIGHEST)
    dcol = lax.broadcasted_iota(jnp.int32, (1, 2 * D), 1)
    chan = jnp.where(dcol >= D, dcol - D, dcol).astype(F32)
    deltas = HY_MIN_DECAY + chan * ((HY_MAX_DECAY - HY_MIN_DECAY) / (D - 1.0))
    h = h * jnp.exp(-t * jnp.abs(deltas))
    h = jnp.where((rowi == 0) & (dcol >= D), 0.0, h)
    if perm:
        _store_rows(h_ref, h, True)
    else:
        h_ref[...] = h
    part = jnp.sum(jnp.abs(h), axis=0, keepdims=True)

    @pl.when(i == 0)
    def _():
        s_ref[...] = part

    @pl.when(i > 0)
    def _():
        s_ref[...] += part


def hyena_filter(L, f_w1, f_b1, f_w2, f_b2, f_w3, f_b3, f_wout, freq, perm_n2=None):
    W = f_w2.shape[0]
    D2 = f_wout.shape[1]
    P = V7X_LANES
    pad2 = lambda a: jnp.pad(a, ((0, P - a.shape[0]), (0, P - a.shape[1])))
    padv = lambda a: jnp.pad(a.reshape(1, -1), ((0, 0), (0, P - a.shape[0])))
    w1 = pad2(f_w1)
    w2 = pad2(f_w2)
    w3 = pad2(f_w3)
    wo = jnp.pad(f_wout, ((0, P - W), (0, 0)))
    bands = jnp.linspace(1e-4, HY_EMB_BANDS - 1, HY_EMB_BANDS, dtype=F32)
    band_row = jnp.concatenate([jnp.zeros((1,), F32), bands, bands,
                                jnp.zeros((P - 1 - 2 * HY_EMB_BANDS,), F32)]).reshape(1, P)
    perm = perm_n2 is not None
    tl = V7X_SUBLANES * perm_n2 if perm else _row_tile(L, 512)
    assert L % tl == 0
    full = lambda shape: pl.BlockSpec(shape, lambda i: (0, 0))
    if perm:
        hshape = jax.ShapeDtypeStruct((1, perm_n2, L // perm_n2, D2), F32)
        hspec = pl.BlockSpec((1, perm_n2, V7X_SUBLANES, D2), lambda i: (0, 0, i, 0))
    else:
        hshape = jax.ShapeDtypeStruct((L, D2), F32)
        hspec = pl.BlockSpec((tl, D2), lambda i: (i, 0))
    return pl.pallas_call(
        functools.partial(_filter_kernel, L=L, D=D2 // 2, tl=tl, perm=perm),
        grid=(L // tl,),
        in_specs=[full((P, P)), full((1, P)), full((P, P)), full((1, P)), full((P, P)), full((1, P)),
                  full((P, D2)), full((1, P)), full((1, P))],
        out_specs=(hspec, pl.BlockSpec((1, D2), lambda i: (0, 0))),
        out_shape=(hshape, jax.ShapeDtypeStruct((1, D2), F32)),
        compiler_params=_cparams("arbitrary"),
        name="hyena_filter",
    )(w1, padv(f_b1), w2, padv(f_b2), w3, padv(f_b3), wo, padv(freq), band_row)


DFT_DTYPE = BF16
K1_GROUP = HALO_ROWS


def _cos_sin(p, n):
    ang = (2.0 * math.pi / n) * p.astype(F32)
    return jnp.cos(ang), jnp.sin(ang)


def _dft_dot(g, x):
    return _dot(g.astype(DFT_DTYPE), x.astype(DFT_DTYPE), HIGHEST if DFT_DTYPE == F32 else None)


def _dft_tables(N1, N2):
    N = N1 * N2
    ar = lambda n: jnp.arange(n, dtype=jnp.int32)
    c, s = _cos_sin((ar(N1)[:, None] * ar(N1 // 2)[None, :]) % N1, N1)
    g1 = jnp.concatenate([c, -s], axis=0)
    k1 = ar(N1)[:, None, None]
    a = ar(N2)[None, :, None]
    b = ar(N2)[None, None, :]
    c, s = _cos_sin((b * (k1 + N1 * a)) % N, N)
    g2 = jnp.concatenate([jnp.concatenate([c, s], axis=2), jnp.concatenate([-s, c], axis=2)], axis=1)
    c, s = _cos_sin((a * (k1 + N1 * b)) % N, N)
    g2i = jnp.concatenate([jnp.concatenate([c, -s], axis=2), jnp.concatenate([s, c], axis=2)], axis=1)
    c, s = _cos_sin((ar(N1 // 2)[:, None] * ar(N1)[None, :]) % N1, N1)
    g4 = jnp.concatenate([c, -s], axis=1) * (1.0 / N)
    return tuple(t.astype(DFT_DTYPE) for t in (g1, g2, g2i, g4))


def _stage1_kernel(g_ref, x_ref, o_ref, acc_ref):
    for s in range(x_ref.shape[1]):
        acc_ref[:, s, :] = _dft_dot(g_ref[...], x_ref[0, s])
    o_ref[0] = acc_ref[...].astype(o_ref.dtype)


def dft_stage1(g, xp, ns, cl):
    B, N2, K, C = xp.shape
    M = g.shape[0]
    return pl.pallas_call(
        _stage1_kernel,
        grid=(B, N2 // ns, C // cl),
        in_specs=[pl.BlockSpec((M, K), lambda b, j, c: (0, 0)),
                  pl.BlockSpec((1, ns, K, cl), lambda b, j, c: (b, j, 0, c))],
        out_specs=pl.BlockSpec((1, M, ns, cl), lambda b, j, c: (b, 0, j, c)),
        out_shape=jax.ShapeDtypeStruct((B, M, N2, C), DFT_DTYPE),
        scratch_shapes=[pltpu.VMEM((M, ns, cl), F32)],
        compiler_params=_cparams("parallel", "parallel", "parallel"),
        name="dft_stage1",
    )(g, xp)


def _combine_spectrum(X, nrm_ref, D):
    n = X.shape[0] // 2
    nrm = nrm_ref[:, :D] + nrm_ref[:, D:]
    hre = (X[:n, :D] + X[:n, D:]) / nrm
    him = (X[n:, :D] - X[n:, D:]) / nrm
    return hre, him


def _filter_spec_kernel(g_ref, a_ref, nrm_ref, o_ref):
    _, _, n2, D2 = a_ref.shape
    slab = a_ref[:, 0].reshape(2 * n2, D2)
    X = _dft_dot(g_ref[0], slab)
    hre, him = _combine_spectrum(X, nrm_ref, D2 // 2)
    o_ref[0, 0] = hre
    o_ref[1, 0] = him


def filter_spectrum(g2, a, nrm):
    _, N1, N2, D2 = a.shape
    D = D2 // 2
    return pl.pallas_call(
        _filter_spec_kernel,
        grid=(N1,),
        in_specs=[pl.BlockSpec((1, 2 * N2, 2 * N2), lambda k: (k, 0, 0)),
                  pl.BlockSpec((2, 1, N2, D2), lambda k: (0, k, 0, 0)),
                  pl.BlockSpec((1, D2), lambda k: (0, 0))],
        out_specs=pl.BlockSpec((2, 1, N2, D), lambda k: (0, k, 0, 0)),
        out_shape=jax.ShapeDtypeStruct((2, N1, N2, D), F32),
        compiler_params=_cparams("parallel"),
        name="filter_spectrum",
    )(g2, a, nrm)


def _cmul(xr, xi, hr, hi):
    return xr * hr - xi * hi, xr * hi + xi * hr


def _spec_mul_kernel(g_ref, gi_ref, h_ref, a_ref, o_ref, acc_ref):
    n2, C = a_ref.shape[-2:]
    for j in range(a_ref.shape[2]):
        slab = a_ref[0, :, j].reshape(2 * n2, C)
        X = _dft_dot(g_ref[j], slab)
        pr, pi = _cmul(X[:n2], X[n2:], h_ref[0, j], h_ref[1, j])
        Y = _dft_dot(gi_ref[j], jnp.concatenate([pr, pi], axis=0))
        acc_ref[:, 0, j, :] = Y[:n2]
        acc_ref[:, 1, j, :] = Y[n2:]
    o_ref[0] = acc_ref[...].astype(o_ref.dtype)


def spectrum_multiply(g2, g2i, H, a, cl):
    B, _, N1, N2, C = a.shape
    kg = K1_GROUP
    gsp = pl.BlockSpec((kg, 2 * N2, 2 * N2), lambda k, c, b: (k, 0, 0))
    return pl.pallas_call(
        _spec_mul_kernel,
        grid=(N1 // kg, C // cl, B),
        in_specs=[gsp, gsp, pl.BlockSpec((2, kg, N2, cl), lambda k, c, b: (0, k, 0, c)),
                  pl.BlockSpec((1, 2, kg, N2, cl), lambda k, c, b: (b, 0, k, 0, c))],
        out_specs=pl.BlockSpec((1, N2, 2, kg, cl), lambda k, c, b: (b, 0, 0, k, c)),
        out_shape=jax.ShapeDtypeStruct((B, N2, 2, N1, C), DFT_DTYPE),
        scratch_shapes=[pltpu.VMEM((N2, 2, kg, cl), F32)],
        compiler_params=_cparams("parallel", "parallel", "parallel"),
        name="spectrum_multiply",
    )(g2, g2i, H, a)


def _idft_out_kernel(g_ref, y_ref, x0_ref, z_ref, bias_ref, o_ref, acc_ref):
    for s in range(y_ref.shape[1]):
        y = _dft_dot(g_ref[...], y_ref[0, s])
        acc_ref[:, s, :] = x0_ref[0, s] * (y + z_ref[0, s] * bias_ref[...])
    o_ref[0] = acc_ref[...].astype(o_ref.dtype)


def idft_gate_out(g4, yv, x0p, zp, bias, ns, cl):
    B, N2, K, C = yv.shape
    M = g4.shape[0]
    pdat = pl.BlockSpec((1, ns, M, cl), lambda b, j, c: (b, j, 0, c))
    return pl.pallas_call(
        _idft_out_kernel,
        grid=(B, N2 // ns, C // cl),
        in_specs=[pl.BlockSpec((M, K), lambda b, j, c: (0, 0)),
                  pl.BlockSpec((1, ns, K, cl), lambda b, j, c: (b, j, 0, c)),
                  pdat, pdat, pl.BlockSpec((1, cl), lambda b, j, c: (0, c))],
        out_specs=pl.BlockSpec((1, M, ns, cl), lambda b, j, c: (b, 0, j, c)),
        out_shape=jax.ShapeDtypeStruct((B, M, N2, C), BF16),
        scratch_shapes=[pltpu.VMEM((M, ns, cl), F32)],
        compiler_params=_cparams("parallel", "parallel", "parallel"),
        name="idft_gate_out",
    )(g4, yv, x0p, zp, bias.reshape(1, C))


def long_conv_two_stage(x0p, zp, hfbp, nrm, bias):
    B, N2, M, C = zp.shape
    L = N2 * M
    N1 = 2 * M
    g1, g2, g2i, g4 = _dft_tables(N1, N2)
    ns = HALO_ROWS
    cl = min(C, 512)
    a_f = dft_stage1(g1, hfbp, ns, cl)
    H = filter_spectrum(g2, a_f.reshape(2, N1, N2, 2 * C), nrm)
    a = dft_stage1(g1, zp, ns, cl)
    yv = spectrum_multiply(g2, g2i, H, a.reshape(B, 2, N1, N2, C), cl // 2)
    out = idft_gate_out(g4, yv.reshape(B, N2, 2 * N1, C), x0p, zp, bias, ns, cl)
    return out.reshape(B, L, C)


def _dense_tables(L):
    N = 2 * L
    ar = lambda n: jnp.arange(n, dtype=jnp.int32)
    c, s = _cos_sin((ar(N)[:, None] * ar(L)[None, :]) % N, N)
    g = jnp.concatenate([c, -s], axis=0)
    c, s = _cos_sin((ar(L)[:, None] * ar(N)[None, :]) % N, N)
    gi = jnp.concatenate([c, -s], axis=1) * (1.0 / N)
    return g.astype(DFT_DTYPE), gi.astype(DFT_DTYPE)


def _dense_spec_kernel(g_ref, hfb_ref, nrm_ref, o_ref):
    X = _dft_dot(g_ref[...], hfb_ref[...])
    hre, him = _combine_spectrum(X, nrm_ref, o_ref.shape[-1])
    o_ref[0] = hre
    o_ref[1] = him


def _dense_conv_kernel(g_ref, gi_ref, h_ref, z_ref, x0_ref, bias_ref, o_ref):
    z = z_ref[0]
    X = _dft_dot(g_ref[...], z)
    n = X.shape[0] // 2
    pr, pi = _cmul(X[:n], X[n:], h_ref[0], h_ref[1])
    y = _dft_dot(gi_ref[...], jnp.concatenate([pr, pi], axis=0))
    o_ref[0] = (x0_ref[0] * (y + z * bias_ref[...])).astype(o_ref.dtype)


def long_conv_dense(x0, z, hfb, nrm, bias):
    B, L, C = z.shape
    N = 2 * L
    g, gi = _dense_tables(L)
    full2 = lambda shape: pl.BlockSpec(shape, lambda *_: (0,) * len(shape))
    H = pl.pallas_call(
        _dense_spec_kernel,
        grid=(1,),
        in_specs=[full2((2 * N, L)), full2((L, 2 * C)), full2((1, 2 * C))],
        out_specs=full2((2, N, C)),
        out_shape=jax.ShapeDtypeStruct((2, N, C), F32),
        compiler_params=_cparams("arbitrary"),
        name="dense_filter_spectrum",
    )(g, hfb, nrm)
    dat = pl.BlockSpec((1, L, C), lambda b: (b, 0, 0))
    return pl.pallas_call(
        _dense_conv_kernel,
        grid=(B,),
        in_specs=[full2((2 * N, L)), full2((L, 2 * N)), full2((2, N, C)), dat, dat, full2((1, C))],
        out_specs=dat,
        out_shape=jax.ShapeDtypeStruct((B, L, C), BF16),
        compiler_params=_cparams("parallel"),
        name="dense_long_conv",
    )(g, gi, H, z, x0, bias.reshape(1, C))


def hyena_mix_pre(u_args, p, dense):
    x, g, shift, scale = u_args
    (w_in, b_in, sc_w, sc_b, f_w1, f_b1, f_w2, f_b2, f_w3, f_b3, f_wout, freq, bias) = p
    L = x.shape[1]
    perm_n2 = None if dense else V7X_LANES
    x0, z = hy_in_gate(x, g, shift, scale, w_in.astype(BF16), b_in, sc_w, sc_b, perm_n2)
    hfb, nrm = hyena_filter(L, f_w1, f_b1, f_w2, f_b2, f_w3, f_b3, f_wout, freq, perm_n2)
    conv = long_conv_dense if dense else long_conv_two_stage
    return conv(x0, z, hfb, nrm, bias)


def _ml_prep_kernel(xm_ref, prev_ref, next_ref, cw_ref, cb_ref, wq_ref, wk_ref, wv_ref, wg_ref, bg_ref,
                    q_ref, k_ref, kt_ref, v_ref, xc_ref, g_ref, *, k_scale):
    xm = xm_ref[0].astype(F32)
    inner = xm.shape[1]
    conv = _conv3_rows(xm, prev_ref, next_ref, cw_ref, cb_ref)
    xc = conv * jax.nn.sigmoid(conv)
    xcb = xc.astype(BF16)
    xc_ref[0] = xcb
    xmb = xm.astype(BF16)
    gw = V7X_MXU_DIM
    gates = bg_ref[...]
    for j in range(inner // gw):
        sl = slice(j * gw, (j + 1) * gw)
        q = _dot(xcb[:, sl], wq_ref[j])
        k = _dot(xcb[:, sl], wk_ref[j])
        v = _dot(xmb[:, sl], wv_ref[j])
        q_ref[0, :, sl] = q.astype(BF16)
        ks = k * k_scale
        k_ref[0, :, sl] = ks.astype(BF16)
        kt_ref[0, sl, :] = ks.T.astype(BF16)
        v_ref[0, :, sl] = v.astype(BF16)
        gates = gates + _dot(q.astype(BF16), wg_ref[j * gw:(j + 1) * gw, :])
        gates = gates + _dot(k.astype(BF16), wg_ref[inner + j * gw:inner + (j + 1) * gw, :])
        gates = gates + _dot(v.astype(BF16), wg_ref[2 * inner + j * gw:2 * inner + (j + 1) * gw, :])
    g_ref[0] = gates


def _block_diag(w, group):
    nb, bs, _ = w.shape
    per = group // bs
    w = w.reshape(nb // per, per, bs, bs)
    eye = jnp.eye(per, dtype=w.dtype)
    dense = jnp.einsum("gpce,pr->gpcre", w, eye)
    return dense.reshape(nb // per, group, group).astype(BF16)


def ml_prep(xz, conv_w, conv_b, wq, wk, wv, w_gate, b_gate, tm=256):
    B, L, C2 = xz.shape
    inner = C2 // 2
    dh = inner // ML_HEADS
    tm = _row_tile(L, tm)
    gw = V7X_MXU_DIM
    ng = inner // gw
    P = V7X_LANES
    wg = jnp.pad(w_gate, ((0, 0), (0, P - w_gate.shape[1]))).astype(BF16)
    bg = jnp.pad(b_gate.reshape(1, -1), ((0, 0), (0, P - b_gate.shape[0])))
    prev, nxt = _halo_specs(tm, L, inner, lambda: 0)
    c2 = lambda shape: pl.BlockSpec(shape, lambda b, i: (0,) * len(shape))
    row = lambda n: pl.BlockSpec((1, tm, n), lambda b, i: (b, i, 0))
    sd = lambda n, dt: jax.ShapeDtypeStruct((B, L, n), dt)
    return pl.pallas_call(
        functools.partial(_ml_prep_kernel, k_scale=dh ** -0.5),
        grid=(B, L // tm),
        in_specs=[row(inner), prev, nxt, c2((3, inner)), c2((1, inner)),
                  c2((ng, gw, gw)), c2((ng, gw, gw)), c2((ng, gw, gw)), c2((3 * inner, P)), c2((1, P))],
        out_specs=(row(inner), row(inner), pl.BlockSpec((1, inner, tm), lambda b, i: (b, 0, i)), row(inner),
                   row(inner), row(P)),
        out_shape=(sd(inner, BF16), sd(inner, BF16), jax.ShapeDtypeStruct((B, inner, L), BF16), sd(inner, BF16),
                   sd(inner, BF16), sd(P, F32)),
        compiler_params=_cparams("parallel", "arbitrary"),
        name="ml_prep",
    )(xz, xz, xz, conv_w, conv_b.reshape(1, inner), _block_diag(wq, gw), _block_diag(wk, gw),
      _block_diag(wv, gw), wg, bg)


def _gates_scan_order(g_ctx, g_lat):
    T = ML_CHUNK

    def lay(g):
        B, L, _ = g.shape
        g = g[..., :4 * ML_HEADS].reshape(B, L // T, T, 2, 2, ML_HEADS)
        return jnp.transpose(g, (3, 0, 5, 4, 1, 2))

    gc, gl = lay(g_ctx), lay(g_lat)
    fwd = jnp.concatenate([gc[0], gl[0]], axis=3)
    bwd = jnp.concatenate([gc[1][..., ::-1, :], gl[1][..., ::-1, :]], axis=3)
    out = jnp.stack([fwd, bwd])
    nct = out.shape[4]
    pad = -nct % V7X_SUBLANES
    return jnp.pad(out, ((0, 0),) * 4 + ((0, pad), (0, 0))), nct


AUX_BMR, AUX_GR, AUX_WINTER, AUX_EMR, AUX_WROW, AUX_GOLD = range(6)


def _ml_gates_kernel(g_ref, aux_ref, bl_scr, me_scr, mp_scr):
    d = pl.program_id(0)
    ig = g_ref[0, 0, 0, 0]
    fg = g_ref[0, 0, 0, 1]
    nct, T = ig.shape
    lf = jnp.minimum(fg, 0.0) - jnp.log1p(jnp.exp(-jnp.abs(fg)))
    lane = lax.broadcasted_iota(jnp.int32, (nct, T), 1)
    rev = d == 1

    def scans(x, op, ident):
        f, r = x, x
        s = 1
        while s < T:
            f = op(f, jnp.where(lane >= s, pltpu.roll(f, s, 1), ident))
            r = op(r, jnp.where(lane < T - s, pltpu.roll(r, T - s, 1), ident))
            s *= 2
        return jnp.where(rev, r, f)

    bcs = scans(lf, jnp.add, 0.0)
    b_last = jnp.sum(lf, axis=1, keepdims=True)
    gr = ig - bcs
    cmax = scans(gr, jnp.maximum, -jnp.inf)
    max_e = b_last + jnp.max(gr, axis=1, keepdims=True)
    bl_scr[...] = jnp.broadcast_to(b_last, bl_scr.shape)
    me_scr[...] = jnp.broadcast_to(max_e, me_scr.shape)

    def body(t, m):
        mp_scr[pl.ds(t, 1), :] = m
        return jnp.maximum(bl_scr[pl.ds(t, 1), :] + m, me_scr[pl.ds(t, 1), :])

    lax.fori_loop(0, nct, body, jnp.zeros((1, V7X_LANES), F32))
    m_prev = mp_scr[:, 0:1]
    a_inter = bcs + m_prev
    m_row = jnp.maximum(a_inter, bcs + cmax)
    m_new = jnp.maximum(b_last + m_prev, max_e)
    rows = {AUX_BMR: bcs - m_row, AUX_GR: gr, AUX_WINTER: jnp.exp(a_inter - m_row),
            AUX_EMR: jnp.exp(-m_row), AUX_WROW: jnp.exp(b_last + gr - m_new),
            AUX_GOLD: jnp.broadcast_to(jnp.exp(b_last + m_prev - m_new), (nct, T))}
    zero = jnp.zeros((nct, T), F32)
    for k in range(V7X_SUBLANES):
        aux_ref[0, 0, 0, :, k, :] = rows.get(k, zero)


def ml_gates(gates):
    _, B, H, _, nct, T = gates.shape
    return pl.pallas_call(
        _ml_gates_kernel,
        grid=(2, B, H),
        in_specs=[pl.BlockSpec((1, 1, 1, 2, nct, T), lambda d, b, h: (d, b, h, 0, 0, 0))],
        out_specs=pl.BlockSpec((1, 1, 1, nct, V7X_SUBLANES, T), lambda d, b, h: (d, b, h, 0, 0, 0)),
        out_shape=jax.ShapeDtypeStruct((2, B, H, nct, V7X_SUBLANES, T), F32),
        scratch_shapes=[pltpu.VMEM((nct, V7X_LANES), F32)] * 3,
        compiler_params=_cparams("arbitrary", "arbitrary", "arbitrary"),
        name="ml_gates",
    )(gates)


SCAN_HEADS = 4


def _mlstm_kernel(qf_ref, kf_ref, ktf_ref, vf_ref, qb_ref, kb_ref, ktb_ref, vb_ref, aux_ref, c0_ref, n0_ref,
                  hf_ref, hb_ref, cf_ref, nf_ref, c_scr, n_scr):
    t = pl.program_id(2)
    nct = pl.num_programs(2)
    T = ML_CHUNK
    dh = qf_ref.shape[-1] // SCAN_HEADS

    @pl.when(t == 0)
    def _():
        c_scr[...] = c0_ref[:, 0]
        n_scr[...] = n0_ref[:, 0]

    row = lax.broadcasted_iota(jnp.int32, (T, T), 0)
    col = lax.broadcasted_iota(jnp.int32, (T, T), 1)
    dirs = ((qf_ref, kf_ref, ktf_ref, vf_ref, hf_ref, col <= row), (qb_ref, kb_ref, ktb_ref, vb_ref, hb_ref, col >= row))
    streams = [(d, j) + dirs[d] for j in range(SCAN_HEADS) for d in range(2)]
    for d, j, q_ref, k_ref, kt_ref, v_ref, h_ref, mask in streams:
        hs = slice(j * dh, (j + 1) * dh)
        ax = aux_ref[d, 0, j, 0]
        axt = jnp.concatenate([ax, jnp.zeros((T - V7X_SUBLANES, T), F32)], axis=0).T
        gr = ax[AUX_GR:AUX_GR + 1, :]
        g_old = ax[AUX_GOLD:AUX_GOLD + 1, 0:1]
        bmr = axt[:, AUX_BMR:AUX_BMR + 1]
        w_inter = axt[:, AUX_WINTER:AUX_WINTER + 1]
        emr = axt[:, AUX_EMR:AUX_EMR + 1]
        w_col = axt[:, AUX_WROW:AUX_WROW + 1]
        pmat = jnp.exp(jnp.where(mask, bmr + gr, -jnp.inf))

        q = q_ref[0, :, hs]
        v = v_ref[0, :, hs]
        kT = kt_ref[0, hs, :]
        n_row = n_scr[d, j, 0:1, :]
        smat = _dot(q, kT) * pmat
        qn = jnp.sum(q.astype(F32) * n_row, axis=1, keepdims=True)
        den = w_inter * qn + jnp.sum(smat, axis=1, keepdims=True)
        num = w_inter * _dot(q, c_scr[d, j].astype(BF16)) + _dot(smat.astype(BF16), v)
        h_ref[0, :, hs] = (num * (1.0 / jnp.maximum(jnp.abs(den), emr))).astype(h_ref.dtype)
        c_scr[d, j] = g_old * c_scr[d, j] + _dot(kT, (v.astype(F32) * w_col).astype(BF16))
        dn = jnp.sum(k_ref[0, :, hs].astype(F32) * w_col, axis=0, keepdims=True)
        n_scr[d, j, 0:1, :] = g_old * n_row + dn

    @pl.when(t == nct - 1)
    def _():
        cf_ref[:, 0] = c_scr[...]
        nf_ref[:, 0] = n_scr[...]


def mlstm_scan(q, k, kt, v, aux, chunk0, c0, n0):
    B, L, inner = q.shape
    H = ML_HEADS
    dh = inner // H
    T = ML_CHUNK
    nc = L // T
    hp = SCAN_HEADS
    fw = pl.BlockSpec((1, T, hp * dh), lambda b, h, t: (b, t, h))
    bw = pl.BlockSpec((1, T, hp * dh), lambda b, h, t: (b, nc - 1 - t, h))
    fwt = pl.BlockSpec((1, hp * dh, T), lambda b, h, t: (b, h, t))
    bwt = pl.BlockSpec((1, hp * dh, T), lambda b, h, t: (b, h, nc - 1 - t))
    cst = pl.BlockSpec((2, 1, hp, dh, dh), lambda b, h, t: (0, b, h, 0, 0))
    nst = pl.BlockSpec((2, 1, hp, V7X_SUBLANES, dh), lambda b, h, t: (0, b, h, 0, 0))
    hshape = jax.ShapeDtypeStruct((B, L, inner), BF16)
    return pl.pallas_call(
        _mlstm_kernel,
        grid=(B, H // hp, nc),
        in_specs=[fw, fw, fwt, fw, bw, bw, bwt, bw,
                  pl.BlockSpec((2, 1, hp, 1, V7X_SUBLANES, T), lambda b, h, t: (0, b, h, chunk0 + t, 0, 0)),
                  cst, nst],
        out_specs=(fw, bw, cst, nst),
        out_shape=(hshape, hshape, jax.ShapeDtypeStruct(c0.shape, F32), jax.ShapeDtypeStruct(n0.shape, F32)),
        scratch_shapes=[pltpu.VMEM((2, hp, dh, dh), F32), pltpu.VMEM((2, hp, V7X_SUBLANES, dh), F32)],
        compiler_params=_cparams("parallel", "parallel", "arbitrary"),
        name="mlstm_scan",
    )(q, k, kt, v, q, k, kt, v, aux, c0, n0)


def _ml_out_down_kernel(hf_ref, hb_ref, xc_ref, z_ref, nw_ref, sk_ref, w_ref, gate_ref, res_ref, o_ref, a_scr):
    h = hf_ref[0].astype(F32) + hb_ref[0].astype(F32)
    dh = h.shape[1] // ML_HEADS
    z = z_ref[0].astype(F32)
    gate = z * jax.nn.sigmoid(z)
    for j in range(ML_HEADS):
        sl = slice(j * dh, (j + 1) * dh)
        seg = h[:, sl]
        mu = jnp.mean(seg, axis=-1, keepdims=True)
        cen = seg - mu
        var = jnp.mean(cen * cen, axis=-1, keepdims=True)
        hn = cen * lax.rsqrt(var + ML_NORM_EPS)
        hs = hn * nw_ref[:, sl] + sk_ref[:, sl] * xc_ref[0, :, sl].astype(F32)
        a_scr[:, sl] = (hs * gate[:, sl]).astype(BF16)
    o_ref[0] = res_ref[0] + gate_ref[0] * _dot(a_scr[...], w_ref[...])


def ml_out_down(hf, hb, xc, xz, norm_w, skip, w_down_b, gate, res, tm=256):
    B, L, inner = hf.shape
    D = w_down_b.shape[1]
    tm = _row_tile(L, tm)
    vec = pl.BlockSpec((1, inner), lambda b, i: (0, 0))
    dat = pl.BlockSpec((1, tm, inner), lambda b, i: (b, i, 0))
    row = pl.BlockSpec((1, tm, D), lambda b, i: (b, i, 0))
    return pl.pallas_call(
        _ml_out_down_kernel,
        grid=(B, L // tm),
        in_specs=[dat, dat, dat, pl.BlockSpec((1, tm, inner), lambda b, i: (b, i, 1)), vec, vec,
                  pl.BlockSpec((inner, D), lambda b, i: (0, 0)), pl.BlockSpec((1, 1, D), lambda b, i: (b, 0, 0)), row],
        out_specs=row,
        out_shape=jax.ShapeDtypeStruct((B, L, D), F32),
        scratch_shapes=[pltpu.VMEM((tm, inner), BF16)],
        compiler_params=_cparams("parallel", "parallel"),
        name="ml_out_down",
    )(hf, hb, xc, xz, norm_w.reshape(1, inner), skip.reshape(1, inner), w_down_b, gate, res)


def mlstm_mix_residual(lat_args, ctx_args, p, gate):
    w_in, conv_w, conv_b, wq, wk, wv, w_gate, b_gate, norm_w, skip, w_down = p
    w_in_b = w_in.astype(BF16)
    zero_b = jnp.zeros((w_in.shape[1],), F32)

    def prep(args):
        x, g, shift, scale = args
        xz = norm_mod_matmul(x, g, shift, scale, w_in_b, zero_b, tm=512)
        return ml_prep(xz, conv_w, conv_b, wq, wk, wv, w_gate, b_gate) + (xz,)

    qc, kc, ktc, vc, _, gates_c, _ = prep(ctx_args)
    q, k, kt, v, xc, gates_l, xz = prep(lat_args)
    B, _, inner = q.shape
    dh = inner // ML_HEADS
    gates, _ = _gates_scan_order(gates_c, gates_l)
    aux = ml_gates(gates)
    c0 = jnp.zeros((2, B, ML_HEADS, dh, dh), F32)
    n0 = jnp.zeros((2, B, ML_HEADS, V7X_SUBLANES, dh), F32)
    _, _, c1, n1 = mlstm_scan(qc, kc, ktc, vc, aux, 0, c0, n0)
    hf, hb, _, _ = mlstm_scan(q, k, kt, v, aux, qc.shape[1] // ML_CHUNK, c1, n1)
    return ml_out_down(hf, hb, xc, xz, norm_w, skip, w_down.astype(BF16), gate, lat_args[0])


def _ffn_kernel(x_ref, xp_ref, xn_ref, ng_ref, sh_ref, sc_ref, wu_ref, cw_ref, cb_ref, wd_ref,
                gate_ref, fg_ref, o_ref, u_scr, *, cols, vertical, final_norm, cb):
    i = pl.program_id(1)
    last = pl.num_programs(1) - 1
    tm = x_ref.shape[1]
    F = wd_ref.shape[0]
    halo = cols if vertical else 0

    def norm_mod(x):
        y = x * lax.rsqrt(jnp.mean(x * x, axis=-1, keepdims=True) + EPS)
        return ((y * ng_ref[...]) * (1.0 + sc_ref[0]) + sh_ref[0]).astype(BF16)

    u_scr[halo:halo + tm] = norm_mod(x_ref[0])
    if vertical:
        u_scr[0:halo] = norm_mod(xp_ref[0])
        u_scr[halo + tm:] = norm_mod(xn_ref[0])
    R = tm + 2 * halo
    rowi = lax.broadcasted_iota(jnp.int32, (R, 1), 0)
    cpos = jnp.bitwise_and(rowi, cols - 1)
    if vertical:
        top_ok = jnp.where(i > 0, 1.0, 0.0)
        bot_ok = jnp.where(i < last, 1.0, 0.0)
        rowmask = jnp.where(rowi < halo, top_ok, jnp.where(rowi >= halo + tm, bot_ok, 1.0))
    acc = jnp.zeros((tm, o_ref.shape[-1]), F32)
    for f in range(F // cb):
        fs = slice(f * cb, (f + 1) * cb)
        g = _dot(u_scr[...], wu_ref[:, F + f * cb:F + (f + 1) * cb])
        a = _dot(u_scr[halo:halo + tm], wu_ref[:, fs])
        if vertical:
            g = g * rowmask
        left = jnp.where(cpos == 0, 0.0, pltpu.roll(g, 1, 0))
        right = jnp.where(cpos == cols - 1, 0.0, pltpu.roll(g, R - 1, 0))
        conv = cb_ref[:, fs]
        for dr in (range(3) if vertical else (1,)):
            sl = slice(dr * halo, dr * halo + tm)
            conv = conv + (cw_ref[3 * dr:3 * dr + 1, fs] * left[sl] + cw_ref[3 * dr + 1:3 * dr + 2, fs] * g[sl]
                           + cw_ref[3 * dr + 2:3 * dr + 3, fs] * right[sl])
        act = ((conv * jax.nn.sigmoid(conv)) * a).astype(BF16)
        acc = acc + _dot(act, wd_ref[fs, :])
    x = x_ref[0] + gate_ref[0] * acc
    if final_norm:
        x = (x * lax.rsqrt(jnp.mean(x * x, axis=-1, keepdims=True) + EPS)) * fg_ref[...]
    o_ref[0] = x


def conv_ffn_residual(x, g, shift, scale, gate, w_up_b, conv_w, conv_b, w_down_b, rows, cols, final_g=None,
                      tm=512):
    B, L, D = x.shape
    F = w_down_b.shape[0]
    assert cols & (cols - 1) == 0 and rows * cols == L
    vertical = rows > 1
    tm = _row_tile(L, tm) if vertical else L
    cb = F
    assert tm % cols == 0
    hb = cols if vertical else HALO_ROWS
    nhb = L // hb
    final_norm = final_g is not None
    fg = final_g if final_norm else jnp.ones((D,), F32)
    vecb = pl.BlockSpec((1, 1, D), lambda b, i: (b, 0, 0))
    full = lambda shape: pl.BlockSpec(shape, lambda b, i: (0, 0), pipeline_mode=pl.Buffered(1))
    row = pl.BlockSpec((1, tm, D), lambda b, i: (b, i, 0))
    return pl.pallas_call(
        functools.partial(_ffn_kernel, cols=cols, vertical=vertical, final_norm=final_norm, cb=cb),
        grid=(B, L // tm),
        in_specs=[row,
                  pl.BlockSpec((1, hb, D), lambda b, i: (b, jnp.maximum(i * (tm // hb) - 1, 0), 0)),
                  pl.BlockSpec((1, hb, D), lambda b, i: (b, jnp.minimum((i + 1) * (tm // hb), nhb - 1), 0)),
                  full((1, D)), vecb, vecb, full((D, 2 * F)), full((9, F)), full((1, F)), full((F, D)),
                  vecb, full((1, D))],
        out_specs=row,
        out_shape=jax.ShapeDtypeStruct((B, L, D), F32),
        scratch_shapes=[pltpu.VMEM((tm + (2 * cols if vertical else 0), D), BF16)],
        compiler_params=_cparams("parallel", "arbitrary"),
        name="conv_ffn",
    )(x, x, x, g.reshape(1, D), shift, scale, w_up_b, conv_w.reshape(9, F), conv_b.reshape(1, F),
      w_down_b, gate, fg.reshape(1, D))


def kernel(x, c, ctx, c_ctx, mod_w, mod_b, norm_g, final_g, hy_w_in, hy_b_in, hy_sc_w, hy_sc_b, hy_f_w1, hy_f_b1, hy_f_w2, hy_f_b2, hy_f_w3, hy_f_b3, hy_f_wout, hy_freq, hy_bias, hy_w_out, hy_b_out, ml_w_in, ml_conv_w, ml_conv_b, ml_wq, ml_wk, ml_wv, ml_w_gate, ml_b_gate, ml_norm_w, ml_skip, ml_w_down, ffn_w_up, ffn_conv_w, ffn_conv_b, ffn_w_down):
    B, L, D = x.shape
    ctx_len = ctx.shape[1]
    depth = mod_w.shape[0]
    n_mixers = 2
    rows = L // GRID_W
    hy_params = (hy_w_in, hy_b_in, hy_sc_w, hy_sc_b, hy_f_w1, hy_f_b1, hy_f_w2, hy_f_b2,
                 hy_f_w3, hy_f_b3, hy_f_wout, hy_freq, hy_bias)
    ml_params = (ml_w_in, ml_conv_w, ml_conv_b, ml_wq, ml_wk, ml_wv, ml_w_gate, ml_b_gate,
                 ml_norm_w, ml_skip, ml_w_down)
    cond = jnp.concatenate([c, c_ctx.reshape(1, D), jnp.zeros((V7X_SUBLANES - B - 1, D), F32)], axis=0)
    for i in range(depth):
        last = i == depth - 1
        mod = adaln(cond, mod_w[i], mod_b[i])
        lat = [mod[:B, k * D:(k + 1) * D].reshape(B, 1, D) for k in range(6)]
        cm = [jnp.broadcast_to(mod[B:B + 1, k * D:(k + 1) * D].reshape(1, 1, D), (B, 1, D)) for k in range(6)]
        lat_args = (x, norm_g[i, 0], lat[0], lat[1])
        ctx_args = (ctx, norm_g[i, 0], cm[0], cm[1])
        j = i // n_mixers
        if i % n_mixers == 0:
            p = tuple(a[j] for a in hy_params)
            w_out_b = hy_w_out[j].astype(BF16)
            x = res_gate_matmul(hyena_mix_pre(lat_args, p, dense=False), w_out_b, hy_b_out[j], lat[2], x)
            if not last:
                ctx = res_gate_matmul(hyena_mix_pre(ctx_args, p, dense=True), w_out_b, hy_b_out[j], cm[2], ctx)
        else:
            assert last, "the mLSTM mixer is only implemented for the last layer (no context output)"
            p = tuple(a[j] for a in ml_params)
            x = mlstm_mix_residual(lat_args, ctx_args, p, lat[2])
        w_up_b = ffn_w_up[i].astype(BF16)
        w_down_b = ffn_w_down[i].astype(BF16)
        x = conv_ffn_residual(x, norm_g[i, 1], lat[3], lat[4], lat[5], w_up_b, ffn_conv_w[i], ffn_conv_b[i],
                              w_down_b, rows, GRID_W, final_g if last else None)
        if not last:
            ctx = conv_ffn_residual(ctx, norm_g[i, 1], cm[3], cm[4], cm[5], w_up_b, ffn_conv_w[i],
                                    ffn_conv_b[i], w_down_b, 1, ctx_len)
    return x
```

```python
import functools
import math

import jax
import jax.numpy as jnp
from jax import lax
from jax.experimental import pallas as pl
from jax.experimental.pallas import tpu as pltpu

F32 = jnp.float32
BF16 = jnp.bfloat16
HIGHEST = lax.Precision.HIGHEST

EPS = 1e-6
ML_NORM_EPS = 1e-5
GRID_W = 64
ML_HEADS = 4
ML_CHUNK = 128
ML_QKV_BLOCK = 4
HY_EMB_BANDS = 16
HY_MAX_DECAY = math.log(1e-2) / 0.3
HY_MIN_DECAY = math.log(1e-2) / 1.5

V7X_LANES = 128
V7X_SUBLANES = 8
V7X_MXU_DIM = 256
V7X_VMEM_BYTES = 64 * 1024 * 1024
VMEM_LIMIT = V7X_VMEM_BYTES - 8 * 1024 * 1024
HALO_ROWS = 16


def _cparams(*sem):
    return pltpu.CompilerParams(dimension_semantics=("arbitrary",) * len(sem), vmem_limit_bytes=VMEM_LIMIT)


def _dot(a, b, precision=None):
    return jnp.dot(a, b, preferred_element_type=F32, precision=precision)


def _row_tile(n, target):
    t = min(n, target)
    assert n % t == 0, (n, t)
    return t


def _adaln_kernel(c_ref, w_ref, b_ref, o_ref):
    c = c_ref[...]
    s = c * jax.nn.sigmoid(c)
    o_ref[...] = _dot(s, w_ref[...], HIGHEST) + b_ref[...]


def adaln(cond8, w, b):
    r, d = cond8.shape
    n = w.shape[1]
    tn = _row_tile(n, 1536)
    return pl.pallas_call(
        _adaln_kernel,
        grid=(n // tn,),
        in_specs=[pl.BlockSpec((r, d), lambda j: (0, 0)),
                  pl.BlockSpec((d, tn), lambda j: (0, j)),
                  pl.BlockSpec((1, tn), lambda j: (0, j))],
        out_specs=pl.BlockSpec((r, tn), lambda j: (0, j)),
        out_shape=jax.ShapeDtypeStruct((r, n), F32),
        compiler_params=_cparams("arbitrary"),
        name="adaln",
    )(cond8, w, b.reshape(1, n))


def _nmm_kernel(x_ref, g_ref, sh_ref, sc_ref, w_ref, b_ref, o_ref):
    x = x_ref[0]
    y = x * lax.rsqrt(jnp.mean(x * x, axis=-1, keepdims=True) + EPS)
    u = (y * g_ref[...]) * (1.0 + sc_ref[0]) + sh_ref[0]
    acc = _dot(u.astype(BF16), w_ref[...])
    o_ref[0] = (acc + b_ref[...]).astype(o_ref.dtype)


def norm_mod_matmul(x, g, shift, scale, w_bf16, bias, tm, out_dtype=BF16):
    B, L, D = x.shape
    n = w_bf16.shape[1]
    tm = _row_tile(L, tm)
    return pl.pallas_call(
        _nmm_kernel,
        grid=(B, L // tm),
        in_specs=[pl.BlockSpec((1, tm, D), lambda b, i: (b, i, 0)),
                  pl.BlockSpec((1, D), lambda b, i: (0, 0)),
                  pl.BlockSpec((1, 1, D), lambda b, i: (b, 0, 0)),
                  pl.BlockSpec((1, 1, D), lambda b, i: (b, 0, 0)),
                  pl.BlockSpec((D, n), lambda b, i: (0, 0)),
                  pl.BlockSpec((1, n), lambda b, i: (0, 0))],
        out_specs=pl.BlockSpec((1, tm, n), lambda b, i: (b, i, 0)),
        out_shape=jax.ShapeDtypeStruct((B, L, n), out_dtype),
        compiler_params=_cparams("parallel", "parallel"),
        name="norm_mod_matmul",
    )(x, g.reshape(1, D), shift, scale, w_bf16, bias.reshape(1, n))


def _rgm_kernel(a_ref, w_ref, b_ref, gate_ref, res_ref, fg_ref, o_ref, *, final_norm):
    acc = _dot(a_ref[0], w_ref[...]) + b_ref[...]
    x = res_ref[0] + gate_ref[0] * acc
    if final_norm:
        x = (x * lax.rsqrt(jnp.mean(x * x, axis=-1, keepdims=True) + EPS)) * fg_ref[...]
    o_ref[0] = x


def res_gate_matmul(a_bf16, w_bf16, bias, gate, res, final_g=None, tm=512):
    B, L, K = a_bf16.shape
    D = w_bf16.shape[1]
    tm = _row_tile(L, tm)
    final_norm = final_g is not None
    fg = final_g if final_norm else jnp.ones((D,), F32)
    return pl.pallas_call(
        functools.partial(_rgm_kernel, final_norm=final_norm),
        grid=(B, L // tm),
        in_specs=[pl.BlockSpec((1, tm, K), lambda b, i: (b, i, 0)),
                  pl.BlockSpec((K, D), lambda b, i: (0, 0)),
                  pl.BlockSpec((1, D), lambda b, i: (0, 0)),
                  pl.BlockSpec((1, 1, D), lambda b, i: (b, 0, 0)),
                  pl.BlockSpec((1, tm, D), lambda b, i: (b, i, 0)),
                  pl.BlockSpec((1, D), lambda b, i: (0, 0))],
        out_specs=pl.BlockSpec((1, tm, D), lambda b, i: (b, i, 0)),
        out_shape=jax.ShapeDtypeStruct((B, L, D), F32),
        compiler_params=_cparams("parallel", "parallel"),
        name="res_gate_matmul",
    )(a_bf16, w_bf16, bias.reshape(1, D), gate, res, fg.reshape(1, D))


def _halo_specs(tm, L, C, cmap):
    r = HALO_ROWS
    nb = L // r
    prev = pl.BlockSpec((1, r, C), lambda b, i, *a: (b, jnp.maximum(i * (tm // r) - 1, 0), cmap(*a)))
    nxt = pl.BlockSpec((1, r, C), lambda b, i, *a: (b, jnp.minimum((i + 1) * (tm // r), nb - 1), cmap(*a)))
    return prev, nxt


def _conv3_rows(x, prev_ref, next_ref, w_ref, b_ref):
    tm = x.shape[0]
    i = pl.program_id(1)
    last = pl.num_programs(1) - 1
    prev_row = jnp.where(i > 0, prev_ref[0].astype(F32)[HALO_ROWS - 1:HALO_ROWS, :], 0.0)
    next_row = jnp.where(i < last, next_ref[0].astype(F32)[0:1, :], 0.0)
    row = lax.broadcasted_iota(jnp.int32, x.shape, 0)
    xm1 = jnp.where(row == 0, prev_row, pltpu.roll(x, 1, 0))
    xp1 = jnp.where(row == tm - 1, next_row, pltpu.roll(x, tm - 1, 0))
    return w_ref[0:1, :] * xm1 + w_ref[1:2, :] * x + w_ref[2:3, :] * xp1 + b_ref[...]


def _x_halo_specs(tm, L, D):
    r = HALO_ROWS
    nb = L // r
    return (pl.BlockSpec((1, tm, D), lambda b, i: (b, i, 0)),
            pl.BlockSpec((1, r, D), lambda b, i: (b, jnp.maximum(i * (tm // r) - 1, 0), 0)),
            pl.BlockSpec((1, r, D), lambda b, i: (b, jnp.minimum((i + 1) * (tm // r), nb - 1), 0)))


def _norm_mod_ext(x_ref, xp_ref, xn_ref, ng_ref, sh_ref, sc_ref, u_scr):
    tm = x_ref.shape[1]
    h = HALO_ROWS

    def norm_mod(x):
        y = x * lax.rsqrt(jnp.mean(x * x, axis=-1, keepdims=True) + EPS)
        return ((y * ng_ref[...]) * (1.0 + sc_ref[0]) + sh_ref[0]).astype(BF16)

    u_scr[0:h] = norm_mod(xp_ref[0])
    u_scr[h:h + tm] = norm_mod(x_ref[0])
    u_scr[h + tm:] = norm_mod(xn_ref[0])
    i = pl.program_id(1)
    rowi = lax.broadcasted_iota(jnp.int32, (tm + 2 * h, 1), 0)
    top_ok = jnp.where(i > 0, 1.0, 0.0)
    bot_ok = jnp.where(i < pl.num_programs(1) - 1, 1.0, 0.0)
    return jnp.where(rowi < h, top_ok, jnp.where(rowi >= h + tm, bot_ok, 1.0))


def _conv3_ext(p, w_ref, b_ref, cs, tm):
    h = HALO_ROWS
    R = p.shape[0]
    return (w_ref[0:1, cs] * pltpu.roll(p, 1, 0)[h:h + tm] + w_ref[1:2, cs] * p[h:h + tm]
            + w_ref[2:3, cs] * pltpu.roll(p, R - 1, 0)[h:h + tm] + b_ref[:, cs])


def _hy_in_gate_kernel(x_ref, xp_ref, xn_ref, ng_ref, sh_ref, sc_ref, w_ref, b_ref, cw_ref, cb_ref,
                       x0_ref, z_ref, u_scr, *, cb):
    tm, D = x_ref.shape[1:]
    rowmask = _norm_mod_ext(x_ref, xp_ref, xn_ref, ng_ref, sh_ref, sc_ref, u_scr)
    for c in range(D // cb):
        conv = []
        for k in range(3):
            cs = slice(k * D + c * cb, k * D + (c + 1) * cb)
            p = (_dot(u_scr[...], w_ref[:, cs]) + b_ref[:, cs]) * rowmask
            conv.append(_conv3_ext(p, cw_ref, cb_ref, cs, tm))
        cs = slice(c * cb, (c + 1) * cb)
        x0_ref[0, :, cs] = conv[0]
        z_ref[0, :, cs] = conv[1] * conv[2]


def _hy_in_gate_perm_kernel(x_ref, xp_ref, xn_ref, ng_ref, sh_ref, sc_ref, w_ref, b_ref, cw_ref, cb_ref,
                            x0_ref, z_ref, u_scr):
    tm, D = x_ref.shape[1:]
    n2, J = x0_ref.shape[1:3]
    i = pl.program_id(1)

    def norm_mod(x):
        y = x * lax.rsqrt(jnp.mean(x * x, axis=-1, keepdims=True) + EPS)
        return (y * ng_ref[...]) * (1.0 + sc_ref[0]) + sh_ref[0]

    u = pltpu.einshape("jnc->njc", norm_mod(x_ref[0]).reshape(J, n2, D)).reshape(tm, D)
    u_scr[0:tm] = u.astype(BF16)
    u_scr[tm:tm + J] = norm_mod(xp_ref[0][HALO_ROWS - J:]).astype(BF16)
    u_scr[tm + J:] = norm_mod(xn_ref[0][:J]).astype(BF16)
    top_ok = jnp.where(i > 0, 1.0, 0.0)
    bot_ok = jnp.where(i < pl.num_programs(1) - 1, 1.0, 0.0)
    jrow = lax.broadcasted_iota(jnp.int32, (J, 1), 0)
    conv = []
    for k in range(3):
        cs = slice(k * D, (k + 1) * D)
        p = _dot(u_scr[...], w_ref[:, cs]) + b_ref[:, cs]
        pm = p[0:tm]
        before = p[tm + J - 1:tm + J] * top_ok
        after = p[tm + J:tm + J + 1] * bot_ok
        first_prev = jnp.where(jrow == 0, before, pltpu.roll(pm[tm - J:tm], 1, 0))
        last_next = jnp.where(jrow == J - 1, after, pltpu.roll(pm[0:J], J - 1, 0))
        xm1 = jnp.concatenate([first_prev, pm[0:tm - J]], axis=0)
        xp1 = jnp.concatenate([pm[J:tm], last_next], axis=0)
        conv.append(cw_ref[0:1, cs] * xm1 + cw_ref[1:2, cs] * pm + cw_ref[2:3, cs] * xp1 + cb_ref[:, cs])
    x0_ref[0] = conv[0].reshape(n2, J, D)
    z_ref[0] = (conv[1] * conv[2]).reshape(n2, J, D)


def hy_in_gate(x, g, shift, scale, w_bf16, b_in, sc_w, sc_b, perm_n2=None):
    B, L, D = x.shape
    C3 = w_bf16.shape[1]
    perm = perm_n2 is not None
    tm = V7X_SUBLANES * perm_n2 if perm else L
    assert L % tm == 0
    full = lambda shape: pl.BlockSpec(shape, lambda b, i: (0, 0))
    vecb = pl.BlockSpec((1, 1, D), lambda b, i: (b, 0, 0))
    if perm:
        out = jax.ShapeDtypeStruct((B, perm_n2, L // perm_n2, D), F32)
        ospec = pl.BlockSpec((1, perm_n2, V7X_SUBLANES, D), lambda b, i: (b, 0, i, 0))
    else:
        out = jax.ShapeDtypeStruct((B, L, D), F32)
        ospec = pl.BlockSpec((1, tm, D), lambda b, i: (b, i, 0))
    kern = _hy_in_gate_perm_kernel if perm else functools.partial(_hy_in_gate_kernel, cb=D)
    return pl.pallas_call(
        kern,
        grid=(B, L // tm),
        in_specs=[*_x_halo_specs(tm, L, D), full((1, D)), vecb, vecb, full((D, C3)), full((1, C3)),
                  full((3, C3)), full((1, C3))],
        out_specs=(ospec, ospec),
        out_shape=(out, out),
        scratch_shapes=[pltpu.VMEM((tm + 2 * (V7X_SUBLANES if perm else HALO_ROWS), D), BF16)],
        compiler_params=_cparams("parallel", "arbitrary"),
        name="hy_in_gate",
    )(x, x, x, g.reshape(1, D), shift, scale, w_bf16, b_in.reshape(1, C3), sc_w, sc_b.reshape(1, C3))


def _filter_kernel(w1_ref, b1_ref, w2_ref, b2_ref, w3_ref, b3_ref, wo_ref, fr_ref, band_ref,
                   h_ref, s_ref, *, L, D, tl, perm):
    i = pl.program_id(0)
    rowi = lax.broadcasted_iota(jnp.int32, (tl, 1), 0)
    if perm:
        n2, J = h_ref.shape[1:3]
        rowi = jnp.bitwise_and(rowi, J - 1) * n2 + jnp.right_shift(rowi, J.bit_length() - 1)
    rowi = rowi + i * tl
    row = rowi.astype(F32)
    t = row / (L - 1.0)
    w = (2.0 * math.pi) * row / L
    col = lax.broadcasted_iota(jnp.int32, (tl, V7X_LANES), 1)
    ang = w * band_ref[...]
    pos = jnp.where(col == 0, t,
                    jnp.where(col <= HY_EMB_BANDS, jnp.cos(ang),
                              jnp.where(col <= 2 * HY_EMB_BANDS, -jnp.sin(ang), 0.0)))
    fr = fr_ref[...]
    h = jnp.sin(fr * (_dot(pos, w1_ref[...], HIGHEST) + b1_ref[...]))
    h = jnp.sin(fr * (_dot(h, w2_ref[...], HIGHEST) + b2_ref[...]))
    h = jnp.sin(fr * (_dot(h, w3_ref[...], HIGHEST) + b3_ref[...]))
    h = _dot(h, wo_ref[...], HIGHEST)
    dcol = lax.broadcasted_iota(jnp.int32, (1, 2 * D), 1)
    chan = jnp.where(dcol >= D, dcol - D, dcol).astype(F32)
    deltas = HY_MIN_DECAY + chan * ((HY_MAX_DECAY - HY_MIN_DECAY) / (D - 1.0))
    h = h * jnp.exp(-t * jnp.abs(deltas))
    h = jnp.where((rowi == 0) & (dcol >= D), 0.0, h)
    h_ref[...] = h.reshape(h_ref.shape)
    part = jnp.sum(jnp.abs(h), axis=0, keepdims=True)

    @pl.when(i == 0)
    def _():
        s_ref[...] = part

    @pl.when(i > 0)
    def _():
        s_ref[...] += part


def hyena_filter(L, f_w1, f_b1, f_w2, f_b2, f_w3, f_b3, f_wout, freq, perm_n2=None):
    W = f_w2.shape[0]
    D2 = f_wout.shape[1]
    P = V7X_LANES
    pad2 = lambda a: jnp.pad(a, ((0, P - a.shape[0]), (0, P - a.shape[1])))
    padv = lambda a: jnp.pad(a.reshape(1, -1), ((0, 0), (0, P - a.shape[0])))
    w1 = pad2(f_w1)
    w2 = pad2(f_w2)
    w3 = pad2(f_w3)
    wo = jnp.pad(f_wout, ((0, P - W), (0, 0)))
    bands = jnp.linspace(1e-4, HY_EMB_BANDS - 1, HY_EMB_BANDS, dtype=F32)
    band_row = jnp.concatenate([jnp.zeros((1,), F32), bands, bands,
                                jnp.zeros((P - 1 - 2 * HY_EMB_BANDS,), F32)]).reshape(1, P)
    perm = perm_n2 is not None
    tl = V7X_SUBLANES * perm_n2 if perm else _row_tile(L, 512)
    assert L % tl == 0
    full = lambda shape: pl.BlockSpec(shape, lambda i: (0, 0))
    if perm:
        hshape = jax.ShapeDtypeStruct((1, perm_n2, L // perm_n2, D2), F32)
        hspec = pl.BlockSpec((1, perm_n2, V7X_SUBLANES, D2), lambda i: (0, 0, i, 0))
    else:
        hshape = jax.ShapeDtypeStruct((L, D2), F32)
        hspec = pl.BlockSpec((tl, D2), lambda i: (i, 0))
    return pl.pallas_call(
        functools.partial(_filter_kernel, L=L, D=D2 // 2, tl=tl, perm=perm),
        grid=(L // tl,),
        in_specs=[full((P, P)), full((1, P)), full((P, P)), full((1, P)), full((P, P)), full((1, P)),
                  full((P, D2)), full((1, P)), full((1, P))],
        out_specs=(hspec, pl.BlockSpec((1, D2), lambda i: (0, 0))),
        out_shape=(hshape, jax.ShapeDtypeStruct((1, D2), F32)),
        compiler_params=_cparams("arbitrary"),
        name="hyena_filter",
    )(w1, padv(f_b1), w2, padv(f_b2), w3, padv(f_b3), wo, padv(freq), band_row)


DFT_DTYPE = BF16
K1_GROUP = HALO_ROWS


def _cos_sin(p, n):
    ang = (2.0 * math.pi / n) * p.astype(F32)
    return jnp.cos(ang), jnp.sin(ang)


def _dft_dot(g, x):
    return _dot(g.astype(DFT_DTYPE), x.astype(DFT_DTYPE), HIGHEST if DFT_DTYPE == F32 else None)


def _dft_tables(N1, N2):
    N = N1 * N2
    ar = lambda n: jnp.arange(n, dtype=jnp.int32)
    c, s = _cos_sin((ar(N1)[:, None] * ar(N1 // 2)[None, :]) % N1, N1)
    g1 = jnp.concatenate([c, -s], axis=0)
    k1 = ar(N1)[:, None, None]
    a = ar(N2)[None, :, None]
    b = ar(N2)[None, None, :]
    c, s = _cos_sin((b * (k1 + N1 * a)) % N, N)
    g2 = jnp.concatenate([jnp.concatenate([c, s], axis=2), jnp.concatenate([-s, c], axis=2)], axis=1)
    c, s = _cos_sin((a * (k1 + N1 * b)) % N, N)
    g2i = jnp.concatenate([jnp.concatenate([c, -s], axis=2), jnp.concatenate([s, c], axis=2)], axis=1)
    c, s = _cos_sin((ar(N1 // 2)[:, None] * ar(N1)[None, :]) % N1, N1)
    g4 = jnp.concatenate([c, -s], axis=1) * (1.0 / N)
    return tuple(t.astype(DFT_DTYPE) for t in (g1, g2, g2i, g4))


def _stage1_kernel(g_ref, x_ref, o_ref, acc_ref):
    for s in range(x_ref.shape[1]):
        acc_ref[:, s, :] = _dft_dot(g_ref[...], x_ref[0, s])
    o_ref[0] = acc_ref[...].astype(o_ref.dtype)


def dft_stage1(g, xp, ns, cl):
    B, N2, K, C = xp.shape
    M = g.shape[0]
    return pl.pallas_call(
        _stage1_kernel,
        grid=(B, N2 // ns, C // cl),
        in_specs=[pl.BlockSpec((M, K), lambda b, j, c: (0, 0)),
                  pl.BlockSpec((1, ns, K, cl), lambda b, j, c: (b, j, 0, c))],
        out_specs=pl.BlockSpec((1, M, ns, cl), lambda b, j, c: (b, 0, j, c)),
        out_shape=jax.ShapeDtypeStruct((B, M, N2, C), DFT_DTYPE),
        scratch_shapes=[pltpu.VMEM((M, ns, cl), F32)],
        compiler_params=_cparams("parallel", "parallel", "parallel"),
        name="dft_stage1",
    )(g, xp)


def _combine_spectrum(X, nrm_ref, D):
    n = X.shape[0] // 2
    nrm = nrm_ref[:, :D] + nrm_ref[:, D:]
    hre = (X[:n, :D] + X[:n, D:]) / nrm
    him = (X[n:, :D] - X[n:, D:]) / nrm
    return hre, him


def _filter_spec_kernel(g_ref, a_ref, nrm_ref, o_ref):
    _, _, n2, D2 = a_ref.shape
    slab = a_ref[:, 0].reshape(2 * n2, D2)
    X = _dft_dot(g_ref[0], slab)
    hre, him = _combine_spectrum(X, nrm_ref, D2 // 2)
    o_ref[0, 0] = hre
    o_ref[1, 0] = him


def filter_spectrum(g2, a, nrm):
    _, N1, N2, D2 = a.shape
    D = D2 // 2
    return pl.pallas_call(
        _filter_spec_kernel,
        grid=(N1,),
        in_specs=[pl.BlockSpec((1, 2 * N2, 2 * N2), lambda k: (k, 0, 0)),
                  pl.BlockSpec((2, 1, N2, D2), lambda k: (0, k, 0, 0)),
                  pl.BlockSpec((1, D2), lambda k: (0, 0))],
        out_specs=pl.BlockSpec((2, 1, N2, D), lambda k: (0, k, 0, 0)),
        out_shape=jax.ShapeDtypeStruct((2, N1, N2, D), F32),
        compiler_params=_cparams("parallel"),
        name="filter_spectrum",
    )(g2, a, nrm)


def _cmul(xr, xi, hr, hi):
    return xr * hr - xi * hi, xr * hi + xi * hr


def _spec_mul_kernel(g_ref, gi_ref, h_ref, a_ref, o_ref):
    n2, C = a_ref.shape[-2:]
    ys = []
    for j in range(a_ref.shape[2]):
        slab = a_ref[0, :, j].reshape(2 * n2, C)
        X = _dft_dot(g_ref[j], slab)
        pr, pi = _cmul(X[:n2], X[n2:], h_ref[0, j], h_ref[1, j])
        ys.append(_dft_dot(gi_ref[j], jnp.concatenate([pr, pi], axis=0)))
    t = pltpu.einshape("jrc->rjc", jnp.stack(ys)).astype(o_ref.dtype)
    o_ref[0, :, 0] = t[:n2]
    o_ref[0, :, 1] = t[n2:]


def spectrum_multiply(g2, g2i, H, a, cl):
    B, _, N1, N2, C = a.shape
    kg = K1_GROUP
    gsp = pl.BlockSpec((kg, 2 * N2, 2 * N2), lambda k, c, b: (k, 0, 0))
    return pl.pallas_call(
        _spec_mul_kernel,
        grid=(N1 // kg, C // cl, B),
        in_specs=[gsp, gsp, pl.BlockSpec((2, kg, N2, cl), lambda k, c, b: (0, k, 0, c)),
                  pl.BlockSpec((1, 2, kg, N2, cl), lambda k, c, b: (b, 0, k, 0, c))],
        out_specs=pl.BlockSpec((1, N2, 2, kg, cl), lambda k, c, b: (b, 0, 0, k, c)),
        out_shape=jax.ShapeDtypeStruct((B, N2, 2, N1, C), DFT_DTYPE),
        compiler_params=_cparams("parallel", "parallel", "parallel"),
        name="spectrum_multiply",
    )(g2, g2i, H, a)


def _idft_out_kernel(g_ref, y_ref, x0_ref, z_ref, bias_ref, o_ref, acc_ref):
    for s in range(y_ref.shape[1]):
        y = _dft_dot(g_ref[...], y_ref[0, s])
        acc_ref[:, s, :] = x0_ref[0, s] * (y + z_ref[0, s] * bias_ref[...])
    o_ref[0] = acc_ref[...].astype(o_ref.dtype)


def idft_gate_out(g4, yv, x0p, zp, bias, ns, cl):
    B, N2, K, C = yv.shape
    M = g4.shape[0]
    pdat = pl.BlockSpec((1, ns, M, cl), lambda b, j, c: (b, j, 0, c))
    return pl.pallas_call(
        _idft_out_kernel,
        grid=(B, N2 // ns, C // cl),
        in_specs=[pl.BlockSpec((M, K), lambda b, j, c: (0, 0)),
                  pl.BlockSpec((1, ns, K, cl), lambda b, j, c: (b, j, 0, c)),
                  pdat, pdat, pl.BlockSpec((1, cl), lambda b, j, c: (0, c))],
        out_specs=pl.BlockSpec((1, M, ns, cl), lambda b, j, c: (b, 0, j, c)),
        out_shape=jax.ShapeDtypeStruct((B, M, N2, C), BF16),
        scratch_shapes=[pltpu.VMEM((M, ns, cl), F32)],
        compiler_params=_cparams("parallel", "parallel", "parallel"),
        name="idft_gate_out",
    )(g4, yv, x0p, zp, bias.reshape(1, C))


def long_conv_two_stage(x0p, zp, hfbp, nrm, bias):
    B, N2, M, C = zp.shape
    L = N2 * M
    N1 = 2 * M
    g1, g2, g2i, g4 = _dft_tables(N1, N2)
    ns = HALO_ROWS
    cl = min(C, 512)
    a_f = dft_stage1(g1, hfbp, ns, cl)
    H = filter_spectrum(g2, a_f.reshape(2, N1, N2, 2 * C), nrm)
    a = dft_stage1(g1, zp, ns, cl)
    yv = spectrum_multiply(g2, g2i, H, a.reshape(B, 2, N1, N2, C), cl // 2)
    out = idft_gate_out(g4, yv.reshape(B, N2, 2 * N1, C), x0p, zp, bias, ns, cl)
    return out.reshape(B, L, C)


def _dense_tables(L):
    N = 2 * L
    ar = lambda n: jnp.arange(n, dtype=jnp.int32)
    c, s = _cos_sin((ar(N)[:, None] * ar(L)[None, :]) % N, N)
    g = jnp.concatenate([c, -s], axis=0)
    c, s = _cos_sin((ar(L)[:, None] * ar(N)[None, :]) % N, N)
    gi = jnp.concatenate([c, -s], axis=1) * (1.0 / N)
    return g.astype(DFT_DTYPE), gi.astype(DFT_DTYPE)


def _dense_spec_kernel(g_ref, hfb_ref, nrm_ref, o_ref):
    X = _dft_dot(g_ref[...], hfb_ref[...])
    hre, him = _combine_spectrum(X, nrm_ref, o_ref.shape[-1])
    o_ref[0] = hre
    o_ref[1] = him


def _dense_conv_kernel(g_ref, gi_ref, h_ref, z_ref, x0_ref, bias_ref, o_ref):
    z = z_ref[0]
    X = _dft_dot(g_ref[...], z)
    n = X.shape[0] // 2
    pr, pi = _cmul(X[:n], X[n:], h_ref[0], h_ref[1])
    y = _dft_dot(gi_ref[...], jnp.concatenate([pr, pi], axis=0))
    o_ref[0] = (x0_ref[0] * (y + z * bias_ref[...])).astype(o_ref.dtype)


def long_conv_dense(x0, z, hfb, nrm, bias):
    B, L, C = z.shape
    N = 2 * L
    g, gi = _dense_tables(L)
    full2 = lambda shape: pl.BlockSpec(shape, lambda *_: (0,) * len(shape))
    H = pl.pallas_call(
        _dense_spec_kernel,
        grid=(1,),
        in_specs=[full2((2 * N, L)), full2((L, 2 * C)), full2((1, 2 * C))],
        out_specs=full2((2, N, C)),
        out_shape=jax.ShapeDtypeStruct((2, N, C), F32),
        compiler_params=_cparams("arbitrary"),
        name="dense_filter_spectrum",
    )(g, hfb, nrm)
    dat = pl.BlockSpec((1, L, C), lambda b: (b, 0, 0))
    return pl.pallas_call(
        _dense_conv_kernel,
        grid=(B,),
        in_specs=[full2((2 * N, L)), full2((L, 2 * N)), full2((2, N, C)), dat, dat, full2((1, C))],
        out_specs=dat,
        out_shape=jax.ShapeDtypeStruct((B, L, C), BF16),
        compiler_params=_cparams("parallel"),
        name="dense_long_conv",
    )(g, gi, H, z, x0, bias.reshape(1, C))


def hyena_mix_pre(u_args, p, dense):
    x, g, shift, scale = u_args
    (w_in, b_in, sc_w, sc_b, f_w1, f_b1, f_w2, f_b2, f_w3, f_b3, f_wout, freq, bias) = p
    L = x.shape[1]
    perm_n2 = None if dense else V7X_LANES
    x0, z = hy_in_gate(x, g, shift, scale, w_in.astype(BF16), b_in, sc_w, sc_b, perm_n2)
    hfb, nrm = hyena_filter(L, f_w1, f_b1, f_w2, f_b2, f_w3, f_b3, f_wout, freq, perm_n2)
    conv = long_conv_dense if dense else long_conv_two_stage
    return conv(x0, z, hfb, nrm, bias)


def _ml_prep_kernel(xm_ref, prev_ref, next_ref, cw_ref, cb_ref, wq_ref, wk_ref, wv_ref, wg_ref, bg_ref,
                    q_ref, k_ref, kt_ref, v_ref, xc_ref, g_ref, *, k_scale):
    xm = xm_ref[0].astype(F32)
    inner = xm.shape[1]
    conv = _conv3_rows(xm, prev_ref, next_ref, cw_ref, cb_ref)
    xc = conv * jax.nn.sigmoid(conv)
    xcb = xc.astype(BF16)
    xc_ref[0] = xcb
    xmb = xm.astype(BF16)
    gw = V7X_MXU_DIM
    gates = bg_ref[...]
    for j in range(inner // gw):
        sl = slice(j * gw, (j + 1) * gw)
        q = _dot(xcb[:, sl], wq_ref[j])
        k = _dot(xcb[:, sl], wk_ref[j])
        v = _dot(xmb[:, sl], wv_ref[j])
        q_ref[0, :, sl] = q.astype(BF16)
        ks = k * k_scale
        k_ref[0, :, sl] = ks.astype(BF16)
        kt_ref[0, sl, :] = ks.T.astype(BF16)
        v_ref[0, :, sl] = v.astype(BF16)
        gates = gates + _dot(q.astype(BF16), wg_ref[j * gw:(j + 1) * gw, :])
        gates = gates + _dot(k.astype(BF16), wg_ref[inner + j * gw:inner + (j + 1) * gw, :])
        gates = gates + _dot(v.astype(BF16), wg_ref[2 * inner + j * gw:2 * inner + (j + 1) * gw, :])
    g_ref[0] = gates


def _block_diag(w, group):
    nb, bs, _ = w.shape
    per = group // bs
    w = w.reshape(nb // per, per, bs, bs)
    eye = jnp.eye(per, dtype=w.dtype)
    dense = jnp.einsum("gpce,pr->gpcre", w, eye)
    return dense.reshape(nb // per, group, group).astype(BF16)


def ml_prep(xz, conv_w, conv_b, wq, wk, wv, w_gate, b_gate, tm=256):
    B, L, C2 = xz.shape
    inner = C2 // 2
    dh = inner // ML_HEADS
    tm = _row_tile(L, tm)
    gw = V7X_MXU_DIM
    ng = inner // gw
    P = V7X_LANES
    wg = jnp.pad(w_gate, ((0, 0), (0, P - w_gate.shape[1]))).astype(BF16)
    bg = jnp.pad(b_gate.reshape(1, -1), ((0, 0), (0, P - b_gate.shape[0])))
    prev, nxt = _halo_specs(tm, L, inner, lambda: 0)
    c2 = lambda shape: pl.BlockSpec(shape, lambda b, i: (0,) * len(shape))
    row = lambda n: pl.BlockSpec((1, tm, n), lambda b, i: (b, i, 0))
    sd = lambda n, dt: jax.ShapeDtypeStruct((B, L, n), dt)
    return pl.pallas_call(
        functools.partial(_ml_prep_kernel, k_scale=dh ** -0.5),
        grid=(B, L // tm),
        in_specs=[row(inner), prev, nxt, c2((3, inner)), c2((1, inner)),
                  c2((ng, gw, gw)), c2((ng, gw, gw)), c2((ng, gw, gw)), c2((3 * inner, P)), c2((1, P))],
        out_specs=(row(inner), row(inner), pl.BlockSpec((1, inner, tm), lambda b, i: (b, 0, i)), row(inner),
                   row(inner), row(P)),
        out_shape=(sd(inner, BF16), sd(inner, BF16), jax.ShapeDtypeStruct((B, inner, L), BF16), sd(inner, BF16),
                   sd(inner, BF16), sd(P, F32)),
        compiler_params=_cparams("parallel", "arbitrary"),
        name="ml_prep",
    )(xz, xz, xz, conv_w, conv_b.reshape(1, inner), _block_diag(wq, gw), _block_diag(wk, gw),
      _block_diag(wv, gw), wg, bg)


def _gates_scan_order(g_ctx, g_lat):
    T = ML_CHUNK

    def lay(g):
        B, L, _ = g.shape
        g = g[..., :4 * ML_HEADS].reshape(B, L // T, T, 2, 2, ML_HEADS)
        return jnp.transpose(g, (3, 0, 5, 4, 1, 2))

    gc, gl = lay(g_ctx), lay(g_lat)
    fwd = jnp.concatenate([gc[0], gl[0]], axis=3)
    bwd = jnp.concatenate([gc[1][..., ::-1, :], gl[1][..., ::-1, :]], axis=3)
    out = jnp.stack([fwd, bwd])
    nct = out.shape[4]
    pad = -nct % V7X_SUBLANES
    return jnp.pad(out, ((0, 0),) * 4 + ((0, pad), (0, 0))), nct


AUX_BMR, AUX_GR, AUX_WINTER, AUX_EMR, AUX_WROW, AUX_GOLD = range(6)


def _ml_gates_kernel(g_ref, aux_ref, bl_scr, me_scr, mp_scr):
    d = pl.program_id(0)
    ig = g_ref[0, 0, 0, 0]
    fg = g_ref[0, 0, 0, 1]
    nct, T = ig.shape
    lf = jnp.minimum(fg, 0.0) - jnp.log1p(jnp.exp(-jnp.abs(fg)))
    lane = lax.broadcasted_iota(jnp.int32, (nct, T), 1)
    rev = d == 1

    def scans(x, op, ident):
        f, r = x, x
        s = 1
        while s < T:
            f = op(f, jnp.where(lane >= s, pltpu.roll(f, s, 1), ident))
            r = op(r, jnp.where(lane < T - s, pltpu.roll(r, T - s, 1), ident))
            s *= 2
        return jnp.where(rev, r, f)

    bcs = scans(lf, jnp.add, 0.0)
    b_last = jnp.sum(lf, axis=1, keepdims=True)
    gr = ig - bcs
    cmax = scans(gr, jnp.maximum, -jnp.inf)
    max_e = b_last + jnp.max(gr, axis=1, keepdims=True)
    bl_scr[...] = jnp.broadcast_to(b_last, bl_scr.shape)
    me_scr[...] = jnp.broadcast_to(max_e, me_scr.shape)

    def body(t, m):
        mp_scr[pl.ds(t, 1), :] = m
        return jnp.maximum(bl_scr[pl.ds(t, 1), :] + m, me_scr[pl.ds(t, 1), :])

    lax.fori_loop(0, nct, body, jnp.zeros((1, V7X_LANES), F32))
    m_prev = mp_scr[:, 0:1]
    a_inter = bcs + m_prev
    m_row = jnp.maximum(a_inter, bcs + cmax)
    m_new = jnp.maximum(b_last + m_prev, max_e)
    rows = {AUX_BMR: bcs - m_row, AUX_GR: gr, AUX_WINTER: jnp.exp(a_inter - m_row),
            AUX_EMR: jnp.exp(-m_row), AUX_WROW: jnp.exp(b_last + gr - m_new),
            AUX_GOLD: jnp.broadcast_to(jnp.exp(b_last + m_prev - m_new), (nct, T))}
    zero = jnp.zeros((nct, T), F32)
    for k in range(V7X_SUBLANES):
        aux_ref[0, 0, 0, :, k, :] = rows.get(k, zero)


def ml_gates(gates):
    _, B, H, _, nct, T = gates.shape
    return pl.pallas_call(
        _ml_gates_kernel,
        grid=(2, B, H),
        in_specs=[pl.BlockSpec((1, 1, 1, 2, nct, T), lambda d, b, h: (d, b, h, 0, 0, 0))],
        out_specs=pl.BlockSpec((1, 1, 1, nct, V7X_SUBLANES, T), lambda d, b, h: (d, b, h, 0, 0, 0)),
        out_shape=jax.ShapeDtypeStruct((2, B, H, nct, V7X_SUBLANES, T), F32),
        scratch_shapes=[pltpu.VMEM((nct, V7X_LANES), F32)] * 3,
        compiler_params=_cparams("arbitrary", "arbitrary", "arbitrary"),
        name="ml_gates",
    )(gates)


SCAN_HEADS = 4


def _mlstm_kernel(qf_ref, kf_ref, ktf_ref, vf_ref, qb_ref, kb_ref, ktb_ref, vb_ref, aux_ref, c0_ref, n0_ref,
                  hf_ref, hb_ref, cf_ref, nf_ref, c_scr, n_scr):
    t = pl.program_id(2)
    nct = pl.num_programs(2)
    T = ML_CHUNK
    dh = qf_ref.shape[-1] // SCAN_HEADS

    @pl.when(t == 0)
    def _():
        c_scr[...] = c0_ref[:, 0]
        n_scr[...] = n0_ref[:, 0]

    row = lax.broadcasted_iota(jnp.int32, (T, T), 0)
    col = lax.broadcasted_iota(jnp.int32, (T, T), 1)
    dirs = ((qf_ref, kf_ref, ktf_ref, vf_ref, hf_ref, col <= row), (qb_ref, kb_ref, ktb_ref, vb_ref, hb_ref, col >= row))
    streams = [(d, j) + dirs[d] for j in range(SCAN_HEADS) for d in range(2)]
    for d, j, q_ref, k_ref, kt_ref, v_ref, h_ref, mask in streams:
        hs = slice(j * dh, (j + 1) * dh)
        ax = aux_ref[d, 0, j, 0]
        axt = jnp.concatenate([ax, jnp.zeros((T - V7X_SUBLANES, T), F32)], axis=0).T
        gr = ax[AUX_GR:AUX_GR + 1, :]
        g_old = ax[AUX_GOLD:AUX_GOLD + 1, 0:1]
        bmr = axt[:, AUX_BMR:AUX_BMR + 1]
        w_inter = axt[:, AUX_WINTER:AUX_WINTER + 1]
        emr = axt[:, AUX_EMR:AUX_EMR + 1]
        w_col = axt[:, AUX_WROW:AUX_WROW + 1]
        pmat = jnp.exp(jnp.where(mask, bmr + gr, -jnp.inf))

        q = q_ref[0, :, hs]
        v = v_ref[0, :, hs]
        kT = kt_ref[0, hs, :]
        n_row = n_scr[d, j, 0:1, :]
        smat = _dot(q, kT) * pmat
        qn = jnp.sum(q.astype(F32) * n_row, axis=1, keepdims=True)
        den = w_inter * qn + jnp.sum(smat, axis=1, keepdims=True)
        num = w_inter * _dot(q, c_scr[d, j].astype(BF16)) + _dot(smat.astype(BF16), v)
        h_ref[0, :, hs] = (num * (1.0 / jnp.maximum(jnp.abs(den), emr))).astype(h_ref.dtype)
        c_scr[d, j] = g_old * c_scr[d, j] + _dot(kT, (v.astype(F32) * w_col).astype(BF16))
        dn = jnp.sum(k_ref[0, :, hs].astype(F32) * w_col, axis=0, keepdims=True)
        n_scr[d, j, 0:1, :] = g_old * n_row + dn

    @pl.when(t == nct - 1)
    def _():
        cf_ref[:, 0] = c_scr[...]
        nf_ref[:, 0] = n_scr[...]


def mlstm_scan(q, k, kt, v, aux, chunk0, c0, n0):
    B, L, inner = q.shape
    H = ML_HEADS
    dh = inner // H
    T = ML_CHUNK
    nc = L // T
    hp = SCAN_HEADS
    fw = pl.BlockSpec((1, T, hp * dh), lambda b, h, t: (b, t, h))
    bw = pl.BlockSpec((1, T, hp * dh), lambda b, h, t: (b, nc - 1 - t, h))
    fwt = pl.BlockSpec((1, hp * dh, T), lambda b, h, t: (b, h, t))
    bwt = pl.BlockSpec((1, hp * dh, T), lambda b, h, t: (b, h, nc - 1 - t))
    cst = pl.BlockSpec((2, 1, hp, dh, dh), lambda b, h, t: (0, b, h, 0, 0))
    nst = pl.BlockSpec((2, 1, hp, V7X_SUBLANES, dh), lambda b, h, t: (0, b, h, 0, 0))
    hshape = jax.ShapeDtypeStruct((B, L, inner), BF16)
    return pl.pallas_call(
        _mlstm_kernel,
        grid=(B, H // hp, nc),
        in_specs=[fw, fw, fwt, fw, bw, bw, bwt, bw,
                  pl.BlockSpec((2, 1, hp, 1, V7X_SUBLANES, T), lambda b, h, t: (0, b, h, chunk0 + t, 0, 0)),
                  cst, nst],
        out_specs=(fw, bw, cst, nst),
        out_shape=(hshape, hshape, jax.ShapeDtypeStruct(c0.shape, F32), jax.ShapeDtypeStruct(n0.shape, F32)),
        scratch_shapes=[pltpu.VMEM((2, hp, dh, dh), F32), pltpu.VMEM((2, hp, V7X_SUBLANES, dh), F32)],
        compiler_params=_cparams("parallel", "parallel", "arbitrary"),
        name="mlstm_scan",
    )(q, k, kt, v, q, k, kt, v, aux, c0, n0)


def _ml_out_down_kernel(hf_ref, hb_ref, xc_ref, z_ref, nw_ref, sk_ref, w_ref, gate_ref, res_ref, o_ref, a_scr):
    h = hf_ref[0].astype(F32) + hb_ref[0].astype(F32)
    dh = h.shape[1] // ML_HEADS
    z = z_ref[0].astype(F32)
    gate = z * jax.nn.sigmoid(z)
    for j in range(ML_HEADS):
        sl = slice(j * dh, (j + 1) * dh)
        seg = h[:, sl]
        mu = jnp.mean(seg, axis=-1, keepdims=True)
        cen = seg - mu
        var = jnp.mean(cen * cen, axis=-1, keepdims=True)
        hn = cen * lax.rsqrt(var + ML_NORM_EPS)
        hs = hn * nw_ref[:, sl] + sk_ref[:, sl] * xc_ref[0, :, sl].astype(F32)
        a_scr[:, sl] = (hs * gate[:, sl]).astype(BF16)
    o_ref[0] = res_ref[0] + gate_ref[0] * _dot(a_scr[...], w_ref[...])


def ml_out_down(hf, hb, xc, xz, norm_w, skip, w_down_b, gate, res, tm=256):
    B, L, inner = hf.shape
    D = w_down_b.shape[1]
    tm = _row_tile(L, tm)
    vec = pl.BlockSpec((1, inner), lambda b, i: (0, 0))
    dat = pl.BlockSpec((1, tm, inner), lambda b, i: (b, i, 0))
    row = pl.BlockSpec((1, tm, D), lambda b, i: (b, i, 0))
    return pl.pallas_call(
        _ml_out_down_kernel,
        grid=(B, L // tm),
        in_specs=[dat, dat, dat, pl.BlockSpec((1, tm, inner), lambda b, i: (b, i, 1)), vec, vec,
                  pl.BlockSpec((inner, D), lambda b, i: (0, 0)), pl.BlockSpec((1, 1, D), lambda b, i: (b, 0, 0)), row],
        out_specs=row,
        out_shape=jax.ShapeDtypeStruct((B, L, D), F32),
        scratch_shapes=[pltpu.VMEM((tm, inner), BF16)],
        compiler_params=_cparams("parallel", "parallel"),
        name="ml_out_down",
    )(hf, hb, xc, xz, norm_w.reshape(1, inner), skip.reshape(1, inner), w_down_b, gate, res)


def mlstm_mix_residual(lat_args, ctx_args, p, gate):
    w_in, conv_w, conv_b, wq, wk, wv, w_gate, b_gate, norm_w, skip, w_down = p
    w_in_b = w_in.astype(BF16)
    zero_b = jnp.zeros((w_in.shape[1],), F32)

    def prep(args):
        x, g, shift, scale = args
        xz = norm_mod_matmul(x, g, shift, scale, w_in_b, zero_b, tm=512)
        return ml_prep(xz, conv_w, conv_b, wq, wk, wv, w_gate, b_gate) + (xz,)

    qc, kc, ktc, vc, _, gates_c, _ = prep(ctx_args)
    q, k, kt, v, xc, gates_l, xz = prep(lat_args)
    B, _, inner = q.shape
    dh = inner // ML_HEADS
    gates, _ = _gates_scan_order(gates_c, gates_l)
    aux = ml_gates(gates)
    c0 = jnp.zeros((2, B, ML_HEADS, dh, dh), F32)
    n0 = jnp.zeros((2, B, ML_HEADS, V7X_SUBLANES, dh), F32)
    _, _, c1, n1 = mlstm_scan(qc, kc, ktc, vc, aux, 0, c0, n0)
    hf, hb, _, _ = mlstm_scan(q, k, kt, v, aux, qc.shape[1] // ML_CHUNK, c1, n1)
    return ml_out_down(hf, hb, xc, xz, norm_w, skip, w_down.astype(BF16), gate, lat_args[0])


def _ffn_kernel(x_ref, xp_ref, xn_ref, ng_ref, sh_ref, sc_ref, wu_ref, cw_ref, cb_ref, wd_ref,
                gate_ref, fg_ref, o_ref, u_scr, *, cols, vertical, final_norm, cb):
    i = pl.program_id(1)
    last = pl.num_programs(1) - 1
    tm = x_ref.shape[1]
    F = wd_ref.shape[0]
    halo = cols if vertical else 0

    def norm_mod(x):
        y = x * lax.rsqrt(jnp.mean(x * x, axis=-1, keepdims=True) + EPS)
        return ((y * ng_ref[...]) * (1.0 + sc_ref[0]) + sh_ref[0]).astype(BF16)

    u_scr[halo:halo + tm] = norm_mod(x_ref[0])
    if vertical:
        u_scr[0:halo] = norm_mod(xp_ref[0])
        u_scr[halo + tm:] = norm_mod(xn_ref[0])
    R = tm + 2 * halo
    rowi = lax.broadcasted_iota(jnp.int32, (R, 1), 0)
    cpos = jnp.bitwise_and(rowi, cols - 1)
    if vertical:
        top_ok = jnp.where(i > 0, 1.0, 0.0)
        bot_ok = jnp.where(i < last, 1.0, 0.0)
        rowmask = jnp.where(rowi < halo, top_ok, jnp.where(rowi >= halo + tm, bot_ok, 1.0))
    acc = jnp.zeros((tm, o_ref.shape[-1]), F32)
    for f in range(F // cb):
        fs = slice(f * cb, (f + 1) * cb)
        g = _dot(u_scr[...], wu_ref[:, F + f * cb:F + (f + 1) * cb])
        a = _dot(u_scr[halo:halo + tm], wu_ref[:, fs])
        if vertical:
            g = g * rowmask
        left = jnp.where(cpos == 0, 0.0, pltpu.roll(g, 1, 0))
        right = jnp.where(cpos == cols - 1, 0.0, pltpu.roll(g, R - 1, 0))
        conv = cb_ref[:, fs]
        for dr in (range(3) if vertical else (1,)):
            sl = slice(dr * halo, dr * halo + tm)
            conv = conv + (cw_ref[3 * dr:3 * dr + 1, fs] * left[sl] + cw_ref[3 * dr + 1:3 * dr + 2, fs] * g[sl]
                           + cw_ref[3 * dr + 2:3 * dr + 3, fs] * right[sl])
        act = ((conv * jax.nn.sigmoid(conv)) * a).astype(BF16)
        acc = acc + _dot(act, wd_ref[fs, :])
    x = x_ref[0] + gate_ref[0] * acc
    if final_norm:
        x = (x * lax.rsqrt(jnp.mean(x * x, axis=-1, keepdims=True) + EPS)) * fg_ref[...]
    o_ref[0] = x


def conv_ffn_residual(x, g, shift, scale, gate, w_up_b, conv_w, conv_b, w_down_b, rows, cols, final_g=None,
                      tm=512):
    B, L, D = x.shape
    F = w_down_b.shape[0]
    assert cols & (cols - 1) == 0 and rows * cols == L
    vertical = rows > 1
    tm = _row_tile(L, tm) if vertical else L
    cb = F
    assert tm % cols == 0
    hb = cols if vertical else HALO_ROWS
    nhb = L // hb
    final_norm = final_g is not None
    fg = final_g if final_norm else jnp.ones((D,), F32)
    vecb = pl.BlockSpec((1, 1, D), lambda b, i: (b, 0, 0))
    full = lambda shape: pl.BlockSpec(shape, lambda b, i: (0, 0), pipeline_mode=pl.Buffered(1))
    row = pl.BlockSpec((1, tm, D), lambda b, i: (b, i, 0))
    return pl.pallas_call(
        functools.partial(_ffn_kernel, cols=cols, vertical=vertical, final_norm=final_norm, cb=cb),
        grid=(B, L // tm),
        in_specs=[row,
                  pl.BlockSpec((1, hb, D), lambda b, i: (b, jnp.maximum(i * (tm // hb) - 1, 0), 0)),
                  pl.BlockSpec((1, hb, D), lambda b, i: (b, jnp.minimum((i + 1) * (tm // hb), nhb - 1), 0)),
                  full((1, D)), vecb, vecb, full((D, 2 * F)), full((9, F)), full((1, F)), full((F, D)),
                  vecb, full((1, D))],
        out_specs=row,
        out_shape=jax.ShapeDtypeStruct((B, L, D), F32),
        scratch_shapes=[pltpu.VMEM((tm + (2 * cols if vertical else 0), D), BF16)],
        compiler_params=_cparams("parallel", "arbitrary"),
        name="conv_ffn",
    )(x, x, x, g.reshape(1, D), shift, scale, w_up_b, conv_w.reshape(9, F), conv_b.reshape(1, F),
      w_down_b, gate, fg.reshape(1, D))


def kernel(x, c, ctx, c_ctx, mod_w, mod_b, norm_g, final_g, hy_w_in, hy_b_in, hy_sc_w, hy_sc_b, hy_f_w1, hy_f_b1, hy_f_w2, hy_f_b2, hy_f_w3, hy_f_b3, hy_f_wout, hy_freq, hy_bias, hy_w_out, hy_b_out, ml_w_in, ml_conv_w, ml_conv_b, ml_wq, ml_wk, ml_wv, ml_w_gate, ml_b_gate, ml_norm_w, ml_skip, ml_w_down, ffn_w_up, ffn_conv_w, ffn_conv_b, ffn_w_down):
    B, L, D = x.shape
    ctx_len = ctx.shape[1]
    depth = mod_w.shape[0]
    n_mixers = 2
    rows = L // GRID_W
    hy_params = (hy_w_in, hy_b_in, hy_sc_w, hy_sc_b, hy_f_w1, hy_f_b1, hy_f_w2, hy_f_b2,
                 hy_f_w3, hy_f_b3, hy_f_wout, hy_freq, hy_bias)
    ml_params = (ml_w_in, ml_conv_w, ml_conv_b, ml_wq, ml_wk, ml_wv, ml_w_gate, ml_b_gate,
                 ml_norm_w, ml_skip, ml_w_down)
    cond = jnp.concatenate([c, c_ctx.reshape(1, D), jnp.zeros((V7X_SUBLANES - B - 1, D), F32)], axis=0)
    for i in range(depth):
        last = i == depth - 1
        mod = adaln(cond, mod_w[i], mod_b[i])
        lat = [mod[:B, k * D:(k + 1) * D].reshape(B, 1, D) for k in range(6)]
        cm = [jnp.broadcast_to(mod[B:B + 1, k * D:(k + 1) * D].reshape(1, 1, D), (B, 1, D)) for k in range(6)]
        lat_args = (x, norm_g[i, 0], lat[0], lat[1])
        ctx_args = (ctx, norm_g[i, 0], cm[0], cm[1])
        j = i // n_mixers
        if i % n_mixers == 0:
            p = tuple(a[j] for a in hy_params)
            w_out_b = hy_w_out[j].astype(BF16)
            x = res_gate_matmul(hyena_mix_pre(lat_args, p, dense=False), w_out_b, hy_b_out[j], lat[2], x)
            if not last:
                ctx = res_gate_matmul(hyena_mix_pre(ctx_args, p, dense=True), w_out_b, hy_b_out[j], cm[2], ctx)
        else:
            assert last, "the mLSTM mixer is only implemented for the last layer (no context output)"
            p = tuple(a[j] for a in ml_params)
            x = mlstm_mix_residual(lat_args, ctx_args, p, lat[2])
        w_up_b = ffn_w_up[i].astype(BF16)
        w_down_b = ffn_w_down[i].astype(BF16)
        x = conv_ffn_residual(x, norm_g[i, 1], lat[3], lat[4], lat[5], w_up_b, ffn_conv_w[i], ffn_conv_b[i],
                              w_down_b, rows, GRID_W, final_g if last else None)
        if not last:
            ctx = conv_ffn_residual(ctx, norm_g[i, 1], cm[3], cm[4], cm[5], w_up_b, ffn_conv_w[i],
                                    ffn_conv_b[i], w_down_b, 1, ctx_len)
    return x
```

```python
import functools
import math

import jax
import jax.numpy as jnp
from jax import lax
from jax.experimental import pallas as pl
from jax.experimental.pallas import tpu as pltpu

F32 = jnp.float32
BF16 = jnp.bfloat16
HIGHEST = lax.Precision.HIGHEST

EPS = 1e-6
ML_NORM_EPS = 1e-5
GRID_W = 64
ML_HEADS = 4
ML_CHUNK = 128
ML_QKV_BLOCK = 4
HY_EMB_BANDS = 16
HY_MAX_DECAY = math.log(1e-2) / 0.3
HY_MIN_DECAY = math.log(1e-2) / 1.5

V7X_LANES = 128
V7X_SUBLANES = 8
V7X_MXU_DIM = 256
V7X_VMEM_BYTES = 64 * 1024 * 1024
VMEM_LIMIT = V7X_VMEM_BYTES - 8 * 1024 * 1024
HALO_ROWS = 16


def _cparams(*sem):
    return pltpu.CompilerParams(dimension_semantics=("arbitrary",) * len(sem), vmem_limit_bytes=VMEM_LIMIT)


def _dot(a, b, precision=None):
    return jnp.dot(a, b, preferred_element_type=F32, precision=precision)


def _row_tile(n, target):
    t = min(n, target)
    assert n % t == 0, (n, t)
    return t


def _adaln_kernel(c_ref, w_ref, b_ref, o_ref):
    c = c_ref[...]
    s = c * jax.nn.sigmoid(c)
    o_ref[...] = _dot(s, w_ref[...], HIGHEST) + b_ref[...]


def adaln(cond8, w, b):
    r, d = cond8.shape
    n = w.shape[1]
    tn = _row_tile(n, 1536)
    return pl.pallas_call(
        _adaln_kernel,
        grid=(n // tn,),
        in_specs=[pl.BlockSpec((r, d), lambda j: (0, 0)),
                  pl.BlockSpec((d, tn), lambda j: (0, j)),
                  pl.BlockSpec((1, tn), lambda j: (0, j))],
        out_specs=pl.BlockSpec((r, tn), lambda j: (0, j)),
        out_shape=jax.ShapeDtypeStruct((r, n), F32),
        compiler_params=_cparams("arbitrary"),
        name="adaln",
    )(cond8, w, b.reshape(1, n))


def _nmm_kernel(x_ref, g_ref, sh_ref, sc_ref, w_ref, b_ref, o_ref):
    x = x_ref[0]
    y = x * lax.rsqrt(jnp.mean(x * x, axis=-1, keepdims=True) + EPS)
    u = (y * g_ref[...]) * (1.0 + sc_ref[0]) + sh_ref[0]
    acc = _dot(u.astype(BF16), w_ref[...])
    o_ref[0] = (acc + b_ref[...]).astype(o_ref.dtype)


def norm_mod_matmul(x, g, shift, scale, w_bf16, bias, tm, out_dtype=BF16):
    B, L, D = x.shape
    n = w_bf16.shape[1]
    tm = _row_tile(L, tm)
    return pl.pallas_call(
        _nmm_kernel,
        grid=(B, L // tm),
        in_specs=[pl.BlockSpec((1, tm, D), lambda b, i: (b, i, 0)),
                  pl.BlockSpec((1, D), lambda b, i: (0, 0)),
                  pl.BlockSpec((1, 1, D), lambda b, i: (b, 0, 0)),
                  pl.BlockSpec((1, 1, D), lambda b, i: (b, 0, 0)),
                  pl.BlockSpec((D, n), lambda b, i: (0, 0)),
                  pl.BlockSpec((1, n), lambda b, i: (0, 0))],
        out_specs=pl.BlockSpec((1, tm, n), lambda b, i: (b, i, 0)),
        out_shape=jax.ShapeDtypeStruct((B, L, n), out_dtype),
        compiler_params=_cparams("parallel", "parallel"),
        name="norm_mod_matmul",
    )(x, g.reshape(1, D), shift, scale, w_bf16, bias.reshape(1, n))


def _rgm_kernel(a_ref, w_ref, b_ref, gate_ref, res_ref, fg_ref, o_ref, *, final_norm):
    acc = _dot(a_ref[0], w_ref[...]) + b_ref[...]
    x = res_ref[0] + gate_ref[0] * acc
    if final_norm:
        x = (x * lax.rsqrt(jnp.mean(x * x, axis=-1, keepdims=True) + EPS)) * fg_ref[...]
    o_ref[0] = x


def res_gate_matmul(a_bf16, w_bf16, bias, gate, res, final_g=None, tm=512):
    B, L, K = a_bf16.shape
    D = w_bf16.shape[1]
    tm = _row_tile(L, tm)
    final_norm = final_g is not None
    fg = final_g if final_norm else jnp.ones((D,), F32)
    return pl.pallas_call(
        functools.partial(_rgm_kernel, final_norm=final_norm),
        grid=(B, L // tm),
        in_specs=[pl.BlockSpec((1, tm, K), lambda b, i: (b, i, 0)),
                  pl.BlockSpec((K, D), lambda b, i: (0, 0)),
                  pl.BlockSpec((1, D), lambda b, i: (0, 0)),
                  pl.BlockSpec((1, 1, D), lambda b, i: (b, 0, 0)),
                  pl.BlockSpec((1, tm, D), lambda b, i: (b, i, 0)),
                  pl.BlockSpec((1, D), lambda b, i: (0, 0))],
        out_specs=pl.BlockSpec((1, tm, D), lambda b, i: (b, i, 0)),
        out_shape=jax.ShapeDtypeStruct((B, L, D), F32),
        compiler_params=_cparams("parallel", "parallel"),
        name="res_gate_matmul",
    )(a_bf16, w_bf16, bias.reshape(1, D), gate, res, fg.reshape(1, D))


def _halo_specs(tm, L, C, cmap):
    r = HALO_ROWS
    nb = L // r
    prev = pl.BlockSpec((1, r, C), lambda b, i, *a: (b, jnp.maximum(i * (tm // r) - 1, 0), cmap(*a)))
    nxt = pl.BlockSpec((1, r, C), lambda b, i, *a: (b, jnp.minimum((i + 1) * (tm // r), nb - 1), cmap(*a)))
    return prev, nxt


def _conv3_rows(x, prev_ref, next_ref, w_ref, b_ref):
    tm = x.shape[0]
    i = pl.program_id(1)
    last = pl.num_programs(1) - 1
    prev_row = jnp.where(i > 0, prev_ref[0].astype(F32)[HALO_ROWS - 1:HALO_ROWS, :], 0.0)
    next_row = jnp.where(i < last, next_ref[0].astype(F32)[0:1, :], 0.0)
    row = lax.broadcasted_iota(jnp.int32, x.shape, 0)
    xm1 = jnp.where(row == 0, prev_row, pltpu.roll(x, 1, 0))
    xp1 = jnp.where(row == tm - 1, next_row, pltpu.roll(x, tm - 1, 0))
    return w_ref[0:1, :] * xm1 + w_ref[1:2, :] * x + w_ref[2:3, :] * xp1 + b_ref[...]


def _x_halo_specs(tm, L, D):
    r = HALO_ROWS
    nb = L // r
    return (pl.BlockSpec((1, tm, D), lambda b, i: (b, i, 0)),
            pl.BlockSpec((1, r, D), lambda b, i: (b, jnp.maximum(i * (tm // r) - 1, 0), 0)),
            pl.BlockSpec((1, r, D), lambda b, i: (b, jnp.minimum((i + 1) * (tm // r), nb - 1), 0)))


def _norm_mod_ext(x_ref, xp_ref, xn_ref, ng_ref, sh_ref, sc_ref, u_scr):
    tm = x_ref.shape[1]
    h = HALO_ROWS

    def norm_mod(x):
        y = x * lax.rsqrt(jnp.mean(x * x, axis=-1, keepdims=True) + EPS)
        return ((y * ng_ref[...]) * (1.0 + sc_ref[0]) + sh_ref[0]).astype(BF16)

    u_scr[0:h] = norm_mod(xp_ref[0])
    u_scr[h:h + tm] = norm_mod(x_ref[0])
    u_scr[h + tm:] = norm_mod(xn_ref[0])
    i = pl.program_id(1)
    rowi = lax.broadcasted_iota(jnp.int32, (tm + 2 * h, 1), 0)
    top_ok = jnp.where(i > 0, 1.0, 0.0)
    bot_ok = jnp.where(i < pl.num_programs(1) - 1, 1.0, 0.0)
    return jnp.where(rowi < h, top_ok, jnp.where(rowi >= h + tm, bot_ok, 1.0))


def _conv3_ext(p, w_ref, b_ref, cs, tm):
    h = HALO_ROWS
    R = p.shape[0]
    return (w_ref[0:1, cs] * pltpu.roll(p, 1, 0)[h:h + tm] + w_ref[1:2, cs] * p[h:h + tm]
            + w_ref[2:3, cs] * pltpu.roll(p, R - 1, 0)[h:h + tm] + b_ref[:, cs])


def _hy_in_gate_kernel(x_ref, xp_ref, xn_ref, ng_ref, sh_ref, sc_ref, w_ref, b_ref, cw_ref, cb_ref,
                       x0_ref, z_ref, u_scr, *, cb):
    tm, D = x_ref.shape[1:]
    rowmask = _norm_mod_ext(x_ref, xp_ref, xn_ref, ng_ref, sh_ref, sc_ref, u_scr)
    for c in range(D // cb):
        conv = []
        for k in range(3):
            cs = slice(k * D + c * cb, k * D + (c + 1) * cb)
            p = (_dot(u_scr[...], w_ref[:, cs]) + b_ref[:, cs]) * rowmask
            conv.append(_conv3_ext(p, cw_ref, cb_ref, cs, tm))
        cs = slice(c * cb, (c + 1) * cb)
        x0_ref[0, :, cs] = conv[0]
        z_ref[0, :, cs] = conv[1] * conv[2]


def _hy_in_gate_perm_kernel(x_ref, xp_ref, xn_ref, ng_ref, sh_ref, sc_ref, w_ref, b_ref, cw_ref, cb_ref,
                            x0_ref, z_ref, u_scr):
    tm, D = x_ref.shape[1:]
    n2, J = x0_ref.shape[1:3]
    i = pl.program_id(1)

    def norm_mod(x):
        y = x * lax.rsqrt(jnp.mean(x * x, axis=-1, keepdims=True) + EPS)
        return (y * ng_ref[...]) * (1.0 + sc_ref[0]) + sh_ref[0]

    u = pltpu.einshape("jnc->njc", norm_mod(x_ref[0]).reshape(J, n2, D)).reshape(tm, D)
    u_scr[0:tm] = u.astype(BF16)
    u_scr[tm:tm + J] = norm_mod(xp_ref[0][HALO_ROWS - J:]).astype(BF16)
    u_scr[tm + J:] = norm_mod(xn_ref[0][:J]).astype(BF16)
    top_ok = jnp.where(i > 0, 1.0, 0.0)
    bot_ok = jnp.where(i < pl.num_programs(1) - 1, 1.0, 0.0)
    jrow = lax.broadcasted_iota(jnp.int32, (J, 1), 0)
    conv = []
    for k in range(3):
        cs = slice(k * D, (k + 1) * D)
        p = _dot(u_scr[...], w_ref[:, cs]) + b_ref[:, cs]
        pm = p[0:tm]
        before = p[tm + J - 1:tm + J] * top_ok
        after = p[tm + J:tm + J + 1] * bot_ok
        first_prev = jnp.where(jrow == 0, before, pltpu.roll(pm[tm - J:tm], 1, 0))
        last_next = jnp.where(jrow == J - 1, after, pltpu.roll(pm[0:J], J - 1, 0))
        xm1 = jnp.concatenate([first_prev, pm[0:tm - J]], axis=0)
        xp1 = jnp.concatenate([pm[J:tm], last_next], axis=0)
        conv.append(cw_ref[0:1, cs] * xm1 + cw_ref[1:2, cs] * pm + cw_ref[2:3, cs] * xp1 + cb_ref[:, cs])
    x0_ref[0] = conv[0].reshape(n2, J, D)
    z_ref[0] = (conv[1] * conv[2]).reshape(n2, J, D)


def hy_in_gate(x, g, shift, scale, w_bf16, b_in, sc_w, sc_b, perm_n2=None):
    B, L, D = x.shape
    C3 = w_bf16.shape[1]
    perm = perm_n2 is not None
    tm = V7X_SUBLANES * perm_n2 if perm else L
    assert L % tm == 0
    full = lambda shape: pl.BlockSpec(shape, lambda b, i: (0, 0))
    vecb = pl.BlockSpec((1, 1, D), lambda b, i: (b, 0, 0))
    if perm:
        out = jax.ShapeDtypeStruct((B, perm_n2, L // perm_n2, D), F32)
        ospec = pl.BlockSpec((1, perm_n2, V7X_SUBLANES, D), lambda b, i: (b, 0, i, 0))
    else:
        out = jax.ShapeDtypeStruct((B, L, D), F32)
        ospec = pl.BlockSpec((1, tm, D), lambda b, i: (b, i, 0))
    kern = _hy_in_gate_perm_kernel if perm else functools.partial(_hy_in_gate_kernel, cb=D)
    return pl.pallas_call(
        kern,
        grid=(B, L // tm),
        in_specs=[*_x_halo_specs(tm, L, D), full((1, D)), vecb, vecb, full((D, C3)), full((1, C3)),
                  full((3, C3)), full((1, C3))],
        out_specs=(ospec, ospec),
        out_shape=(out, out),
        scratch_shapes=[pltpu.VMEM((tm + 2 * (V7X_SUBLANES if perm else HALO_ROWS), D), BF16)],
        compiler_params=_cparams("parallel", "arbitrary"),
        name="hy_in_gate",
    )(x, x, x, g.reshape(1, D), shift, scale, w_bf16, b_in.reshape(1, C3), sc_w, sc_b.reshape(1, C3))


def _filter_kernel(w1_ref, b1_ref, w2_ref, b2_ref, w3_ref, b3_ref, wo_ref, fr_ref, band_ref,
                   h_ref, s_ref, *, L, D, tl, perm):
    i = pl.program_id(0)
    rowi = lax.broadcasted_iota(jnp.int32, (tl, 1), 0)
    if perm:
        n2, J = h_ref.shape[1:3]
        rowi = jnp.bitwise_and(rowi, J - 1) * n2 + jnp.right_shift(rowi, J.bit_length() - 1)
    rowi = rowi + i * tl
    row = rowi.astype(F32)
    t = row / (L - 1.0)
    w = (2.0 * math.pi) * row / L
    col = lax.broadcasted_iota(jnp.int32, (tl, V7X_LANES), 1)
    ang = w * band_ref[...]
    pos = jnp.where(col == 0, t,
                    jnp.where(col <= HY_EMB_BANDS, jnp.cos(ang),
                              jnp.where(col <= 2 * HY_EMB_BANDS, -jnp.sin(ang), 0.0)))
    fr = fr_ref[...]
    h = jnp.sin(fr * (_dot(pos, w1_ref[...], HIGHEST) + b1_ref[...]))
    h = jnp.sin(fr * (_dot(h, w2_ref[...], HIGHEST) + b2_ref[...]))
    h = jnp.sin(fr * (_dot(h, w3_ref[...], HIGHEST) + b3_ref[...]))
    h = _dot(h, wo_ref[...], HIGHEST)
    dcol = lax.broadcasted_iota(jnp.int32, (1, 2 * D), 1)
    chan = jnp.where(dcol >= D, dcol - D, dcol).astype(F32)
    deltas = HY_MIN_DECAY + chan * ((HY_MAX_DECAY - HY_MIN_DECAY) / (D - 1.0))
    h = h * jnp.exp(-t * jnp.abs(deltas))
    h = jnp.where((rowi == 0) & (dcol >= D), 0.0, h)
    h_ref[...] = h.reshape(h_ref.shape)
    part = jnp.sum(jnp.abs(h), axis=0, keepdims=True)

    @pl.when(i == 0)
    def _():
        s_ref[...] = part

    @pl.when(i > 0)
    def _():
        s_ref[...] += part


def hyena_filter(L, f_w1, f_b1, f_w2, f_b2, f_w3, f_b3, f_wout, freq, perm_n2=None):
    W = f_w2.shape[0]
    D2 = f_wout.shape[1]
    P = V7X_LANES
    pad2 = lambda a: jnp.pad(a, ((0, P - a.shape[0]), (0, P - a.shape[1])))
    padv = lambda a: jnp.pad(a.reshape(1, -1), ((0, 0), (0, P - a.shape[0])))
    w1 = pad2(f_w1)
    w2 = pad2(f_w2)
    w3 = pad2(f_w3)
    wo = jnp.pad(f_wout, ((0, P - W), (0, 0)))
    bands = jnp.linspace(1e-4, HY_EMB_BANDS - 1, HY_EMB_BANDS, dtype=F32)
    band_row = jnp.concatenate([jnp.zeros((1,), F32), bands, bands,
                                jnp.zeros((P - 1 - 2 * HY_EMB_BANDS,), F32)]).reshape(1, P)
    perm = perm_n2 is not None
    tl = V7X_SUBLANES * perm_n2 if perm else _row_tile(L, 512)
    assert L % tl == 0
    full = lambda shape: pl.BlockSpec(shape, lambda i: (0, 0))
    if perm:
        hshape = jax.ShapeDtypeStruct((1, perm_n2, L // perm_n2, D2), F32)
        hspec = pl.BlockSpec((1, perm_n2, V7X_SUBLANES, D2), lambda i: (0, 0, i, 0))
    else:
        hshape = jax.ShapeDtypeStruct((L, D2), F32)
        hspec = pl.BlockSpec((tl, D2), lambda i: (i, 0))
    return pl.pallas_call(
        functools.partial(_filter_kernel, L=L, D=D2 // 2, tl=tl, perm=perm),
        grid=(L // tl,),
        in_specs=[full((P, P)), full((1, P)), full((P, P)), full((1, P)), full((P, P)), full((1, P)),
                  full((P, D2)), full((1, P)), full((1, P))],
        out_specs=(hspec, pl.BlockSpec((1, D2), lambda i: (0, 0))),
        out_shape=(hshape, jax.ShapeDtypeStruct((1, D2), F32)),
        compiler_params=_cparams("arbitrary"),
        name="hyena_filter",
    )(w1, padv(f_b1), w2, padv(f_b2), w3, padv(f_b3), wo, padv(freq), band_row)


DFT_DTYPE = BF16
K1_GROUP = HALO_ROWS


def _cos_sin(p, n):
    ang = (2.0 * math.pi / n) * p.astype(F32)
    return jnp.cos(ang), jnp.sin(ang)


def _dft_dot(g, x):
    return _dot(g.astype(DFT_DTYPE), x.astype(DFT_DTYPE), HIGHEST if DFT_DTYPE == F32 else None)


def _dft_tables(N1, N2):
    N = N1 * N2
    ar = lambda n: jnp.arange(n, dtype=jnp.int32)
    c, s = _cos_sin((ar(N1)[:, None] * ar(N1 // 2)[None, :]) % N1, N1)
    g1 = jnp.concatenate([c, -s], axis=0)
    k1 = ar(N1)[:, None, None]
    a = ar(N2)[None, :, None]
    b = ar(N2)[None, None, :]
    c, s = _cos_sin((b * (k1 + N1 * a)) % N, N)
    g2 = jnp.concatenate([jnp.concatenate([c, s], axis=2), jnp.concatenate([-s, c], axis=2)], axis=1)
    c, s = jnp.swapaxes(c, 1, 2), jnp.swapaxes(s, 1, 2)
    g2i = jnp.concatenate([jnp.concatenate([c, -s], axis=2), jnp.concatenate([s, c], axis=2)], axis=1)
    c, s = _cos_sin((ar(N1 // 2)[:, None] * ar(N1)[None, :]) % N1, N1)
    g4 = jnp.concatenate([c, -s], axis=1) * (1.0 / N)
    return tuple(t.astype(DFT_DTYPE) for t in (g1, g2, g2i, g4))


def _stage1_kernel(g_ref, x_ref, o_ref, acc_ref):
    for s in range(x_ref.shape[1]):
        acc_ref[:, s, :] = _dft_dot(g_ref[...], x_ref[0, s])
    o_ref[0] = acc_ref[...].astype(o_ref.dtype)


def dft_stage1(g, xp, ns, cl):
    B, N2, K, C = xp.shape
    M = g.shape[0]
    return pl.pallas_call(
        _stage1_kernel,
        grid=(B, N2 // ns, C // cl),
        in_specs=[pl.BlockSpec((M, K), lambda b, j, c: (0, 0)),
                  pl.BlockSpec((1, ns, K, cl), lambda b, j, c: (b, j, 0, c))],
        out_specs=pl.BlockSpec((1, M, ns, cl), lambda b, j, c: (b, 0, j, c)),
        out_shape=jax.ShapeDtypeStruct((B, M, N2, C), DFT_DTYPE),
        scratch_shapes=[pltpu.VMEM((M, ns, cl), F32)],
        compiler_params=_cparams("parallel", "parallel", "parallel"),
        name="dft_stage1",
    )(g, xp)


def _combine_spectrum(X, nrm_ref, D):
    n = X.shape[0] // 2
    nrm = nrm_ref[:, :D] + nrm_ref[:, D:]
    hre = (X[:n, :D] + X[:n, D:]) / nrm
    him = (X[n:, :D] - X[n:, D:]) / nrm
    return hre, him


def _filter_spec_kernel(g_ref, a_ref, nrm_ref, o_ref):
    _, _, n2, D2 = a_ref.shape
    slab = a_ref[:, 0].reshape(2 * n2, D2)
    X = _dft_dot(g_ref[0], slab)
    hre, him = _combine_spectrum(X, nrm_ref, D2 // 2)
    o_ref[0, 0] = hre
    o_ref[1, 0] = him


def filter_spectrum(g2, a, nrm):
    _, N1, N2, D2 = a.shape
    D = D2 // 2
    return pl.pallas_call(
        _filter_spec_kernel,
        grid=(N1,),
        in_specs=[pl.BlockSpec((1, 2 * N2, 2 * N2), lambda k: (k, 0, 0)),
                  pl.BlockSpec((2, 1, N2, D2), lambda k: (0, k, 0, 0)),
                  pl.BlockSpec((1, D2), lambda k: (0, 0))],
        out_specs=pl.BlockSpec((2, 1, N2, D), lambda k: (0, k, 0, 0)),
        out_shape=jax.ShapeDtypeStruct((2, N1, N2, D), F32),
        compiler_params=_cparams("parallel"),
        name="filter_spectrum",
    )(g2, a, nrm)


def _cmul(xr, xi, hr, hi):
    return xr * hr - xi * hi, xr * hi + xi * hr


def _spec_mul_kernel(g_ref, gi_ref, h_ref, a_ref, o_ref):
    n2, C = a_ref.shape[-2:]
    ys = []
    for j in range(a_ref.shape[2]):
        slab = a_ref[0, :, j].reshape(2 * n2, C)
        X = _dft_dot(g_ref[j], slab)
        pr, pi = _cmul(X[:n2], X[n2:], h_ref[0, j], h_ref[1, j])
        ys.append(_dft_dot(gi_ref[j], jnp.concatenate([pr, pi], axis=0)))
    t = pltpu.einshape("jrc->rjc", jnp.stack(ys)).astype(o_ref.dtype)
    o_ref[0, :, 0] = t[:n2]
    o_ref[0, :, 1] = t[n2:]


def spectrum_multiply(g2, g2i, H, a, cl):
    B, _, N1, N2, C = a.shape
    kg = K1_GROUP
    gsp = pl.BlockSpec((kg, 2 * N2, 2 * N2), lambda k, c, b: (k, 0, 0))
    return pl.pallas_call(
        _spec_mul_kernel,
        grid=(N1 // kg, C // cl, B),
        in_specs=[gsp, gsp, pl.BlockSpec((2, kg, N2, cl), lambda k, c, b: (0, k, 0, c)),
                  pl.BlockSpec((1, 2, kg, N2, cl), lambda k, c, b: (b, 0, k, 0, c))],
        out_specs=pl.BlockSpec((1, N2, 2, kg, cl), lambda k, c, b: (b, 0, 0, k, c)),
        out_shape=jax.ShapeDtypeStruct((B, N2, 2, N1, C), DFT_DTYPE),
        compiler_params=_cparams("parallel", "parallel", "parallel"),
        name="spectrum_multiply",
    )(g2, g2i, H, a)


def _idft_out_kernel(g_ref, y_ref, x0_ref, z_ref, bias_ref, o_ref, acc_ref):
    for s in range(y_ref.shape[1]):
        y = _dft_dot(g_ref[...], y_ref[0, s])
        acc_ref[:, s, :] = x0_ref[0, s] * (y + z_ref[0, s] * bias_ref[...])
    o_ref[0] = acc_ref[...].astype(o_ref.dtype)


def idft_gate_out(g4, yv, x0p, zp, bias, ns, cl):
    B, N2, K, C = yv.shape
    M = g4.shape[0]
    pdat = pl.BlockSpec((1, ns, M, cl), lambda b, j, c: (b, j, 0, c))
    return pl.pallas_call(
        _idft_out_kernel,
        grid=(B, N2 // ns, C // cl),
        in_specs=[pl.BlockSpec((M, K), lambda b, j, c: (0, 0)),
                  pl.BlockSpec((1, ns, K, cl), lambda b, j, c: (b, j, 0, c)),
                  pdat, pdat, pl.BlockSpec((1, cl), lambda b, j, c: (0, c))],
        out_specs=pl.BlockSpec((1, M, ns, cl), lambda b, j, c: (b, 0, j, c)),
        out_shape=jax.ShapeDtypeStruct((B, M, N2, C), BF16),
        scratch_shapes=[pltpu.VMEM((M, ns, cl), F32)],
        compiler_params=_cparams("parallel", "parallel", "parallel"),
        name="idft_gate_out",
    )(g4, yv, x0p, zp, bias.reshape(1, C))


def long_conv_two_stage(x0p, zp, hfbp, nrm, bias):
    B, N2, M, C = zp.shape
    L = N2 * M
    N1 = 2 * M
    g1, g2, g2i, g4 = _dft_tables(N1, N2)
    ns = HALO_ROWS
    cl = min(C, 512)
    a_f = dft_stage1(g1, hfbp, ns, cl)
    H = filter_spectrum(g2, a_f.reshape(2, N1, N2, 2 * C), nrm)
    a = dft_stage1(g1, zp, ns, cl)
    yv = spectrum_multiply(g2, g2i, H, a.reshape(B, 2, N1, N2, C), cl // 2)
    out = idft_gate_out(g4, yv.reshape(B, N2, 2 * N1, C), x0p, zp, bias, ns, cl)
    return out.reshape(B, L, C)


def _dense_tables(L):
    N = 2 * L
    ar = lambda n: jnp.arange(n, dtype=jnp.int32)
    c, s = _cos_sin((ar(N)[:, None] * ar(L)[None, :]) % N, N)
    g = jnp.concatenate([c, -s], axis=0)
    c, s = _cos_sin((ar(L)[:, None] * ar(N)[None, :]) % N, N)
    gi = jnp.concatenate([c, -s], axis=1) * (1.0 / N)
    return g.astype(DFT_DTYPE), gi.astype(DFT_DTYPE)


def _dense_spec_kernel(g_ref, hfb_ref, nrm_ref, o_ref):
    X = _dft_dot(g_ref[...], hfb_ref[...])
    hre, him = _combine_spectrum(X, nrm_ref, o_ref.shape[-1])
    o_ref[0] = hre
    o_ref[1] = him


def _dense_conv_kernel(g_ref, gi_ref, h_ref, z_ref, x0_ref, bias_ref, o_ref):
    z = z_ref[0]
    X = _dft_dot(g_ref[...], z)
    n = X.shape[0] // 2
    pr, pi = _cmul(X[:n], X[n:], h_ref[0], h_ref[1])
    y = _dft_dot(gi_ref[...], jnp.concatenate([pr, pi], axis=0))
    o_ref[0] = (x0_ref[0] * (y + z * bias_ref[...])).astype(o_ref.dtype)


def long_conv_dense(x0, z, hfb, nrm, bias):
    B, L, C = z.shape
    N = 2 * L
    g, gi = _dense_tables(L)
    full2 = lambda shape: pl.BlockSpec(shape, lambda *_: (0,) * len(shape))
    H = pl.pallas_call(
        _dense_spec_kernel,
        grid=(1,),
        in_specs=[full2((2 * N, L)), full2((L, 2 * C)), full2((1, 2 * C))],
        out_specs=full2((2, N, C)),
        out_shape=jax.ShapeDtypeStruct((2, N, C), F32),
        compiler_params=_cparams("arbitrary"),
        name="dense_filter_spectrum",
    )(g, hfb, nrm)
    dat = pl.BlockSpec((1, L, C), lambda b: (b, 0, 0))
    return pl.pallas_call(
        _dense_conv_kernel,
        grid=(B,),
        in_specs=[full2((2 * N, L)), full2((L, 2 * N)), full2((2, N, C)), dat, dat, full2((1, C))],
        out_specs=dat,
        out_shape=jax.ShapeDtypeStruct((B, L, C), BF16),
        compiler_params=_cparams("parallel"),
        name="dense_long_conv",
    )(g, gi, H, z, x0, bias.reshape(1, C))


def hyena_mix_pre(u_args, p, dense):
    x, g, shift, scale = u_args
    (w_in, b_in, sc_w, sc_b, f_w1, f_b1, f_w2, f_b2, f_w3, f_b3, f_wout, freq, bias) = p
    L = x.shape[1]
    perm_n2 = None if dense else V7X_LANES
    x0, z = hy_in_gate(x, g, shift, scale, w_in.astype(BF16), b_in, sc_w, sc_b, perm_n2)
    hfb, nrm = hyena_filter(L, f_w1, f_b1, f_w2, f_b2, f_w3, f_b3, f_wout, freq, perm_n2)
    conv = long_conv_dense if dense else long_conv_two_stage
    return conv(x0, z, hfb, nrm, bias)


def _ml_prep_kernel(xm_ref, prev_ref, next_ref, cw_ref, cb_ref, wq_ref, wk_ref, wv_ref, wg_ref, bg_ref,
                    q_ref, k_ref, kt_ref, v_ref, xc_ref, g_ref, *, k_scale):
    xm = xm_ref[0].astype(F32)
    inner = xm.shape[1]
    conv = _conv3_rows(xm, prev_ref, next_ref, cw_ref, cb_ref)
    xc = conv * jax.nn.sigmoid(conv)
    xcb = xc.astype(BF16)
    xc_ref[0] = xcb
    xmb = xm.astype(BF16)
    gw = V7X_MXU_DIM
    gates = bg_ref[...]
    for j in range(inner // gw):
        sl = slice(j * gw, (j + 1) * gw)
        q = _dot(xcb[:, sl], wq_ref[j])
        k = _dot(xcb[:, sl], wk_ref[j])
        v = _dot(xmb[:, sl], wv_ref[j])
        q_ref[0, :, sl] = q.astype(BF16)
        ks = k * k_scale
        k_ref[0, :, sl] = ks.astype(BF16)
        kt_ref[0, sl, :] = ks.T.astype(BF16)
        v_ref[0, :, sl] = v.astype(BF16)
        gates = gates + _dot(q.astype(BF16), wg_ref[j * gw:(j + 1) * gw, :])
        gates = gates + _dot(k.astype(BF16), wg_ref[inner + j * gw:inner + (j + 1) * gw, :])
        gates = gates + _dot(v.astype(BF16), wg_ref[2 * inner + j * gw:2 * inner + (j + 1) * gw, :])
    g_ref[0] = gates


def _block_diag(w, group):
    nb, bs, _ = w.shape
    per = group // bs
    w = w.reshape(nb // per, per, bs, bs)
    eye = jnp.eye(per, dtype=w.dtype)
    dense = jnp.einsum("gpce,pr->gpcre", w, eye)
    return dense.reshape(nb // per, group, group).astype(BF16)


def ml_prep(xz, conv_w, conv_b, wq, wk, wv, w_gate, b_gate, tm=256):
    B, L, C2 = xz.shape
    inner = C2 // 2
    dh = inner // ML_HEADS
    tm = _row_tile(L, tm)
    gw = V7X_MXU_DIM
    ng = inner // gw
    P = V7X_LANES
    wg = jnp.pad(w_gate, ((0, 0), (0, P - w_gate.shape[1]))).astype(BF16)
    bg = jnp.pad(b_gate.reshape(1, -1), ((0, 0), (0, P - b_gate.shape[0])))
    prev, nxt = _halo_specs(tm, L, inner, lambda: 0)
    c2 = lambda shape: pl.BlockSpec(shape, lambda b, i: (0,) * len(shape))
    row = lambda n: pl.BlockSpec((1, tm, n), lambda b, i: (b, i, 0))
    sd = lambda n, dt: jax.ShapeDtypeStruct((B, L, n), dt)
    return pl.pallas_call(
        functools.partial(_ml_prep_kernel, k_scale=dh ** -0.5),
        grid=(B, L // tm),
        in_specs=[row(inner), prev, nxt, c2((3, inner)), c2((1, inner)),
                  c2((ng, gw, gw)), c2((ng, gw, gw)), c2((ng, gw, gw)), c2((3 * inner, P)), c2((1, P))],
        out_specs=(row(inner), row(inner), pl.BlockSpec((1, inner, tm), lambda b, i: (b, 0, i)), row(inner),
                   row(inner), row(P)),
        out_shape=(sd(inner, BF16), sd(inner, BF16), jax.ShapeDtypeStruct((B, inner, L), BF16), sd(inner, BF16),
                   sd(inner, BF16), sd(P, F32)),
        compiler_params=_cparams("parallel", "arbitrary"),
        name="ml_prep",
    )(xz, xz, xz, conv_w, conv_b.reshape(1, inner), _block_diag(wq, gw), _block_diag(wk, gw),
      _block_diag(wv, gw), wg, bg)


def _gates_scan_order(g_ctx, g_lat):
    T = ML_CHUNK

    def lay(g):
        B, L, _ = g.shape
        g = g[..., :4 * ML_HEADS].reshape(B, L // T, T, 2, 2, ML_HEADS)
        return jnp.transpose(g, (3, 0, 5, 4, 1, 2))

    gc, gl = lay(g_ctx), lay(g_lat)
    fwd = jnp.concatenate([gc[0], gl[0]], axis=3)
    bwd = jnp.concatenate([gc[1][..., ::-1, :], gl[1][..., ::-1, :]], axis=3)
    out = jnp.stack([fwd, bwd])
    nct = out.shape[4]
    pad = -nct % V7X_SUBLANES
    return jnp.pad(out, ((0, 0),) * 4 + ((0, pad), (0, 0))), nct


AUX_BMR, AUX_GR, AUX_WINTER, AUX_EMR, AUX_WROW, AUX_GOLD = range(6)


def _ml_gates_kernel(g_ref, aux_ref, bl_scr, me_scr, mp_scr):
    d = pl.program_id(0)
    ig = g_ref[0, 0, 0, 0]
    fg = g_ref[0, 0, 0, 1]
    nct, T = ig.shape
    lf = jnp.minimum(fg, 0.0) - jnp.log1p(jnp.exp(-jnp.abs(fg)))
    lane = lax.broadcasted_iota(jnp.int32, (nct, T), 1)
    rev = d == 1

    def scans(x, op, ident):
        f, r = x, x
        s = 1
        while s < T:
            f = op(f, jnp.where(lane >= s, pltpu.roll(f, s, 1), ident))
            r = op(r, jnp.where(lane < T - s, pltpu.roll(r, T - s, 1), ident))
            s *= 2
        return jnp.where(rev, r, f)

    bcs = scans(lf, jnp.add, 0.0)
    b_last = jnp.sum(lf, axis=1, keepdims=True)
    gr = ig - bcs
    cmax = scans(gr, jnp.maximum, -jnp.inf)
    max_e = b_last + jnp.max(gr, axis=1, keepdims=True)
    bl_scr[...] = jnp.broadcast_to(b_last, bl_scr.shape)
    me_scr[...] = jnp.broadcast_to(max_e, me_scr.shape)

    def body(t, m):
        mp_scr[pl.ds(t, 1), :] = m
        return jnp.maximum(bl_scr[pl.ds(t, 1), :] + m, me_scr[pl.ds(t, 1), :])

    lax.fori_loop(0, nct, body, jnp.zeros((1, V7X_LANES), F32))
    m_prev = mp_scr[:, 0:1]
    a_inter = bcs + m_prev
    m_row = jnp.maximum(a_inter, bcs + cmax)
    m_new = jnp.maximum(b_last + m_prev, max_e)
    rows = {AUX_BMR: bcs - m_row, AUX_GR: gr, AUX_WINTER: jnp.exp(a_inter - m_row),
            AUX_EMR: jnp.exp(-m_row), AUX_WROW: jnp.exp(b_last + gr - m_new),
            AUX_GOLD: jnp.broadcast_to(jnp.exp(b_last + m_prev - m_new), (nct, T))}
    zero = jnp.zeros((nct, T), F32)
    for k in range(V7X_SUBLANES):
        aux_ref[0, 0, 0, :, k, :] = rows.get(k, zero)


def ml_gates(gates):
    _, B, H, _, nct, T = gates.shape
    return pl.pallas_call(
        _ml_gates_kernel,
        grid=(2, B, H),
        in_specs=[pl.BlockSpec((1, 1, 1, 2, nct, T), lambda d, b, h: (d, b, h, 0, 0, 0))],
        out_specs=pl.BlockSpec((1, 1, 1, nct, V7X_SUBLANES, T), lambda d, b, h: (d, b, h, 0, 0, 0)),
        out_shape=jax.ShapeDtypeStruct((2, B, H, nct, V7X_SUBLANES, T), F32),
        scratch_shapes=[pltpu.VMEM((nct, V7X_LANES), F32)] * 3,
        compiler_params=_cparams("arbitrary", "arbitrary", "arbitrary"),
        name="ml_gates",
    )(gates)


SCAN_HEADS = 4


def _mlstm_kernel(qf_ref, kf_ref, ktf_ref, vf_ref, qb_ref, kb_ref, ktb_ref, vb_ref, aux_ref, *refs,
                  from_zero, keep_state):
    refs = list(refs)
    c0_ref, n0_ref = (None, None) if from_zero else (refs.pop(0), refs.pop(0))
    hf_ref, hb_ref = refs.pop(0), refs.pop(0)
    cf_ref, nf_ref = (refs.pop(0), refs.pop(0)) if keep_state else (None, None)
    c_scr, n_scr = refs
    t = pl.program_id(2)
    nct = pl.num_programs(2)
    T = ML_CHUNK
    dh = qf_ref.shape[-1] // SCAN_HEADS

    @pl.when(t == 0)
    def _():
        if from_zero:
            c_scr[...] = jnp.zeros_like(c_scr)
            n_scr[...] = jnp.zeros_like(n_scr)
        else:
            c_scr[...] = c0_ref[:, 0]
            n_scr[...] = n0_ref[:, 0]

    row = lax.broadcasted_iota(jnp.int32, (T, T), 0)
    col = lax.broadcasted_iota(jnp.int32, (T, T), 1)
    dirs = ((qf_ref, kf_ref, ktf_ref, vf_ref, hf_ref, col <= row), (qb_ref, kb_ref, ktb_ref, vb_ref, hb_ref, col >= row))
    streams = [(d, j) + dirs[d] for j in range(SCAN_HEADS) for d in range(2)]
    for d, j, q_ref, k_ref, kt_ref, v_ref, h_ref, mask in streams:
        hs = slice(j * dh, (j + 1) * dh)
        ax = aux_ref[d, 0, j, 0]
        axt = jnp.concatenate([ax, jnp.zeros((T - V7X_SUBLANES, T), F32)], axis=0).T
        gr = ax[AUX_GR:AUX_GR + 1, :]
        g_old = ax[AUX_GOLD:AUX_GOLD + 1, 0:1]
        bmr = axt[:, AUX_BMR:AUX_BMR + 1]
        w_inter = axt[:, AUX_WINTER:AUX_WINTER + 1]
        emr = axt[:, AUX_EMR:AUX_EMR + 1]
        w_col = axt[:, AUX_WROW:AUX_WROW + 1]
        pmat = jnp.exp(jnp.where(mask, bmr + gr, -jnp.inf))

        q = q_ref[0, :, hs]
        v = v_ref[0, :, hs]
        kT = kt_ref[0, hs, :]
        n_row = n_scr[d, j, 0:1, :]
        smat = _dot(q, kT) * pmat
        qn = jnp.sum(q.astype(F32) * n_row, axis=1, keepdims=True)
        den = w_inter * qn + jnp.sum(smat, axis=1, keepdims=True)
        num = w_inter * _dot(q, c_scr[d, j].astype(BF16)) + _dot(smat.astype(BF16), v)
        h_ref[0, :, hs] = (num * (1.0 / jnp.maximum(jnp.abs(den), emr))).astype(h_ref.dtype)
        c_scr[d, j] = g_old * c_scr[d, j] + _dot(kT, (v.astype(F32) * w_col).astype(BF16))
        dn = jnp.sum(k_ref[0, :, hs].astype(F32) * w_col, axis=0, keepdims=True)
        n_scr[d, j, 0:1, :] = g_old * n_row + dn

    if keep_state:
        @pl.when(t == nct - 1)
        def _():
            cf_ref[:, 0] = c_scr[...]
            nf_ref[:, 0] = n_scr[...]


def mlstm_scan(q, k, kt, v, aux, chunk0, state=None, keep_state=False):
    B, L, inner = q.shape
    H = ML_HEADS
    dh = inner // H
    T = ML_CHUNK
    nc = L // T
    hp = SCAN_HEADS
    fw = pl.BlockSpec((1, T, hp * dh), lambda b, h, t: (b, t, h))
    bw = pl.BlockSpec((1, T, hp * dh), lambda b, h, t: (b, nc - 1 - t, h))
    fwt = pl.BlockSpec((1, hp * dh, T), lambda b, h, t: (b, h, t))
    bwt = pl.BlockSpec((1, hp * dh, T), lambda b, h, t: (b, h, nc - 1 - t))
    cst = pl.BlockSpec((2, 1, hp, dh, dh), lambda b, h, t: (0, b, h, 0, 0))
    nst = pl.BlockSpec((2, 1, hp, V7X_SUBLANES, dh), lambda b, h, t: (0, b, h, 0, 0))
    hshape = jax.ShapeDtypeStruct((B, L, inner), BF16)
    sshape = (jax.ShapeDtypeStruct((2, B, H, dh, dh), F32), jax.ShapeDtypeStruct((2, B, H, V7X_SUBLANES, dh), F32))
    from_zero = state is None
    return pl.pallas_call(
        functools.partial(_mlstm_kernel, from_zero=from_zero, keep_state=keep_state),
        grid=(B, H // hp, nc),
        in_specs=[fw, fw, fwt, fw, bw, bw, bwt, bw,
                  pl.BlockSpec((2, 1, hp, 1, V7X_SUBLANES, T), lambda b, h, t: (0, b, h, chunk0 + t, 0, 0))]
        + ([] if from_zero else [cst, nst]),
        out_specs=(fw, bw) + ((cst, nst) if keep_state else ()),
        out_shape=(hshape, hshape) + (sshape if keep_state else ()),
        scratch_shapes=[pltpu.VMEM((2, hp, dh, dh), F32), pltpu.VMEM((2, hp, V7X_SUBLANES, dh), F32)],
        compiler_params=_cparams("parallel", "parallel", "arbitrary"),
        name="mlstm_scan",
    )(q, k, kt, v, q, k, kt, v, aux, *(() if from_zero else state))


def _ml_out_down_kernel(hf_ref, hb_ref, xc_ref, z_ref, nw_ref, sk_ref, w_ref, gate_ref, res_ref, o_ref, a_scr):
    tm, inner = hf_ref.shape[1:]
    dh = inner // ML_HEADS
    for r in range(OUT_DOWN_SPLIT):
        rs = slice(r * (tm // OUT_DOWN_SPLIT), (r + 1) * (tm // OUT_DOWN_SPLIT))
        for j in range(ML_HEADS):
            sl = slice(j * dh, (j + 1) * dh)
            seg = hf_ref[0, rs, sl].astype(F32) + hb_ref[0, rs, sl].astype(F32)
            z = z_ref[0, rs, sl].astype(F32)
            mu = jnp.mean(seg, axis=-1, keepdims=True)
            cen = seg - mu
            var = jnp.mean(cen * cen, axis=-1, keepdims=True)
            hn = cen * lax.rsqrt(var + ML_NORM_EPS)
            hs = hn * nw_ref[:, sl] + sk_ref[:, sl] * xc_ref[0, rs, sl].astype(F32)
            a_scr[rs, sl] = (hs * (z * jax.nn.sigmoid(z))).astype(BF16)
        o_ref[0, rs] = res_ref[0, rs] + gate_ref[0] * _dot(a_scr[rs], w_ref[...])


OUT_DOWN_SPLIT = 2


def ml_out_down(hf, hb, xc, xz, norm_w, skip, w_down_b, gate, res, tm=512):
    B, L, inner = hf.shape
    D = w_down_b.shape[1]
    tm = _row_tile(L, tm)
    vec = pl.BlockSpec((1, inner), lambda b, i: (0, 0))
    dat = pl.BlockSpec((1, tm, inner), lambda b, i: (b, i, 0))
    row = pl.BlockSpec((1, tm, D), lambda b, i: (b, i, 0))
    return pl.pallas_call(
        _ml_out_down_kernel,
        grid=(B, L // tm),
        in_specs=[dat, dat, dat, pl.BlockSpec((1, tm, inner), lambda b, i: (b, i, 1)), vec, vec,
                  pl.BlockSpec((inner, D), lambda b, i: (0, 0)), pl.BlockSpec((1, 1, D), lambda b, i: (b, 0, 0)), row],
        out_specs=row,
        out_shape=jax.ShapeDtypeStruct((B, L, D), F32),
        scratch_shapes=[pltpu.VMEM((tm, inner), BF16)],
        compiler_params=_cparams("parallel", "parallel"),
        name="ml_out_down",
    )(hf, hb, xc, xz, norm_w.reshape(1, inner), skip.reshape(1, inner), w_down_b, gate, res)


def mlstm_mix_residual(lat_args, ctx_args, p, gate):
    w_in, conv_w, conv_b, wq, wk, wv, w_gate, b_gate, norm_w, skip, w_down = p
    w_in_b = w_in.astype(BF16)
    zero_b = jnp.zeros((w_in.shape[1],), F32)

    def prep(args):
        x, g, shift, scale = args
        xz = norm_mod_matmul(x, g, shift, scale, w_in_b, zero_b, tm=512)
        return ml_prep(xz, conv_w, conv_b, wq, wk, wv, w_gate, b_gate) + (xz,)

    qc, kc, ktc, vc, _, gates_c, _ = prep(ctx_args)
    q, k, kt, v, xc, gates_l, xz = prep(lat_args)
    gates, _ = _gates_scan_order(gates_c, gates_l)
    aux = ml_gates(gates)
    _, _, c1, n1 = mlstm_scan(qc, kc, ktc, vc, aux, 0, keep_state=True)
    hf, hb = mlstm_scan(q, k, kt, v, aux, qc.shape[1] // ML_CHUNK, state=(c1, n1))
    return ml_out_down(hf, hb, xc, xz, norm_w, skip, w_down.astype(BF16), gate, lat_args[0])


def _ffn_kernel(x_ref, xp_ref, xn_ref, ng_ref, sh_ref, sc_ref, wu_ref, cw_ref, cb_ref, wd_ref,
                gate_ref, fg_ref, o_ref, u_scr, *, cols, vertical, final_norm, cb):
    i = pl.program_id(1)
    last = pl.num_programs(1) - 1
    tm = x_ref.shape[1]
    F = wd_ref.shape[0]
    halo = cols if vertical else 0

    def norm_mod(x):
        y = x * lax.rsqrt(jnp.mean(x * x, axis=-1, keepdims=True) + EPS)
        return ((y * ng_ref[...]) * (1.0 + sc_ref[0]) + sh_ref[0]).astype(BF16)

    u_scr[halo:halo + tm] = norm_mod(x_ref[0])
    if vertical:
        u_scr[0:halo] = norm_mod(xp_ref[0])
        u_scr[halo + tm:] = norm_mod(xn_ref[0])
    R = tm + 2 * halo
    rowi = lax.broadcasted_iota(jnp.int32, (R, 1), 0)
    cpos = jnp.bitwise_and(rowi, cols - 1)
    if vertical:
        top_ok = jnp.where(i > 0, 1.0, 0.0)
        bot_ok = jnp.where(i < last, 1.0, 0.0)
        rowmask = jnp.where(rowi < halo, top_ok, jnp.where(rowi >= halo + tm, bot_ok, 1.0))
    acc = jnp.zeros((tm, o_ref.shape[-1]), F32)
    for f in range(F // cb):
        fs = slice(f * cb, (f + 1) * cb)
        g = _dot(u_scr[...], wu_ref[:, F + f * cb:F + (f + 1) * cb])
        a = _dot(u_scr[halo:halo + tm], wu_ref[:, fs])
        if vertical:
            g = g * rowmask
        left = jnp.where(cpos == 0, 0.0, pltpu.roll(g, 1, 0))
        right = jnp.where(cpos == cols - 1, 0.0, pltpu.roll(g, R - 1, 0))
        conv = cb_ref[:, fs]
        for dr in (range(3) if vertical else (1,)):
            sl = slice(dr * halo, dr * halo + tm)
            conv = conv + (cw_ref[3 * dr:3 * dr + 1, fs] * left[sl] + cw_ref[3 * dr + 1:3 * dr + 2, fs] * g[sl]
                           + cw_ref[3 * dr + 2:3 * dr + 3, fs] * right[sl])
        act = ((conv * jax.nn.sigmoid(conv)) * a).astype(BF16)
        acc = acc + _dot(act, wd_ref[fs, :])
    x = x_ref[0] + gate_ref[0] * acc
    if final_norm:
        x = (x * lax.rsqrt(jnp.mean(x * x, axis=-1, keepdims=True) + EPS)) * fg_ref[...]
    o_ref[0] = x


def conv_ffn_residual(x, g, shift, scale, gate, w_up_b, conv_w, conv_b, w_down_b, rows, cols, final_g=None,
                      tm=512):
    B, L, D = x.shape
    F = w_down_b.shape[0]
    assert cols & (cols - 1) == 0 and rows * cols == L
    vertical = rows > 1
    tm = _row_tile(L, tm) if vertical else L
    cb = F
    assert tm % cols == 0
    hb = cols if vertical else HALO_ROWS
    nhb = L // hb
    final_norm = final_g is not None
    fg = final_g if final_norm else jnp.ones((D,), F32)
    vecb = pl.BlockSpec((1, 1, D), lambda b, i: (b, 0, 0))
    full = lambda shape: pl.BlockSpec(shape, lambda b, i: (0, 0), pipeline_mode=pl.Buffered(1))
    row = pl.BlockSpec((1, tm, D), lambda b, i: (b, i, 0))
    return pl.pallas_call(
        functools.partial(_ffn_kernel, cols=cols, vertical=vertical, final_norm=final_norm, cb=cb),
        grid=(B, L // tm),
        in_specs=[row,
                  pl.BlockSpec((1, hb, D), lambda b, i: (b, jnp.maximum(i * (tm // hb) - 1, 0), 0)),
                  pl.BlockSpec((1, hb, D), lambda b, i: (b, jnp.minimum((i + 1) * (tm // hb), nhb - 1), 0)),
                  full((1, D)), vecb, vecb, full((D, 2 * F)), full((9, F)), full((1, F)), full((F, D)),
                  vecb, full((1, D))],
        out_specs=row,
        out_shape=jax.ShapeDtypeStruct((B, L, D), F32),
        scratch_shapes=[pltpu.VMEM((tm + (2 * cols if vertical else 0), D), BF16)],
        compiler_params=_cparams("parallel", "arbitrary"),
        name="conv_ffn",
    )(x, x, x, g.reshape(1, D), shift, scale, w_up_b, conv_w.reshape(9, F), conv_b.reshape(1, F),
      w_down_b, gate, fg.reshape(1, D))


def kernel(x, c, ctx, c_ctx, mod_w, mod_b, norm_g, final_g, hy_w_in, hy_b_in, hy_sc_w, hy_sc_b, hy_f_w1, hy_f_b1, hy_f_w2, hy_f_b2, hy_f_w3, hy_f_b3, hy_f_wout, hy_freq, hy_bias, hy_w_out, hy_b_out, ml_w_in, ml_conv_w, ml_conv_b, ml_wq, ml_wk, ml_wv, ml_w_gate, ml_b_gate, ml_norm_w, ml_skip, ml_w_down, ffn_w_up, ffn_conv_w, ffn_conv_b, ffn_w_down):
    B, L, D = x.shape
    ctx_len = ctx.shape[1]
    depth = mod_w.shape[0]
    n_mixers = 2
    rows = L // GRID_W
    hy_params = (hy_w_in, hy_b_in, hy_sc_w, hy_sc_b, hy_f_w1, hy_f_b1, hy_f_w2, hy_f_b2,
                 hy_f_w3, hy_f_b3, hy_f_wout, hy_freq, hy_bias)
    ml_params = (ml_w_in, ml_conv_w, ml_conv_b, ml_wq, ml_wk, ml_wv, ml_w_gate, ml_b_gate,
                 ml_norm_w, ml_skip, ml_w_down)
    cond = jnp.concatenate([c, c_ctx.reshape(1, D), jnp.zeros((V7X_SUBLANES - B - 1, D), F32)], axis=0)
    for i in range(depth):
        last = i == depth - 1
        mod = adaln(cond, mod_w[i], mod_b[i])
        lat = [mod[:B, k * D:(k + 1) * D].reshape(B, 1, D) for k in range(6)]
        cm = [jnp.broadcast_to(mod[B:B + 1, k * D:(k + 1) * D].reshape(1, 1, D), (B, 1, D)) for k in range(6)]
        lat_args = (x, norm_g[i, 0], lat[0], lat[1])
        ctx_args = (ctx, norm_g[i, 0], cm[0], cm[1])
        j = i // n_mixers
        if i % n_mixers == 0:
            p = tuple(a[j] for a in hy_params)
            w_out_b = hy_w_out[j].astype(BF16)
            x = res_gate_matmul(hyena_mix_pre(lat_args, p, dense=False), w_out_b, hy_b_out[j], lat[2], x)
            if not last:
                ctx = res_gate_matmul(hyena_mix_pre(ctx_args, p, dense=True), w_out_b, hy_b_out[j], cm[2], ctx)
        else:
            assert last, "the mLSTM mixer is only implemented for the last layer (no context output)"
            p = tuple(a[j] for a in ml_params)
            x = mlstm_mix_residual(lat_args, ctx_args, p, lat[2])
        w_up_b = ffn_w_up[i].astype(BF16)
        w_down_b = ffn_w_down[i].astype(BF16)
        x = conv_ffn_residual(x, norm_g[i, 1], lat[3], lat[4], lat[5], w_up_b, ffn_conv_w[i], ffn_conv_b[i],
                              w_down_b, rows, GRID_W, final_g if last else None)
        if not last:
            ctx = conv_ffn_residual(ctx, norm_g[i, 1], cm[3], cm[4], cm[5], w_up_b, ffn_conv_w[i],
                                    ffn_conv_b[i], w_down_b, 1, ctx_len)
    return x
```

```python
import functools
import math

import jax
import jax.numpy as jnp
from jax import lax
from jax.experimental import pallas as pl
from jax.experimental.pallas import tpu as pltpu

F32 = jnp.float32
BF16 = jnp.bfloat16
HIGHEST = lax.Precision.HIGHEST

EPS = 1e-6
ML_NORM_EPS = 1e-5
GRID_W = 64
ML_HEADS = 4
ML_CHUNK = 128
ML_QKV_BLOCK = 4
HY_EMB_BANDS = 16
HY_MAX_DECAY = math.log(1e-2) / 0.3
HY_MIN_DECAY = math.log(1e-2) / 1.5

V7X_LANES = 128
V7X_SUBLANES = 8
V7X_MXU_DIM = 256
V7X_VMEM_BYTES = 64 * 1024 * 1024
VMEM_LIMIT = V7X_VMEM_BYTES - 8 * 1024 * 1024
HALO_ROWS = 16


def _cparams(*sem):
    return pltpu.CompilerParams(dimension_semantics=("arbitrary",) * len(sem), vmem_limit_bytes=VMEM_LIMIT)


def _dot(a, b, precision=None):
    return jnp.dot(a, b, preferred_element_type=F32, precision=precision)


def _dot_bf16x3(a, b):
    a_hi = a.astype(BF16)
    b_hi = b.astype(BF16)
    a_lo = (a - a_hi.astype(F32)).astype(BF16)
    b_lo = (b - b_hi.astype(F32)).astype(BF16)
    return _dot(a_hi, b_hi) + (_dot(a_lo, b_hi) + _dot(a_hi, b_lo))


def _row_tile(n, target):
    t = min(n, target)
    assert n % t == 0, (n, t)
    return t


def _adaln_kernel(c_ref, w_ref, b_ref, o_ref):
    c = c_ref[...]
    s = c * jax.nn.sigmoid(c)
    o_ref[...] = _dot(s, w_ref[...], HIGHEST) + b_ref[...]


def adaln(cond8, w, b):
    r, d = cond8.shape
    n = w.shape[1]
    tn = _row_tile(n, 1536)
    return pl.pallas_call(
        _adaln_kernel,
        grid=(n // tn,),
        in_specs=[pl.BlockSpec((r, d), lambda j: (0, 0)),
                  pl.BlockSpec((d, tn), lambda j: (0, j)),
                  pl.BlockSpec((1, tn), lambda j: (0, j))],
        out_specs=pl.BlockSpec((r, tn), lambda j: (0, j)),
        out_shape=jax.ShapeDtypeStruct((r, n), F32),
        compiler_params=_cparams("arbitrary"),
        name="adaln",
    )(cond8, w, b.reshape(1, n))


def _nmm_kernel(x_ref, g_ref, sh_ref, sc_ref, w_ref, b_ref, o_ref):
    x = x_ref[0]
    y = x * lax.rsqrt(jnp.mean(x * x, axis=-1, keepdims=True) + EPS)
    u = (y * g_ref[...]) * (1.0 + sc_ref[0]) + sh_ref[0]
    acc = _dot(u.astype(BF16), w_ref[...])
    o_ref[0] = (acc + b_ref[...]).astype(o_ref.dtype)


def norm_mod_matmul(x, g, shift, scale, w_bf16, bias, tm, out_dtype=BF16):
    B, L, D = x.shape
    n = w_bf16.shape[1]
    tm = _row_tile(L, tm)
    return pl.pallas_call(
        _nmm_kernel,
        grid=(B, L // tm),
        in_specs=[pl.BlockSpec((1, tm, D), lambda b, i: (b, i, 0)),
                  pl.BlockSpec((1, D), lambda b, i: (0, 0)),
                  pl.BlockSpec((1, 1, D), lambda b, i: (b, 0, 0)),
                  pl.BlockSpec((1, 1, D), lambda b, i: (b, 0, 0)),
                  pl.BlockSpec((D, n), lambda b, i: (0, 0)),
                  pl.BlockSpec((1, n), lambda b, i: (0, 0))],
        out_specs=pl.BlockSpec((1, tm, n), lambda b, i: (b, i, 0)),
        out_shape=jax.ShapeDtypeStruct((B, L, n), out_dtype),
        compiler_params=_cparams("parallel", "parallel"),
        name="norm_mod_matmul",
    )(x, g.reshape(1, D), shift, scale, w_bf16, bias.reshape(1, n))


def _rgm_kernel(a_ref, w_ref, b_ref, gate_ref, res_ref, fg_ref, o_ref, *, final_norm):
    acc = _dot(a_ref[0], w_ref[...]) + b_ref[...]
    x = res_ref[0] + gate_ref[0] * acc
    if final_norm:
        x = (x * lax.rsqrt(jnp.mean(x * x, axis=-1, keepdims=True) + EPS)) * fg_ref[...]
    o_ref[0] = x


def res_gate_matmul(a_bf16, w_bf16, bias, gate, res, final_g=None, tm=512):
    B, L, K = a_bf16.shape
    D = w_bf16.shape[1]
    tm = _row_tile(L, tm)
    final_norm = final_g is not None
    fg = final_g if final_norm else jnp.ones((D,), F32)
    return pl.pallas_call(
        functools.partial(_rgm_kernel, final_norm=final_norm),
        grid=(B, L // tm),
        in_specs=[pl.BlockSpec((1, tm, K), lambda b, i: (b, i, 0)),
                  pl.BlockSpec((K, D), lambda b, i: (0, 0)),
                  pl.BlockSpec((1, D), lambda b, i: (0, 0)),
                  pl.BlockSpec((1, 1, D), lambda b, i: (b, 0, 0)),
                  pl.BlockSpec((1, tm, D), lambda b, i: (b, i, 0)),
                  pl.BlockSpec((1, D), lambda b, i: (0, 0))],
        out_specs=pl.BlockSpec((1, tm, D), lambda b, i: (b, i, 0)),
        out_shape=jax.ShapeDtypeStruct((B, L, D), F32),
        compiler_params=_cparams("parallel", "parallel"),
        name="res_gate_matmul",
    )(a_bf16, w_bf16, bias.reshape(1, D), gate, res, fg.reshape(1, D))


def _halo_specs(tm, L, C, cmap):
    r = HALO_ROWS
    nb = L // r
    prev = pl.BlockSpec((1, r, C), lambda b, i, *a: (b, jnp.maximum(i * (tm // r) - 1, 0), cmap(*a)))
    nxt = pl.BlockSpec((1, r, C), lambda b, i, *a: (b, jnp.minimum((i + 1) * (tm // r), nb - 1), cmap(*a)))
    return prev, nxt


def _conv3_rows(x, prev_ref, next_ref, w_ref, b_ref):
    tm = x.shape[0]
    i = pl.program_id(1)
    last = pl.num_programs(1) - 1
    prev_row = jnp.where(i > 0, prev_ref[0].astype(F32)[HALO_ROWS - 1:HALO_ROWS, :], 0.0)
    next_row = jnp.where(i < last, next_ref[0].astype(F32)[0:1, :], 0.0)
    row = lax.broadcasted_iota(jnp.int32, x.shape, 0)
    xm1 = jnp.where(row == 0, prev_row, pltpu.roll(x, 1, 0))
    xp1 = jnp.where(row == tm - 1, next_row, pltpu.roll(x, tm - 1, 0))
    return w_ref[0:1, :] * xm1 + w_ref[1:2, :] * x + w_ref[2:3, :] * xp1 + b_ref[...]


def _x_halo_specs(tm, L, D):
    r = HALO_ROWS
    nb = L // r
    return (pl.BlockSpec((1, tm, D), lambda b, i: (b, i, 0)),
            pl.BlockSpec((1, r, D), lambda b, i: (b, jnp.maximum(i * (tm // r) - 1, 0), 0)),
            pl.BlockSpec((1, r, D), lambda b, i: (b, jnp.minimum((i + 1) * (tm // r), nb - 1), 0)))


def _norm_mod_ext(x_ref, xp_ref, xn_ref, ng_ref, sh_ref, sc_ref, u_scr):
    tm = x_ref.shape[1]
    h = HALO_ROWS

    def norm_mod(x):
        y = x * lax.rsqrt(jnp.mean(x * x, axis=-1, keepdims=True) + EPS)
        return ((y * ng_ref[...]) * (1.0 + sc_ref[0]) + sh_ref[0]).astype(BF16)

    u_scr[0:h] = norm_mod(xp_ref[0])
    u_scr[h:h + tm] = norm_mod(x_ref[0])
    u_scr[h + tm:] = norm_mod(xn_ref[0])
    i = pl.program_id(1)
    rowi = lax.broadcasted_iota(jnp.int32, (tm + 2 * h, 1), 0)
    top_ok = jnp.where(i > 0, 1.0, 0.0)
    bot_ok = jnp.where(i < pl.num_programs(1) - 1, 1.0, 0.0)
    return jnp.where(rowi < h, top_ok, jnp.where(rowi >= h + tm, bot_ok, 1.0))


def _conv3_ext(p, w_ref, b_ref, cs, tm):
    h = HALO_ROWS
    R = p.shape[0]
    return (w_ref[0:1, cs] * pltpu.roll(p, 1, 0)[h:h + tm] + w_ref[1:2, cs] * p[h:h + tm]
            + w_ref[2:3, cs] * pltpu.roll(p, R - 1, 0)[h:h + tm] + b_ref[:, cs])


def _hy_in_gate_kernel(x_ref, xp_ref, xn_ref, ng_ref, sh_ref, sc_ref, w_ref, b_ref, cw_ref, cb_ref,
                       x0_ref, z_ref, u_scr, *, cb):
    tm, D = x_ref.shape[1:]
    rowmask = _norm_mod_ext(x_ref, xp_ref, xn_ref, ng_ref, sh_ref, sc_ref, u_scr)
    for c in range(D // cb):
        conv = []
        for k in range(3):
            cs = slice(k * D + c * cb, k * D + (c + 1) * cb)
            p = (_dot(u_scr[...], w_ref[:, cs]) + b_ref[:, cs]) * rowmask
            conv.append(_conv3_ext(p, cw_ref, cb_ref, cs, tm))
        cs = slice(c * cb, (c + 1) * cb)
        x0_ref[0, :, cs] = conv[0]
        z_ref[0, :, cs] = conv[1] * conv[2]


def _hy_in_gate_perm_kernel(x_ref, xp_ref, xn_ref, ng_ref, sh_ref, sc_ref, w_ref, b_ref, cw_ref, cb_ref,
                            x0_ref, z_ref, u_scr):
    tm, D = x_ref.shape[1:]
    n2, J = x0_ref.shape[1:3]
    i = pl.program_id(1)

    def norm_mod(x):
        y = x * lax.rsqrt(jnp.mean(x * x, axis=-1, keepdims=True) + EPS)
        return (y * ng_ref[...]) * (1.0 + sc_ref[0]) + sh_ref[0]

    u = pltpu.einshape("jnc->njc", norm_mod(x_ref[0]).reshape(J, n2, D)).reshape(tm, D)
    u_scr[0:tm] = u.astype(BF16)
    u_scr[tm:tm + J] = norm_mod(xp_ref[0][HALO_ROWS - J:]).astype(BF16)
    u_scr[tm + J:] = norm_mod(xn_ref[0][:J]).astype(BF16)
    top_ok = jnp.where(i > 0, 1.0, 0.0)
    bot_ok = jnp.where(i < pl.num_programs(1) - 1, 1.0, 0.0)
    jrow = lax.broadcasted_iota(jnp.int32, (J, 1), 0)
    conv = []
    for k in range(3):
        cs = slice(k * D, (k + 1) * D)
        p = _dot(u_scr[...], w_ref[:, cs]) + b_ref[:, cs]
        pm = p[0:tm]
        before = p[tm + J - 1:tm + J] * top_ok
        after = p[tm + J:tm + J + 1] * bot_ok
        first_prev = jnp.where(jrow == 0, before, pltpu.roll(pm[tm - J:tm], 1, 0))
        last_next = jnp.where(jrow == J - 1, after, pltpu.roll(pm[0:J], J - 1, 0))
        xm1 = jnp.concatenate([first_prev, pm[0:tm - J]], axis=0)
        xp1 = jnp.concatenate([pm[J:tm], last_next], axis=0)
        conv.append(cw_ref[0:1, cs] * xm1 + cw_ref[1:2, cs] * pm + cw_ref[2:3, cs] * xp1 + cb_ref[:, cs])
    x0_ref[0] = conv[0].reshape(n2, J, D)
    z_ref[0] = (conv[1] * conv[2]).reshape(n2, J, D)


def hy_in_gate(x, g, shift, scale, w_bf16, b_in, sc_w, sc_b, perm_n2=None):
    B, L, D = x.shape
    C3 = w_bf16.shape[1]
    perm = perm_n2 is not None
    tm = V7X_SUBLANES * perm_n2 if perm else L
    assert L % tm == 0
    full = lambda shape: pl.BlockSpec(shape, lambda b, i: (0, 0))
    vecb = pl.BlockSpec((1, 1, D), lambda b, i: (b, 0, 0))
    if perm:
        out = jax.ShapeDtypeStruct((B, perm_n2, L // perm_n2, D), F32)
        ospec = pl.BlockSpec((1, perm_n2, V7X_SUBLANES, D), lambda b, i: (b, 0, i, 0))
    else:
        out = jax.ShapeDtypeStruct((B, L, D), F32)
        ospec = pl.BlockSpec((1, tm, D), lambda b, i: (b, i, 0))
    kern = _hy_in_gate_perm_kernel if perm else functools.partial(_hy_in_gate_kernel, cb=D)
    return pl.pallas_call(
        kern,
        grid=(B, L // tm),
        in_specs=[*_x_halo_specs(tm, L, D), full((1, D)), vecb, vecb, full((D, C3)), full((1, C3)),
                  full((3, C3)), full((1, C3))],
        out_specs=(ospec, ospec),
        out_shape=(out, out),
        scratch_shapes=[pltpu.VMEM((tm + 2 * (V7X_SUBLANES if perm else HALO_ROWS), D), BF16)],
        compiler_params=_cparams("parallel", "arbitrary"),
        name="hy_in_gate",
    )(x, x, x, g.reshape(1, D), shift, scale, w_bf16, b_in.reshape(1, C3), sc_w, sc_b.reshape(1, C3))


def _filter_kernel(w1_ref, b1_ref, w2_ref, b2_ref, w3_ref, b3_ref, wo_ref, fr_ref, band_ref,
                   h_ref, s_ref, *, L, D, tl, perm):
    i = pl.program_id(0)
    rowi = lax.broadcasted_iota(jnp.int32, (tl, 1), 0)
    if perm:
        n2, J = h_ref.shape[1:3]
        rowi = jnp.bitwise_and(rowi, J - 1) * n2 + jnp.right_shift(rowi, J.bit_length() - 1)
    rowi = rowi + i * tl
    row = rowi.astype(F32)
    t = row / (L - 1.0)
    w = (2.0 * math.pi) * row / L
    col = lax.broadcasted_iota(jnp.int32, (tl, V7X_LANES), 1)
    ang = w * band_ref[...]
    pos = jnp.where(col == 0, t,
                    jnp.where(col <= HY_EMB_BANDS, jnp.cos(ang),
                              jnp.where(col <= 2 * HY_EMB_BANDS, -jnp.sin(ang), 0.0)))
    fr = fr_ref[...]
    h = jnp.sin(fr * (_dot(pos, w1_ref[...], HIGHEST) + b1_ref[...]))
    h = jnp.sin(fr * (_dot(h, w2_ref[...], HIGHEST) + b2_ref[...]))
    h = jnp.sin(fr * (_dot(h, w3_ref[...], HIGHEST) + b3_ref[...]))
    h = _dot_bf16x3(h, wo_ref[...])
    dcol = lax.broadcasted_iota(jnp.int32, (1, 2 * D), 1)
    chan = jnp.where(dcol >= D, dcol - D, dcol).astype(F32)
    deltas = HY_MIN_DECAY + chan * ((HY_MAX_DECAY - HY_MIN_DECAY) / (D - 1.0))
    h = h * jnp.exp(-t * jnp.abs(deltas))
    h = jnp.where((rowi == 0) & (dcol >= D), 0.0, h)
    h_ref[...] = h.reshape(h_ref.shape)
    part = jnp.sum(jnp.abs(h), axis=0, keepdims=True)

    @pl.when(i == 0)
    def _():
        s_ref[...] = part

    @pl.when(i > 0)
    def _():
        s_ref[...] += part


def hyena_filter(L, f_w1, f_b1, f_w2, f_b2, f_w3, f_b3, f_wout, freq, perm_n2=None):
    W = f_w2.shape[0]
    D2 = f_wout.shape[1]
    P = V7X_LANES
    pad2 = lambda a: jnp.pad(a, ((0, P - a.shape[0]), (0, P - a.shape[1])))
    padv = lambda a: jnp.pad(a.reshape(1, -1), ((0, 0), (0, P - a.shape[0])))
    w1 = pad2(f_w1)
    w2 = pad2(f_w2)
    w3 = pad2(f_w3)
    wo = jnp.pad(f_wout, ((0, P - W), (0, 0)))
    bands = jnp.linspace(1e-4, HY_EMB_BANDS - 1, HY_EMB_BANDS, dtype=F32)
    band_row = jnp.concatenate([jnp.zeros((1,), F32), bands, bands,
                                jnp.zeros((P - 1 - 2 * HY_EMB_BANDS,), F32)]).reshape(1, P)
    perm = perm_n2 is not None
    tl = V7X_SUBLANES * perm_n2 if perm else _row_tile(L, 512)
    assert L % tl == 0
    full = lambda shape: pl.BlockSpec(shape, lambda i: (0, 0))
    if perm:
        hshape = jax.ShapeDtypeStruct((1, perm_n2, L // perm_n2, D2), F32)
        hspec = pl.BlockSpec((1, perm_n2, V7X_SUBLANES, D2), lambda i: (0, 0, i, 0))
    else:
        hshape = jax.ShapeDtypeStruct((L, D2), F32)
        hspec = pl.BlockSpec((tl, D2), lambda i: (i, 0))
    return pl.pallas_call(
        functools.partial(_filter_kernel, L=L, D=D2 // 2, tl=tl, perm=perm),
        grid=(L // tl,),
        in_specs=[full((P, P)), full((1, P)), full((P, P)), full((1, P)), full((P, P)), full((1, P)),
                  full((P, D2)), full((1, P)), full((1, P))],
        out_specs=(hspec, pl.BlockSpec((1, D2), lambda i: (0, 0))),
        out_shape=(hshape, jax.ShapeDtypeStruct((1, D2), F32)),
        compiler_params=_cparams("arbitrary"),
        name="hyena_filter",
    )(w1, padv(f_b1), w2, padv(f_b2), w3, padv(f_b3), wo, padv(freq), band_row)


DFT_DTYPE = BF16
K1_GROUP = HALO_ROWS


def _cos_sin(p, n):
    ang = (2.0 * math.pi / n) * p.astype(F32)
    return jnp.cos(ang), jnp.sin(ang)


def _dft_dot(g, x):
    return _dot(g.astype(DFT_DTYPE), x.astype(DFT_DTYPE), HIGHEST if DFT_DTYPE == F32 else None)


def _dft_tables(N1, N2):
    N = N1 * N2
    ar = lambda n: jnp.arange(n, dtype=jnp.int32)
    c, s = _cos_sin((ar(N1)[:, None] * ar(N1 // 2)[None, :]) % N1, N1)
    g1 = jnp.concatenate([c, -s], axis=0)
    k1 = ar(N1)[:, None, None]
    a = ar(N2)[None, :, None]
    b = ar(N2)[None, None, :]
    c, s = _cos_sin((b * (k1 + N1 * a)) % N, N)
    g2 = jnp.concatenate([jnp.concatenate([c, s], axis=2), jnp.concatenate([-s, c], axis=2)], axis=1)
    c, s = jnp.swapaxes(c, 1, 2), jnp.swapaxes(s, 1, 2)
    g2i = jnp.concatenate([jnp.concatenate([c, -s], axis=2), jnp.concatenate([s, c], axis=2)], axis=1)
    c, s = _cos_sin((ar(N1 // 2)[:, None] * ar(N1)[None, :]) % N1, N1)
    g4 = jnp.concatenate([c, -s], axis=1) * (1.0 / N)
    return tuple(t.astype(DFT_DTYPE) for t in (g1, g2, g2i, g4))


def _stage1_kernel(g_ref, x_ref, o_ref, acc_ref):
    for s in range(x_ref.shape[1]):
        acc_ref[:, s, :] = _dft_dot(g_ref[...], x_ref[0, s])
    o_ref[0] = acc_ref[...].astype(o_ref.dtype)


def dft_stage1(g, xp, ns, cl):
    B, N2, K, C = xp.shape
    M = g.shape[0]
    return pl.pallas_call(
        _stage1_kernel,
        grid=(B, N2 // ns, C // cl),
        in_specs=[pl.BlockSpec((M, K), lambda b, j, c: (0, 0)),
                  pl.BlockSpec((1, ns, K, cl), lambda b, j, c: (b, j, 0, c))],
        out_specs=pl.BlockSpec((1, M, ns, cl), lambda b, j, c: (b, 0, j, c)),
        out_shape=jax.ShapeDtypeStruct((B, M, N2, C), DFT_DTYPE),
        scratch_shapes=[pltpu.VMEM((M, ns, cl), F32)],
        compiler_params=_cparams("parallel", "parallel", "parallel"),
        name="dft_stage1",
    )(g, xp)


def _combine_spectrum(X, nrm_ref, D):
    n = X.shape[0] // 2
    nrm = nrm_ref[:, :D] + nrm_ref[:, D:]
    hre = (X[:n, :D] + X[:n, D:]) / nrm
    him = (X[n:, :D] - X[n:, D:]) / nrm
    return hre, him


def _filter_spec_kernel(g_ref, a_ref, nrm_ref, o_ref):
    _, _, n2, D2 = a_ref.shape
    slab = a_ref[:, 0].reshape(2 * n2, D2)
    X = _dft_dot(g_ref[0], slab)
    hre, him = _combine_spectrum(X, nrm_ref, D2 // 2)
    o_ref[0, 0] = hre
    o_ref[1, 0] = him


def filter_spectrum(g2, a, nrm):
    _, N1, N2, D2 = a.shape
    D = D2 // 2
    return pl.pallas_call(
        _filter_spec_kernel,
        grid=(N1,),
        in_specs=[pl.BlockSpec((1, 2 * N2, 2 * N2), lambda k: (k, 0, 0)),
                  pl.BlockSpec((2, 1, N2, D2), lambda k: (0, k, 0, 0)),
                  pl.BlockSpec((1, D2), lambda k: (0, 0))],
        out_specs=pl.BlockSpec((2, 1, N2, D), lambda k: (0, k, 0, 0)),
        out_shape=jax.ShapeDtypeStruct((2, N1, N2, D), F32),
        compiler_params=_cparams("parallel"),
        name="filter_spectrum",
    )(g2, a, nrm)


def _cmul(xr, xi, hr, hi):
    return xr * hr - xi * hi, xr * hi + xi * hr


def _spec_mul_kernel(g_ref, gi_ref, h_ref, a_ref, o_ref):
    n2, C = a_ref.shape[-2:]
    ys = []
    for j in range(a_ref.shape[2]):
        slab = a_ref[0, :, j].reshape(2 * n2, C)
        X = _dft_dot(g_ref[j], slab)
        pr, pi = _cmul(X[:n2], X[n2:], h_ref[0, j], h_ref[1, j])
        ys.append(_dft_dot(gi_ref[j], jnp.concatenate([pr, pi], axis=0)))
    t = pltpu.einshape("jrc->rjc", jnp.stack(ys)).astype(o_ref.dtype)
    o_ref[0, :, 0] = t[:n2]
    o_ref[0, :, 1] = t[n2:]


def spectrum_multiply(g2, g2i, H, a, cl):
    B, _, N1, N2, C = a.shape
    kg = K1_GROUP
    gsp = pl.BlockSpec((kg, 2 * N2, 2 * N2), lambda k, c, b: (k, 0, 0))
    return pl.pallas_call(
        _spec_mul_kernel,
        grid=(N1 // kg, C // cl, B),
        in_specs=[gsp, gsp, pl.BlockSpec((2, kg, N2, cl), lambda k, c, b: (0, k, 0, c)),
                  pl.BlockSpec((1, 2, kg, N2, cl), lambda k, c, b: (b, 0, k, 0, c))],
        out_specs=pl.BlockSpec((1, N2, 2, kg, cl), lambda k, c, b: (b, 0, 0, k, c)),
        out_shape=jax.ShapeDtypeStruct((B, N2, 2, N1, C), DFT_DTYPE),
        compiler_params=_cparams("parallel", "parallel", "parallel"),
        name="spectrum_multiply",
    )(g2, g2i, H, a)


def _idft_out_kernel(g_ref, y_ref, x0_ref, z_ref, bias_ref, o_ref, acc_ref):
    for s in range(y_ref.shape[1]):
        y = _dft_dot(g_ref[...], y_ref[0, s])
        acc_ref[:, s, :] = x0_ref[0, s] * (y + z_ref[0, s] * bias_ref[...])
    o_ref[0] = acc_ref[...].astype(o_ref.dtype)


def idft_gate_out(g4, yv, x0p, zp, bias, ns, cl):
    B, N2, K, C = yv.shape
    M = g4.shape[0]
    pdat = pl.BlockSpec((1, ns, M, cl), lambda b, j, c: (b, j, 0, c))
    return pl.pallas_call(
        _idft_out_kernel,
        grid=(B, N2 // ns, C // cl),
        in_specs=[pl.BlockSpec((M, K), lambda b, j, c: (0, 0)),
                  pl.BlockSpec((1, ns, K, cl), lambda b, j, c: (b, j, 0, c)),
                  pdat, pdat, pl.BlockSpec((1, cl), lambda b, j, c: (0, c))],
        out_specs=pl.BlockSpec((1, M, ns, cl), lambda b, j, c: (b, 0, j, c)),
        out_shape=jax.ShapeDtypeStruct((B, M, N2, C), BF16),
        scratch_shapes=[pltpu.VMEM((M, ns, cl), F32)],
        compiler_params=_cparams("parallel", "parallel", "parallel"),
        name="idft_gate_out",
    )(g4, yv, x0p, zp, bias.reshape(1, C))


def long_conv_two_stage(x0p, zp, hfbp, nrm, bias):
    B, N2, M, C = zp.shape
    L = N2 * M
    N1 = 2 * M
    g1, g2, g2i, g4 = _dft_tables(N1, N2)
    ns = HALO_ROWS
    cl = min(C, 512)
    a_f = dft_stage1(g1, hfbp, ns, cl)
    H = filter_spectrum(g2, a_f.reshape(2, N1, N2, 2 * C), nrm)
    a = dft_stage1(g1, zp, ns, cl)
    yv = spectrum_multiply(g2, g2i, H, a.reshape(B, 2, N1, N2, C), cl // 2)
    out = idft_gate_out(g4, yv.reshape(B, N2, 2 * N1, C), x0p, zp, bias, ns, cl)
    return out.reshape(B, L, C)


def _dense_tables(L):
    N = 2 * L
    ar = lambda n: jnp.arange(n, dtype=jnp.int32)
    c, s = _cos_sin((ar(N)[:, None] * ar(L)[None, :]) % N, N)
    g = jnp.concatenate([c, -s], axis=0)
    c, s = _cos_sin((ar(L)[:, None] * ar(N)[None, :]) % N, N)
    gi = jnp.concatenate([c, -s], axis=1) * (1.0 / N)
    return g.astype(DFT_DTYPE), gi.astype(DFT_DTYPE)


def _dense_spec_kernel(g_ref, hfb_ref, nrm_ref, o_ref):
    X = _dft_dot(g_ref[...], hfb_ref[...])
    hre, him = _combine_spectrum(X, nrm_ref, o_ref.shape[-1])
    o_ref[0] = hre
    o_ref[1] = him


def _dense_conv_kernel(g_ref, gi_ref, h_ref, z_ref, x0_ref, bias_ref, o_ref):
    z = z_ref[0]
    X = _dft_dot(g_ref[...], z)
    n = X.shape[0] // 2
    pr, pi = _cmul(X[:n], X[n:], h_ref[0], h_ref[1])
    y = _dft_dot(gi_ref[...], jnp.concatenate([pr, pi], axis=0))
    o_ref[0] = (x0_ref[0] * (y + z * bias_ref[...])).astype(o_ref.dtype)


def long_conv_dense(x0, z, hfb, nrm, bias):
    B, L, C = z.shape
    N = 2 * L
    g, gi = _dense_tables(L)
    full2 = lambda shape: pl.BlockSpec(shape, lambda *_: (0,) * len(shape))
    H = pl.pallas_call(
        _dense_spec_kernel,
        grid=(1,),
        in_specs=[full2((2 * N, L)), full2((L, 2 * C)), full2((1, 2 * C))],
        out_specs=full2((2, N, C)),
        out_shape=jax.ShapeDtypeStruct((2, N, C), F32),
        compiler_params=_cparams("arbitrary"),
        name="dense_filter_spectrum",
    )(g, hfb, nrm)
    dat = pl.BlockSpec((1, L, C), lambda b: (b, 0, 0))
    return pl.pallas_call(
        _dense_conv_kernel,
        grid=(B,),
        in_specs=[full2((2 * N, L)), full2((L, 2 * N)), full2((2, N, C)), dat, dat, full2((1, C))],
        out_specs=dat,
        out_shape=jax.ShapeDtypeStruct((B, L, C), BF16),
        compiler_params=_cparams("parallel"),
        name="dense_long_conv",
    )(g, gi, H, z, x0, bias.reshape(1, C))


def hyena_mix_pre(u_args, p, dense):
    x, g, shift, scale = u_args
    (w_in, b_in, sc_w, sc_b, f_w1, f_b1, f_w2, f_b2, f_w3, f_b3, f_wout, freq, bias) = p
    L = x.shape[1]
    perm_n2 = None if dense else V7X_LANES
    x0, z = hy_in_gate(x, g, shift, scale, w_in.astype(BF16), b_in, sc_w, sc_b, perm_n2)
    hfb, nrm = hyena_filter(L, f_w1, f_b1, f_w2, f_b2, f_w3, f_b3, f_wout, freq, perm_n2)
    conv = long_conv_dense if dense else long_conv_two_stage
    return conv(x0, z, hfb, nrm, bias)


def _ml_prep_kernel(xm_ref, prev_ref, next_ref, cw_ref, cb_ref, wq_ref, wk_ref, wv_ref, wg_ref, bg_ref,
                    q_ref, k_ref, kt_ref, v_ref, xc_ref, g_ref, *, k_scale):
    xm = xm_ref[0].astype(F32)
    inner = xm.shape[1]
    conv = _conv3_rows(xm, prev_ref, next_ref, cw_ref, cb_ref)
    xc = conv * jax.nn.sigmoid(conv)
    xcb = xc.astype(BF16)
    xc_ref[0] = xcb
    xmb = xm.astype(BF16)
    gw = V7X_MXU_DIM
    gates = bg_ref[...]
    for j in range(inner // gw):
        sl = slice(j * gw, (j + 1) * gw)
        q = _dot(xcb[:, sl], wq_ref[j])
        k = _dot(xcb[:, sl], wk_ref[j])
        v = _dot(xmb[:, sl], wv_ref[j])
        q_ref[0, :, sl] = q.astype(BF16)
        ks = k * k_scale
        k_ref[0, :, sl] = ks.astype(BF16)
        kt_ref[0, sl, :] = ks.T.astype(BF16)
        v_ref[0, :, sl] = v.astype(BF16)
        gates = gates + _dot(q.astype(BF16), wg_ref[j * gw:(j + 1) * gw, :])
        gates = gates + _dot(k.astype(BF16), wg_ref[inner + j * gw:inner + (j + 1) * gw, :])
        gates = gates + _dot(v.astype(BF16), wg_ref[2 * inner + j * gw:2 * inner + (j + 1) * gw, :])
    g_ref[0] = gates


def _block_diag(w, group):
    nb, bs, _ = w.shape
    per = group // bs
    w = w.reshape(nb // per, per, bs, bs)
    eye = jnp.eye(per, dtype=w.dtype)
    dense = jnp.einsum("gpce,pr->gpcre", w, eye)
    return dense.reshape(nb // per, group, group).astype(BF16)


def ml_prep(xz, conv_w, conv_b, wq, wk, wv, w_gate, b_gate, tm=512):
    B, L, C2 = xz.shape
    inner = C2 // 2
    dh = inner // ML_HEADS
    tm = _row_tile(L, tm)
    gw = V7X_MXU_DIM
    ng = inner // gw
    P = V7X_LANES
    wg = jnp.pad(w_gate, ((0, 0), (0, P - w_gate.shape[1]))).astype(BF16)
    bg = jnp.pad(b_gate.reshape(1, -1), ((0, 0), (0, P - b_gate.shape[0])))
    prev, nxt = _halo_specs(tm, L, inner, lambda: 0)
    c2 = lambda shape: pl.BlockSpec(shape, lambda b, i: (0,) * len(shape))
    row = lambda n: pl.BlockSpec((1, tm, n), lambda b, i: (b, i, 0))
    sd = lambda n, dt: jax.ShapeDtypeStruct((B, L, n), dt)
    return pl.pallas_call(
        functools.partial(_ml_prep_kernel, k_scale=dh ** -0.5),
        grid=(B, L // tm),
        in_specs=[row(inner), prev, nxt, c2((3, inner)), c2((1, inner)),
                  c2((ng, gw, gw)), c2((ng, gw, gw)), c2((ng, gw, gw)), c2((3 * inner, P)), c2((1, P))],
        out_specs=(row(inner), row(inner), pl.BlockSpec((1, inner, tm), lambda b, i: (b, 0, i)), row(inner),
                   row(inner), row(P)),
        out_shape=(sd(inner, BF16), sd(inner, BF16), jax.ShapeDtypeStruct((B, inner, L), BF16), sd(inner, BF16),
                   sd(inner, BF16), sd(P, F32)),
        compiler_params=_cparams("parallel", "arbitrary"),
        name="ml_prep",
    )(xz, xz, xz, conv_w, conv_b.reshape(1, inner), _block_diag(wq, gw), _block_diag(wk, gw),
      _block_diag(wv, gw), wg, bg)


def _gates_scan_order(g_ctx, g_lat):
    T = ML_CHUNK

    def lay(g):
        B, L, _ = g.shape
        g = g[..., :4 * ML_HEADS].reshape(B, L // T, T, 2, 2, ML_HEADS)
        return jnp.transpose(g, (3, 0, 5, 4, 1, 2))

    gc, gl = lay(g_ctx), lay(g_lat)
    fwd = jnp.concatenate([gc[0], gl[0]], axis=3)
    bwd = jnp.concatenate([gc[1][..., ::-1, :], gl[1][..., ::-1, :]], axis=3)
    out = jnp.stack([fwd, bwd])
    nct = out.shape[4]
    pad = -nct % V7X_SUBLANES
    return jnp.pad(out, ((0, 0),) * 4 + ((0, pad), (0, 0))), nct


AUX_BMR, AUX_GR, AUX_WINTER, AUX_EMR, AUX_WROW, AUX_GOLD = range(6)


def _ml_gates_kernel(g_ref, aux_ref, bl_scr, me_scr, mp_scr):
    d = pl.program_id(0)
    ig = g_ref[0, 0, 0, 0]
    fg = g_ref[0, 0, 0, 1]
    nct, T = ig.shape
    lf = jnp.minimum(fg, 0.0) - jnp.log1p(jnp.exp(-jnp.abs(fg)))
    lane = lax.broadcasted_iota(jnp.int32, (nct, T), 1)
    rev = d == 1

    def scans(x, op, ident):
        f, r = x, x
        s = 1
        while s < T:
            f = op(f, jnp.where(lane >= s, pltpu.roll(f, s, 1), ident))
            r = op(r, jnp.where(lane < T - s, pltpu.roll(r, T - s, 1), ident))
            s *= 2
        return jnp.where(rev, r, f)

    bcs = scans(lf, jnp.add, 0.0)
    b_last = jnp.sum(lf, axis=1, keepdims=True)
    gr = ig - bcs
    cmax = scans(gr, jnp.maximum, -jnp.inf)
    max_e = b_last + jnp.max(gr, axis=1, keepdims=True)
    bl_scr[...] = jnp.broadcast_to(b_last, bl_scr.shape)
    me_scr[...] = jnp.broadcast_to(max_e, me_scr.shape)

    def body(t, m):
        mp_scr[pl.ds(t, 1), :] = m
        return jnp.maximum(bl_scr[pl.ds(t, 1), :] + m, me_scr[pl.ds(t, 1), :])

    lax.fori_loop(0, nct, body, jnp.zeros((1, V7X_LANES), F32))
    m_prev = mp_scr[:, 0:1]
    a_inter = bcs + m_prev
    m_row = jnp.maximum(a_inter, bcs + cmax)
    m_new = jnp.maximum(b_last + m_prev, max_e)
    rows = {AUX_BMR: bcs - m_row, AUX_GR: gr, AUX_WINTER: jnp.exp(a_inter - m_row),
            AUX_EMR: jnp.exp(-m_row), AUX_WROW: jnp.exp(b_last + gr - m_new),
            AUX_GOLD: jnp.broadcast_to(jnp.exp(b_last + m_prev - m_new), (nct, T))}
    zero = jnp.zeros((nct, T), F32)
    for k in range(V7X_SUBLANES):
        aux_ref[0, 0, 0, :, k, :] = rows.get(k, zero)


def ml_gates(gates):
    _, B, H, _, nct, T = gates.shape
    return pl.pallas_call(
        _ml_gates_kernel,
        grid=(2, B, H),
        in_specs=[pl.BlockSpec((1, 1, 1, 2, nct, T), lambda d, b, h: (d, b, h, 0, 0, 0))],
        out_specs=pl.BlockSpec((1, 1, 1, nct, V7X_SUBLANES, T), lambda d, b, h: (d, b, h, 0, 0, 0)),
        out_shape=jax.ShapeDtypeStruct((2, B, H, nct, V7X_SUBLANES, T), F32),
        scratch_shapes=[pltpu.VMEM((nct, V7X_LANES), F32)] * 3,
        compiler_params=_cparams("arbitrary", "arbitrary", "arbitrary"),
        name="ml_gates",
    )(gates)


SCAN_HEADS = 4


def _mlstm_kernel(qf_ref, kf_ref, ktf_ref, vf_ref, qb_ref, kb_ref, ktb_ref, vb_ref, aux_ref, *refs,
                  from_zero, keep_state):
    refs = list(refs)
    c0_ref, n0_ref = (None, None) if from_zero else (refs.pop(0), refs.pop(0))
    hf_ref, hb_ref = refs.pop(0), refs.pop(0)
    cf_ref, nf_ref = (refs.pop(0), refs.pop(0)) if keep_state else (None, None)
    c_scr, n_scr = refs
    t = pl.program_id(2)
    nct = pl.num_programs(2)
    T = ML_CHUNK
    dh = qf_ref.shape[-1] // SCAN_HEADS

    @pl.when(t == 0)
    def _():
        if from_zero:
            c_scr[...] = jnp.zeros_like(c_scr)
            n_scr[...] = jnp.zeros_like(n_scr)
        else:
            c_scr[...] = c0_ref[:, 0]
            n_scr[...] = n0_ref[:, 0]

    row = lax.broadcasted_iota(jnp.int32, (T, T), 0)
    col = lax.broadcasted_iota(jnp.int32, (T, T), 1)
    dirs = ((qf_ref, kf_ref, ktf_ref, vf_ref, hf_ref, col <= row), (qb_ref, kb_ref, ktb_ref, vb_ref, hb_ref, col >= row))
    streams = [(d, j) + dirs[d] for j in range(SCAN_HEADS) for d in range(2)]
    for d, j, q_ref, k_ref, kt_ref, v_ref, h_ref, mask in streams:
        hs = slice(j * dh, (j + 1) * dh)
        ax = aux_ref[d, 0, j, 0]
        axt = jnp.concatenate([ax, jnp.zeros((T - V7X_SUBLANES, T), F32)], axis=0).T
        gr = ax[AUX_GR:AUX_GR + 1, :]
        g_old = ax[AUX_GOLD:AUX_GOLD + 1, 0:1]
        bmr = axt[:, AUX_BMR:AUX_BMR + 1]
        w_inter = axt[:, AUX_WINTER:AUX_WINTER + 1]
        emr = axt[:, AUX_EMR:AUX_EMR + 1]
        w_col = axt[:, AUX_WROW:AUX_WROW + 1]
        pmat = jnp.exp(jnp.where(mask, bmr + gr, -jnp.inf))

        q = q_ref[0, :, hs]
        v = v_ref[0, :, hs]
        kT = kt_ref[0, hs, :]
        n_row = n_scr[d, j, 0:1, :]
        smat = _dot(q, kT) * pmat
        qn = jnp.sum(q.astype(F32) * n_row, axis=1, keepdims=True)
        den = w_inter * qn + jnp.sum(smat, axis=1, keepdims=True)
        num = w_inter * _dot(q, c_scr[d, j].astype(BF16)) + _dot(smat.astype(BF16), v)
        h_ref[0, :, hs] = (num * (1.0 / jnp.maximum(jnp.abs(den), emr))).astype(h_ref.dtype)
        c_scr[d, j] = g_old * c_scr[d, j] + _dot(kT, (v.astype(F32) * w_col).astype(BF16))
        dn = jnp.sum(k_ref[0, :, hs].astype(F32) * w_col, axis=0, keepdims=True)
        n_scr[d, j, 0:1, :] = g_old * n_row + dn

    if keep_state:
        @pl.when(t == nct - 1)
        def _():
            cf_ref[:, 0] = c_scr[...]
            nf_ref[:, 0] = n_scr[...]


def mlstm_scan(q, k, kt, v, aux, chunk0, state=None, keep_state=False):
    B, L, inner = q.shape
    H = ML_HEADS
    dh = inner // H
    T = ML_CHUNK
    nc = L // T
    hp = SCAN_HEADS
    fw = pl.BlockSpec((1, T, hp * dh), lambda b, h, t: (b, t, h))
    bw = pl.BlockSpec((1, T, hp * dh), lambda b, h, t: (b, nc - 1 - t, h))
    fwt = pl.BlockSpec((1, hp * dh, T), lambda b, h, t: (b, h, t))
    bwt = pl.BlockSpec((1, hp * dh, T), lambda b, h, t: (b, h, nc - 1 - t))
    cst = pl.BlockSpec((2, 1, hp, dh, dh), lambda b, h, t: (0, b, h, 0, 0))
    nst = pl.BlockSpec((2, 1, hp, V7X_SUBLANES, dh), lambda b, h, t: (0, b, h, 0, 0))
    hshape = jax.ShapeDtypeStruct((B, L, inner), BF16)
    sshape = (jax.ShapeDtypeStruct((2, B, H, dh, dh), F32), jax.ShapeDtypeStruct((2, B, H, V7X_SUBLANES, dh), F32))
    from_zero = state is None
    return pl.pallas_call(
        functools.partial(_mlstm_kernel, from_zero=from_zero, keep_state=keep_state),
        grid=(B, H // hp, nc),
        in_specs=[fw, fw, fwt, fw, bw, bw, bwt, bw,
                  pl.BlockSpec((2, 1, hp, 1, V7X_SUBLANES, T), lambda b, h, t: (0, b, h, chunk0 + t, 0, 0))]
        + ([] if from_zero else [cst, nst]),
        out_specs=(fw, bw) + ((cst, nst) if keep_state else ()),
        out_shape=(hshape, hshape) + (sshape if keep_state else ()),
        scratch_shapes=[pltpu.VMEM((2, hp, dh, dh), F32), pltpu.VMEM((2, hp, V7X_SUBLANES, dh), F32)],
        compiler_params=_cparams("parallel", "parallel", "arbitrary"),
        name="mlstm_scan",
    )(q, k, kt, v, q, k, kt, v, aux, *(() if from_zero else state))


def _ml_out_down_kernel(hf_ref, hb_ref, xc_ref, z_ref, nw_ref, sk_ref, w_ref, gate_ref, res_ref, o_ref, a_scr):
    tm, inner = hf_ref.shape[1:]
    dh = inner // ML_HEADS
    for r in range(OUT_DOWN_SPLIT):
        rs = slice(r * (tm // OUT_DOWN_SPLIT), (r + 1) * (tm // OUT_DOWN_SPLIT))
        for j in range(ML_HEADS):
            sl = slice(j * dh, (j + 1) * dh)
            seg = hf_ref[0, rs, sl].astype(F32) + hb_ref[0, rs, sl].astype(F32)
            z = z_ref[0, rs, sl].astype(F32)
            mu = jnp.mean(seg, axis=-1, keepdims=True)
            cen = seg - mu
            var = jnp.mean(cen * cen, axis=-1, keepdims=True)
            hn = cen * lax.rsqrt(var + ML_NORM_EPS)
            hs = hn * nw_ref[:, sl] + sk_ref[:, sl] * xc_ref[0, rs, sl].astype(F32)
            a_scr[rs, sl] = (hs * (z * jax.nn.sigmoid(z))).astype(BF16)
        o_ref[0, rs] = res_ref[0, rs] + gate_ref[0] * _dot(a_scr[rs], w_ref[...])


OUT_DOWN_SPLIT = 2


def ml_out_down(hf, hb, xc, xz, norm_w, skip, w_down_b, gate, res, tm=512):
    B, L, inner = hf.shape
    D = w_down_b.shape[1]
    tm = _row_tile(L, tm)
    vec = pl.BlockSpec((1, inner), lambda b, i: (0, 0))
    dat = pl.BlockSpec((1, tm, inner), lambda b, i: (b, i, 0))
    row = pl.BlockSpec((1, tm, D), lambda b, i: (b, i, 0))
    return pl.pallas_call(
        _ml_out_down_kernel,
        grid=(B, L // tm),
        in_specs=[dat, dat, dat, pl.BlockSpec((1, tm, inner), lambda b, i: (b, i, 1)), vec, vec,
                  pl.BlockSpec((inner, D), lambda b, i: (0, 0)), pl.BlockSpec((1, 1, D), lambda b, i: (b, 0, 0)), row],
        out_specs=row,
        out_shape=jax.ShapeDtypeStruct((B, L, D), F32),
        scratch_shapes=[pltpu.VMEM((tm, inner), BF16)],
        compiler_params=_cparams("parallel", "parallel"),
        name="ml_out_down",
    )(hf, hb, xc, xz, norm_w.reshape(1, inner), skip.reshape(1, inner), w_down_b, gate, res)


def mlstm_mix_residual(lat_args, ctx_args, p, gate):
    w_in, conv_w, conv_b, wq, wk, wv, w_gate, b_gate, norm_w, skip, w_down = p
    w_in_b = w_in.astype(BF16)
    zero_b = jnp.zeros((w_in.shape[1],), F32)

    def prep(args):
        x, g, shift, scale = args
        xz = norm_mod_matmul(x, g, shift, scale, w_in_b, zero_b, tm=512)
        return ml_prep(xz, conv_w, conv_b, wq, wk, wv, w_gate, b_gate) + (xz,)

    qc, kc, ktc, vc, _, gates_c, _ = prep(ctx_args)
    q, k, kt, v, xc, gates_l, xz = prep(lat_args)
    gates, _ = _gates_scan_order(gates_c, gates_l)
    aux = ml_gates(gates)
    _, _, c1, n1 = mlstm_scan(qc, kc, ktc, vc, aux, 0, keep_state=True)
    hf, hb = mlstm_scan(q, k, kt, v, aux, qc.shape[1] // ML_CHUNK, state=(c1, n1))
    return ml_out_down(hf, hb, xc, xz, norm_w, skip, w_down.astype(BF16), gate, lat_args[0])


def _ffn_kernel(x_ref, xp_ref, xn_ref, ng_ref, sh_ref, sc_ref, wu_ref, cw_ref, cb_ref, wd_ref,
                gate_ref, fg_ref, o_ref, u_scr, *, cols, vertical, final_norm, cb):
    i = pl.program_id(1)
    last = pl.num_programs(1) - 1
    tm = x_ref.shape[1]
    F = wd_ref.shape[0]
    halo = cols if vertical else 0

    def norm_mod(x):
        y = x * lax.rsqrt(jnp.mean(x * x, axis=-1, keepdims=True) + EPS)
        return ((y * ng_ref[...]) * (1.0 + sc_ref[0]) + sh_ref[0]).astype(BF16)

    u_scr[halo:halo + tm] = norm_mod(x_ref[0])
    if vertical:
        u_scr[0:halo] = norm_mod(xp_ref[0])
        u_scr[halo + tm:] = norm_mod(xn_ref[0])
    R = tm + 2 * halo
    rowi = lax.broadcasted_iota(jnp.int32, (R, 1), 0)
    cpos = jnp.bitwise_and(rowi, cols - 1)
    if vertical:
        top_ok = jnp.where(i > 0, 1.0, 0.0)
        bot_ok = jnp.where(i < last, 1.0, 0.0)
        rowmask = jnp.where(rowi < halo, top_ok, jnp.where(rowi >= halo + tm, bot_ok, 1.0))
    acc = jnp.zeros((tm, o_ref.shape[-1]), F32)
    for f in range(F // cb):
        fs = slice(f * cb, (f + 1) * cb)
        g = _dot(u_scr[...], wu_ref[:, F + f * cb:F + (f + 1) * cb])
        a = _dot(u_scr[halo:halo + tm], wu_ref[:, fs])
        if vertical:
            g = g * rowmask
        left = jnp.where(cpos == 0, 0.0, pltpu.roll(g, 1, 0))
        right = jnp.where(cpos == cols - 1, 0.0, pltpu.roll(g, R - 1, 0))
        conv = cb_ref[:, fs]
        for dr in (range(3) if vertical else (1,)):
            sl = slice(dr * halo, dr * halo + tm)
            conv = conv + (cw_ref[3 * dr:3 * dr + 1, fs] * left[sl] + cw_ref[3 * dr + 1:3 * dr + 2, fs] * g[sl]
                           + cw_ref[3 * dr + 2:3 * dr + 3, fs] * right[sl])
        act = ((conv * jax.nn.sigmoid(conv)) * a).astype(BF16)
        acc = acc + _dot(act, wd_ref[fs, :])
    x = x_ref[0] + gate_ref[0] * acc
    if final_norm:
        x = (x * lax.rsqrt(jnp.mean(x * x, axis=-1, keepdims=True) + EPS)) * fg_ref[...]
    o_ref[0] = x


def conv_ffn_residual(x, g, shift, scale, gate, w_up_b, conv_w, conv_b, w_down_b, rows, cols, final_g=None,
                      tm=512):
    B, L, D = x.shape
    F = w_down_b.shape[0]
    assert cols & (cols - 1) == 0 and rows * cols == L
    vertical = rows > 1
    tm = _row_tile(L, tm) if vertical else L
    cb = F
    assert tm % cols == 0
    hb = cols if vertical else HALO_ROWS
    nhb = L // hb
    final_norm = final_g is not None
    fg = final_g if final_norm else jnp.ones((D,), F32)
    vecb = pl.BlockSpec((1, 1, D), lambda b, i: (b, 0, 0))
    full = lambda shape: pl.BlockSpec(shape, lambda b, i: (0, 0), pipeline_mode=pl.Buffered(1))
    row = pl.BlockSpec((1, tm, D), lambda b, i: (b, i, 0))
    return pl.pallas_call(
        functools.partial(_ffn_kernel, cols=cols, vertical=vertical, final_norm=final_norm, cb=cb),
        grid=(B, L // tm),
        in_specs=[row,
                  pl.BlockSpec((1, hb, D), lambda b, i: (b, jnp.maximum(i * (tm // hb) - 1, 0), 0)),
                  pl.BlockSpec((1, hb, D), lambda b, i: (b, jnp.minimum((i + 1) * (tm // hb), nhb - 1), 0)),
                  full((1, D)), vecb, vecb, full((D, 2 * F)), full((9, F)), full((1, F)), full((F, D)),
                  vecb, full((1, D))],
        out_specs=row,
        out_shape=jax.ShapeDtypeStruct((B, L, D), F32),
        scratch_shapes=[pltpu.VMEM((tm + (2 * cols if vertical else 0), D), BF16)],
        compiler_params=_cparams("parallel", "arbitrary"),
        name="conv_ffn",
    )(x, x, x, g.reshape(1, D), shift, scale, w_up_b, conv_w.reshape(9, F), conv_b.reshape(1, F),
      w_down_b, gate, fg.reshape(1, D))


def kernel(x, c, ctx, c_ctx, mod_w, mod_b, norm_g, final_g, hy_w_in, hy_b_in, hy_sc_w, hy_sc_b, hy_f_w1, hy_f_b1, hy_f_w2, hy_f_b2, hy_f_w3, hy_f_b3, hy_f_wout, hy_freq, hy_bias, hy_w_out, hy_b_out, ml_w_in, ml_conv_w, ml_conv_b, ml_wq, ml_wk, ml_wv, ml_w_gate, ml_b_gate, ml_norm_w, ml_skip, ml_w_down, ffn_w_up, ffn_conv_w, ffn_conv_b, ffn_w_down):
    B, L, D = x.shape
    ctx_len = ctx.shape[1]
    depth = mod_w.shape[0]
    n_mixers = 2
    rows = L // GRID_W
    hy_params = (hy_w_in, hy_b_in, hy_sc_w, hy_sc_b, hy_f_w1, hy_f_b1, hy_f_w2, hy_f_b2,
                 hy_f_w3, hy_f_b3, hy_f_wout, hy_freq, hy_bias)
    ml_params = (ml_w_in, ml_conv_w, ml_conv_b, ml_wq, ml_wk, ml_wv, ml_w_gate, ml_b_gate,
                 ml_norm_w, ml_skip, ml_w_down)
    cond = jnp.concatenate([c, c_ctx.reshape(1, D), jnp.zeros((V7X_SUBLANES - B - 1, D), F32)], axis=0)
    for i in range(depth):
        last = i == depth - 1
        mod = adaln(cond, mod_w[i], mod_b[i])
        lat = [mod[:B, k * D:(k + 1) * D].reshape(B, 1, D) for k in range(6)]
        cm = [jnp.broadcast_to(mod[B:B + 1, k * D:(k + 1) * D].reshape(1, 1, D), (B, 1, D)) for k in range(6)]
        lat_args = (x, norm_g[i, 0], lat[0], lat[1])
        ctx_args = (ctx, norm_g[i, 0], cm[0], cm[1])
        j = i // n_mixers
        if i % n_mixers == 0:
            p = tuple(a[j] for a in hy_params)
            w_out_b = hy_w_out[j].astype(BF16)
            x = res_gate_matmul(hyena_mix_pre(lat_args, p, dense=False), w_out_b, hy_b_out[j], lat[2], x)
            if not last:
                ctx = res_gate_matmul(hyena_mix_pre(ctx_args, p, dense=True), w_out_b, hy_b_out[j], cm[2], ctx)
        else:
            assert last, "the mLSTM mixer is only implemented for the last layer (no context output)"
            p = tuple(a[j] for a in ml_params)
            x = mlstm_mix_residual(lat_args, ctx_args, p, lat[2])
        w_up_b = ffn_w_up[i].astype(BF16)
        w_down_b = ffn_w_down[i].astype(BF16)
        x = conv_ffn_residual(x, norm_g[i, 1], lat[3], lat[4], lat[5], w_up_b, ffn_conv_w[i], ffn_conv_b[i],
                              w_down_b, rows, GRID_W, final_g if last else None)
        if not last:
            ctx = conv_ffn_residual(ctx, norm_g[i, 1], cm[3], cm[4], cm[5], w_up_b, ffn_conv_w[i],
                                    ffn_conv_b[i], w_down_b, 1, ctx_len)
    return x
```
